```python
import jax, jax.numpy as jnp
from jax import lax
import numpy as np

D_MODEL = 1024
BATCH = 16
SEQ = 4096
DEPTH = 4

GRID_W = 64
CTX_LEN = 256
N_MIXERS = 2
N_GLA_LAYERS = (DEPTH + 1) // 2
N_MLA_LAYERS = DEPTH // 2
EPS = 1e-6

GLA_HEADS = 4
GLA_DK = D_MODEL // 2 // GLA_HEADS
GLA_DV = D_MODEL // GLA_HEADS
GLA_GATE_RANK = 16
GLA_GATE_NORMALIZER = 16.0
GLA_CHUNK = 64
GLA_IN = 2 * GLA_HEADS * GLA_DK + 2 * GLA_HEADS * GLA_DV

MLA_HEADS = 8
MLA_NOPE = 128
MLA_ROPE = 64
MLA_QK = MLA_NOPE + MLA_ROPE
MLA_V = 128
MLA_Q_RANK = 384
MLA_KV_RANK = 256
MLA_DOWN = MLA_Q_RANK + MLA_KV_RANK + MLA_ROPE
ROPE_THETA = 10000.0
ROPE_FREQS = MLA_ROPE // 4
Q_BLOCK = 128

D_FF = 2816
CONV_W = 3

kernel_name = 'hybrid_gla_mla_convffn_prefix_dit'


def rmsnorm(x, g):
    xf = x.astype(jnp.float32)
    y = xf * lax.rsqrt(jnp.mean(xf * xf, axis=-1, keepdims=True) + EPS)
    return (y * g.astype(jnp.float32)).astype(x.dtype)


def ada_modulation(cond, w, b):
    m = jax.nn.silu(cond) @ w + b
    return jnp.split(m, 6, axis=-1)


def modulate(x, g, shift, scale):
    return rmsnorm(x, g) * (1 + scale[..., None, :]) + shift[..., None, :]


def axial_rope_tables(length):
    rows = length // GRID_W
    t = jnp.arange(rows * GRID_W)
    row = (t // GRID_W).astype(jnp.float32)
    col = (t % GRID_W).astype(jnp.float32)
    inv = ROPE_THETA ** (-jnp.arange(ROPE_FREQS, dtype=jnp.float32) / ROPE_FREQS)
    ang = jnp.stack([row[:, None] * inv, col[:, None] * inv], axis=1)
    return jnp.cos(ang)[:, None], jnp.sin(ang)[:, None]


def apply_axial_rope(x, cos, sin):
    xr = x.astype(jnp.float32).reshape(*x.shape[:-1], 2, 2, ROPE_FREQS)
    x1, x2 = xr[..., 0, :], xr[..., 1, :]
    out = jnp.stack([x1 * cos - x2 * sin, x1 * sin + x2 * cos], axis=-2)
    return out.reshape(x.shape).astype(x.dtype)


def gla_project(h, w_in):
    B, L, _ = h.shape
    q, k, v, r = jnp.split(h @ w_in, [GLA_HEADS * GLA_DK, 2 * GLA_HEADS * GLA_DK,
                                      2 * GLA_HEADS * GLA_DK + GLA_HEADS * GLA_DV], axis=-1)
    q = q.reshape(B, L, GLA_HEADS, GLA_DK) * (GLA_DK ** -0.5)
    k = k.reshape(B, L, GLA_HEADS, GLA_DK)
    v = v.reshape(B, L, GLA_HEADS, GLA_DV)
    return q, k, v, r


def gla_log_decay(h, w1, w2, b):
    z = ((h @ w1) @ w2 + b).astype(jnp.float32)
    return (jax.nn.log_sigmoid(z) / GLA_GATE_NORMALIZER).reshape(*h.shape[:-1], GLA_HEADS, GLA_DK)


def gla_chunked(q, k, v, g, s0):
    B, L, H, _ = q.shape
    n = L // GLA_CHUNK
    def chunks(a):
        return a.reshape(B, n, GLA_CHUNK, H, a.shape[-1]).astype(jnp.float32)
    qc, kc, vc, gc = chunks(q), chunks(k), chunks(v), chunks(g)
    b = jnp.cumsum(gc, axis=2)
    b_last = b[:, :, -1:]
    q_dec = qc * jnp.exp(b)
    k_intra = kc * jnp.exp(-b)
    k_state = kc * jnp.exp(b_last - b)
    mask = jnp.tril(jnp.ones((GLA_CHUNK, GLA_CHUNK), dtype=bool))
    scores = jnp.where(mask, jnp.einsum('bnthd,bnshd->bnhts', q_dec, k_intra), 0.0)
    o_intra = jnp.einsum('bnhts,bnshv->bnthv', scores, vc)

    def step(S, xs):
        qd, ks, vv, bl = xs
        o = jnp.einsum('bthd,bhdv->bthv', qd, S)
        S = S * jnp.exp(bl)[..., None] + jnp.einsum('bthd,bthv->bhdv', ks, vv)
        return S, o

    xs = (jnp.moveaxis(q_dec, 1, 0), jnp.moveaxis(k_state, 1, 0),
          jnp.moveaxis(vc, 1, 0), jnp.moveaxis(b_last[:, :, 0], 1, 0))
    s_final, o_inter = lax.scan(step, s0.astype(jnp.float32), xs)
    o = o_intra + jnp.moveaxis(o_inter, 0, 1)
    return o.reshape(B, L, H, v.shape[-1]).astype(v.dtype), s_final


def gla_bidirectional(q, k, v, g_fwd, g_bwd, s_fwd0, s_bwd0):
    o_f, s_f = gla_chunked(q, k, v, g_fwd, s_fwd0)
    flip = lambda a: jnp.flip(a, axis=1)
    o_b, s_b = gla_chunked(flip(q), flip(k), flip(v), flip(g_bwd), s_bwd0)
    return o_f + flip(o_b), s_f, s_b


def gla_output(o, r, out_norm, w_out):
    B, L = o.shape[:2]
    o = rmsnorm(o, out_norm).reshape(B, L, GLA_HEADS * GLA_DV)
    return (o * jax.nn.silu(r)) @ w_out


def gla_mixer(h, hc, w_in, gate_w1, gate_w2, gate_b, out_norm, w_out, need_ctx_out):
    B = h.shape[0]
    s0 = jnp.zeros((B, GLA_HEADS, GLA_DK, GLA_DV), jnp.float32)
    qc, kc, vc, rc = gla_project(hc, w_in)
    gfc = gla_log_decay(hc, gate_w1[0], gate_w2[0], gate_b[0])
    gbc = gla_log_decay(hc, gate_w1[1], gate_w2[1], gate_b[1])
    oc, s_fwd, s_bwd = gla_bidirectional(qc, kc, vc, gfc, gbc, s0, s0)
    q, k, v, r = gla_project(h, w_in)
    gf = gla_log_decay(h, gate_w1[0], gate_w2[0], gate_b[0])
    gb = gla_log_decay(h, gate_w1[1], gate_w2[1], gate_b[1])
    o, _, _ = gla_bidirectional(q, k, v, gf, gb, s_fwd, s_bwd)
    out = gla_output(o, r, out_norm, w_out)
    out_c = gla_output(oc, rc, out_norm, w_out) if need_ctx_out else None
    return out, out_c


def split_norm(t, g):
    return jnp.concatenate([rmsnorm(t[..., :MLA_NOPE], g[:MLA_NOPE]),
                            rmsnorm(t[..., MLA_NOPE:], g[MLA_NOPE:])], axis=-1)


def mla_qkv(h, w_down, q_lora_norm, kv_lora_norm, w_uq, w_ukv, q_norm, k_norm, rope, want_q):
    B, L, _ = h.shape
    c_q, c_kv, k_pe = jnp.split(h @ w_down, [MLA_Q_RANK, MLA_Q_RANK + MLA_KV_RANK], axis=-1)
    kv = (rmsnorm(c_kv, kv_lora_norm) @ w_ukv).reshape(B, L, MLA_HEADS, MLA_NOPE + MLA_V)
    k_nope, v = kv[..., :MLA_NOPE], kv[..., MLA_NOPE:]
    k_nope = rmsnorm(k_nope, k_norm[:MLA_NOPE])
    k_pe = rmsnorm(k_pe, k_norm[MLA_NOPE:])[:, :, None, :]
    if rope is not None:
        k_pe = apply_axial_rope(k_pe, *rope)
    k = jnp.concatenate([k_nope, jnp.broadcast_to(k_pe, (B, L, MLA_HEADS, MLA_ROPE))], axis=-1)
    q = None
    if want_q:
        q = split_norm((rmsnorm(c_q, q_lora_norm) @ w_uq).reshape(B, L, MLA_HEADS, MLA_QK), q_norm)
        if rope is not None:
            q = jnp.concatenate([q[..., :MLA_NOPE], apply_axial_rope(q[..., MLA_NOPE:], *rope)], axis=-1)
    return q, k, v


def block_softmax_attention(q, k, v):
    B, L, H, Dh = q.shape
    nb = L // Q_BLOCK
    scale = Dh ** -0.5
    qb = jnp.moveaxis(q.reshape(B, nb, Q_BLOCK, H, Dh), 1, 0)
    def one(qblk):
        s = jnp.einsum('bqhd,bkhd->bhqk', qblk, k).astype(jnp.float32) * scale
        p = jax.nn.softmax(s, axis=-1).astype(v.dtype)
        return jnp.einsum('bhqk,bkhv->bqhv', p, v)
    o = lax.map(one, qb)
    return jnp.moveaxis(o, 0, 1).reshape(B, L, H, v.shape[-1])


def mla_mixer(h, hc, w_down, q_lora_norm, kv_lora_norm, w_uq, w_ukv, q_norm, k_norm, w_out, rope, need_ctx_out):
    B, L, _ = h.shape
    qc, kc, vc = mla_qkv(hc, w_down, q_lora_norm, kv_lora_norm, w_uq, w_ukv, q_norm, k_norm, None, need_ctx_out)
    q, k, v = mla_qkv(h, w_down, q_lora_norm, kv_lora_norm, w_uq, w_ukv, q_norm, k_norm, rope, True)
    o = block_softmax_attention(q, jnp.concatenate([kc, k], axis=1), jnp.concatenate([vc, v], axis=1))
    out = o.reshape(B, L, MLA_HEADS * MLA_V) @ w_out
    out_c = None
    if need_ctx_out:
        oc = block_softmax_attention(qc, kc, vc)
        out_c = oc.reshape(B, hc.shape[1], MLA_HEADS * MLA_V) @ w_out
    return out, out_c


def depthwise_conv3(u, w, b):
    up = jnp.pad(u, ((0, 0), (1, 1), (0, 0)))
    return up[:, :-2] * w[0] + up[:, 1:-1] * w[1] + up[:, 2:] * w[2] + b


def conv_ffn(h, w_up, conv_w, conv_b, w_down):
    u = depthwise_conv3(h @ w_up, conv_w, conv_b)
    a, g = jnp.split(u, 2, axis=-1)
    return (jax.nn.silu(g) * a) @ w_down


def setup_inputs(seed: int = 0) -> dict:
    key = jax.random.key(seed)
    ks = jax.random.split(key, 32)
    nrm = lambda k, shape, s: jax.random.normal(k, shape, jnp.float32) * s
    gain = lambda k, shape: 1.0 + 0.05 * jax.random.normal(k, shape, jnp.float32)
    return {
        'x': nrm(ks[0], (BATCH, SEQ, D_MODEL), 1.0),
        'c': nrm(ks[1], (BATCH, D_MODEL), 1.0),
        'ctx': nrm(ks[2], (BATCH, CTX_LEN, D_MODEL), 1.0),
        'c_ctx': nrm(ks[3], (D_MODEL,), 1.0),
        'w_ada': nrm(ks[4], (DEPTH, D_MODEL, 6 * D_MODEL), 0.5 * D_MODEL ** -0.5),
        'b_ada': nrm(ks[5], (DEPTH, 6 * D_MODEL), 0.02),
        'norm_mix': gain(ks[6], (DEPTH, D_MODEL)),
        'norm_ffn': gain(ks[7], (DEPTH, D_MODEL)),
        'gla_w_in': nrm(ks[8], (N_GLA_LAYERS, D_MODEL, GLA_IN), D_MODEL ** -0.5),
        'gla_gate_w1': nrm(ks[9], (N_GLA_LAYERS, 2, D_MODEL, GLA_GATE_RANK), D_MODEL ** -0.5),
        'gla_gate_w2': nrm(ks[10], (N_GLA_LAYERS, 2, GLA_GATE_RANK, GLA_HEADS * GLA_DK), GLA_GATE_RANK ** -0.5),
        'gla_gate_b': nrm(ks[11], (N_GLA_LAYERS, 2, GLA_HEADS * GLA_DK), 0.1),
        'gla_out_norm': gain(ks[12], (N_GLA_LAYERS, GLA_DV)),
        'gla_w_out': nrm(ks[13], (N_GLA_LAYERS, GLA_HEADS * GLA_DV, D_MODEL), (GLA_HEADS * GLA_DV) ** -0.5),
        'mla_w_down': nrm(ks[14], (N_MLA_LAYERS, D_MODEL, MLA_DOWN), D_MODEL ** -0.5),
        'mla_q_lora_norm': gain(ks[15], (N_MLA_LAYERS, MLA_Q_RANK)),
        'mla_kv_lora_norm': gain(ks[16], (N_MLA_LAYERS, MLA_KV_RANK)),
        'mla_w_uq': nrm(ks[17], (N_MLA_LAYERS, MLA_Q_RANK, MLA_HEADS * MLA_QK), MLA_Q_RANK ** -0.5),
        'mla_w_ukv': nrm(ks[18], (N_MLA_LAYERS, MLA_KV_RANK, MLA_HEADS * (MLA_NOPE + MLA_V)), MLA_KV_RANK ** -0.5),
        'mla_q_norm': gain(ks[19], (N_MLA_LAYERS, MLA_QK)),
        'mla_k_norm': gain(ks[20], (N_MLA_LAYERS, MLA_QK)),
        'mla_w_out': nrm(ks[21], (N_MLA_LAYERS, MLA_HEADS * MLA_V, D_MODEL), (MLA_HEADS * MLA_V) ** -0.5),
        'ffn_w_up': nrm(ks[22], (DEPTH, D_MODEL, 2 * D_FF), D_MODEL ** -0.5),
        'ffn_conv_w': nrm(ks[23], (DEPTH, CONV_W, 2 * D_FF), CONV_W ** -0.5),
        'ffn_conv_b': nrm(ks[24], (DEPTH, 2 * D_FF), 0.02),
        'ffn_w_down': nrm(ks[25], (DEPTH, D_FF, D_MODEL), D_FF ** -0.5),
    }


def reference(x, c, ctx, c_ctx, w_ada, b_ada, norm_mix, norm_ffn,
              gla_w_in, gla_gate_w1, gla_gate_w2, gla_gate_b, gla_out_norm, gla_w_out,
              mla_w_down, mla_q_lora_norm, mla_kv_lora_norm, mla_w_uq, mla_w_ukv,
              mla_q_norm, mla_k_norm, mla_w_out,
              ffn_w_up, ffn_conv_w, ffn_conv_b, ffn_w_down):
    rope = axial_rope_tables(x.shape[1])
    xc = ctx
    for i in range(DEPTH):
        last = i == DEPTH - 1
        j = i // N_MIXERS
        sh1, sc1, g1, sh2, sc2, g2 = ada_modulation(c, w_ada[i], b_ada[i])
        csh1, csc1, cg1, csh2, csc2, cg2 = ada_modulation(c_ctx, w_ada[i], b_ada[i])
        h = modulate(x, norm_mix[i], sh1, sc1)
        hc = modulate(xc, norm_mix[i], csh1, csc1)
        if i % N_MIXERS == 0:
            o, oc = gla_mixer(h, hc, gla_w_in[j], gla_gate_w1[j], gla_gate_w2[j], gla_gate_b[j],
                              gla_out_norm[j], gla_w_out[j], not last)
        else:
            o, oc = mla_mixer(h, hc, mla_w_down[j], mla_q_lora_norm[j], mla_kv_lora_norm[j],
                              mla_w_uq[j], mla_w_ukv[j], mla_q_norm[j], mla_k_norm[j], mla_w_out[j],
                              rope, not last)
        x = x + g1[:, None, :] * o
        x = x + g2[:, None, :] * conv_ffn(modulate(x, norm_ffn[i], sh2, sc2),
                                          ffn_w_up[i], ffn_conv_w[i], ffn_conv_b[i], ffn_w_down[i])
        if not last:
            xc = xc + cg1 * oc
            xc = xc + cg2 * conv_ffn(modulate(xc, norm_ffn[i], csh2, csc2),
                                     ffn_w_up[i], ffn_conv_w[i], ffn_conv_b[i], ffn_w_down[i])
    return x
```

```python
import functools

import jax
import jax.numpy as jnp
from jax import lax
from jax.experimental import pallas as pl
from jax.experimental.pallas import tpu as pltpu

F32 = jnp.float32
BF16 = jnp.bfloat16

EPS = 1e-6
GRID_W = 64
ROPE_THETA = 10000.0

GLA_HEADS = 4
GLA_GATE_NORMALIZER = 16.0
GLA_CHUNK = 128
GLA_RANK_PAD = 128

MLA_HEADS = 8
MLA_NOPE = 128
MLA_ROPE = 64
MLA_V = 128
MLA_QPAD = 256

CONV_HALO = 16
VMEM_LIMIT = 56 * 1024 * 1024


def _cparams(*sem):
    return pltpu.CompilerParams(dimension_semantics=sem, vmem_limit_bytes=VMEM_LIMIT)


def _const_spec(shape):
    nd = len(shape)
    return pl.BlockSpec(shape, lambda *_: (0,) * nd, pipeline_mode=pl.Buffered(1))


def _dot(a, b):
    return jnp.dot(a, b, preferred_element_type=F32)


def _dot_nt(a, b):
    return lax.dot_general(a, b, (((1,), (1,)), ((), ())), preferred_element_type=F32)


def _dot_tn(a, b):
    return lax.dot_general(a, b, (((0,), (0,)), ((), ())), preferred_element_type=F32)


def _sigmoid(x):
    return 1.0 / (1.0 + jnp.exp(-x))


def _rms_scale(x, width):
    ss = jnp.sum(x * x, axis=-1, keepdims=True)
    return lax.rsqrt(ss * (1.0 / width) + EPS)


def _modulate(x, g, shift, scale):
    return x * _rms_scale(x, x.shape[-1]) * (g * (1.0 + scale)) + shift


def _ada_kernel(cond_ref, w_ref, b_ref, o_ref):
    cond = cond_ref[...]
    s = cond * _sigmoid(cond)
    o_ref[0] = jnp.dot(s, w_ref[0], preferred_element_type=F32,
                       precision=lax.Precision.HIGHEST) + b_ref[0]


def _ada_all(cond, w_ada, b_ada):
    depth, d, d6 = w_ada.shape
    rows = cond.shape[0]
    return pl.pallas_call(
        _ada_kernel,
        grid=(depth, d6 // d),
        in_specs=[
            pl.BlockSpec((rows, d), lambda i, j: (0, 0)),
            pl.BlockSpec((1, d, d), lambda i, j: (i, 0, j)),
            pl.BlockSpec((1, 1, d), lambda i, j: (i, 0, j)),
        ],
        out_specs=pl.BlockSpec((1, rows, d), lambda i, j: (i, 0, j)),
        out_shape=jax.ShapeDtypeStruct((depth, rows, d6), F32),
        compiler_params=_cparams("parallel", "parallel"),
        name="ada_mod",
    )(cond, w_ada, b_ada.reshape(depth, 1, d6))


def _gla_proj_kernel(x_ref, mod_ref, g_ref, win_ref, w1_ref, w2_ref, gb_ref,
                     q_ref, k_ref, v_ref, r_ref, gate_ref, *, dqk):
    mod = mod_ref[0]
    h = _modulate(x_ref[0], g_ref[...], mod[0:1], mod[1:2]).astype(BF16)
    p = _dot(h, win_ref[...])
    dk = dqk // GLA_HEADS
    q_ref[0] = (p[:, :dqk] * (dk ** -0.5)).astype(BF16)
    k_ref[0] = p[:, dqk:2 * dqk].astype(BF16)
    dv = (p.shape[1] - 2 * dqk) // 2
    v_ref[0] = p[:, 2 * dqk:2 * dqk + dv].astype(BF16)
    r_ref[0] = p[:, 2 * dqk + dv:].astype(BF16)
    low = _dot(h, w1_ref[...]).astype(BF16)
    z = _dot(low, w2_ref[...]) + gb_ref[...]
    log_sig = jnp.minimum(z, 0.0) - jnp.log(1.0 + jnp.exp(-jnp.abs(z)))
    gate_ref[0] = log_sig * (1.0 / GLA_GATE_NORMALIZER)


def _gla_proj(x, mod, norm_g, w_in, w1, w2, gb, *, tm):
    b, l, d = x.shape
    dqk = w2.shape[1] // 2
    dv = (w_in.shape[1] - 2 * dqk) // 2
    row = lambda i, j: (i, j, 0)
    return pl.pallas_call(
        functools.partial(_gla_proj_kernel, dqk=dqk),
        grid=(b, l // tm),
        in_specs=[
            pl.BlockSpec((1, tm, d), row),
            pl.BlockSpec((1, 6, d), lambda i, j: (i, 0, 0)),
            _const_spec(norm_g.shape),
            _const_spec(w_in.shape),
            _const_spec(w1.shape),
            _const_spec(w2.shape),
            _const_spec(gb.shape),
        ],
        out_specs=[
            pl.BlockSpec((1, tm, dqk), row),
            pl.BlockSpec((1, tm, dqk), row),
            pl.BlockSpec((1, tm, dv), row),
            pl.BlockSpec((1, tm, dv), row),
            pl.BlockSpec((1, tm, 2 * dqk), row),
        ],
        out_shape=[
            jax.ShapeDtypeStruct((b, l, dqk), BF16),
            jax.ShapeDtypeStruct((b, l, dqk), BF16),
            jax.ShapeDtypeStruct((b, l, dv), BF16),
            jax.ShapeDtypeStruct((b, l, dv), BF16),
            jax.ShapeDtypeStruct((b, l, 2 * dqk), F32),
        ],
        compiler_params=_cparams("parallel", "parallel"),
        name="gla_proj",
    )(x, mod, norm_g, w_in, w1, w2, gb)


def _gla_scan_kernel(*refs, reverse, want_o, fuse_out, tblk):
    if fuse_out:
        (q_ref, k_ref, v_ref, g_ref, s0_ref, ob_ref, r_ref, x_ref, mod_ref, gn_ref, wout_ref,
         xo_ref, st_ref, y_scr) = refs
    elif want_o:
        q_ref, k_ref, v_ref, g_ref, s0_ref, o_ref, st_ref = refs
    else:
        q_ref, k_ref, v_ref, g_ref, s0_ref, st_ref = refs

    @pl.when(pl.program_id(1) == 0)
    def _():
        st_ref[...] = s0_ref[...]

    c = GLA_CHUNK
    dk = q_ref.shape[2] // GLA_HEADS
    dv = v_ref.shape[2] // GLA_HEADS
    row = lax.broadcasted_iota(jnp.int32, (c, c), 0)
    col = lax.broadcasted_iota(jnp.int32, (c, c), 1)
    mask = (col >= row) if reverse else (col <= row)
    tri = mask.astype(BF16)
    i_mid = c // 2 if reverse else c // 2 - 1
    i_end = 0 if reverse else c - 1

    nch = tblk // c
    for ci in (reversed(range(nch)) if reverse else range(nch)):
        rows = slice(ci * c, (ci + 1) * c)
        g = g_ref[0, rows, :]
        g_hi = g.astype(BF16)
        rem = g - g_hi.astype(F32)
        g_mid = rem.astype(BF16)
        g_lo = (rem - g_mid.astype(F32)).astype(BF16)
        cum = _dot(tri, g_hi) + _dot(tri, g_mid) + _dot(tri, g_lo)
        c_mid = cum[i_mid:i_mid + 1, :]
        c_end = cum[i_end:i_end + 1, :]
        q = q_ref[0, rows, :].astype(F32)
        k = k_ref[0, rows, :].astype(F32)
        k_in = (k * jnp.exp(c_mid - cum)).astype(BF16)
        k_st = (k * jnp.exp(c_end - cum)).astype(BF16)
        decay_end = jnp.exp(c_end)
        if want_o:
            q_in = (q * jnp.exp(cum - c_mid)).astype(BF16)
            q_st = (q * jnp.exp(cum)).astype(BF16)
        for h in range(GLA_HEADS):
            ks = slice(h * dk, (h + 1) * dk)
            vs = slice(h * dv, (h + 1) * dv)
            v = v_ref[0, rows, vs]
            st = st_ref[0, h]
            if want_o:
                scores = jnp.where(mask, _dot_nt(q_in[:, ks], k_in[:, ks]), 0.0).astype(BF16)
                o = _dot(scores, v) + _dot_nt(q_st[:, ks], st.astype(BF16))
                if fuse_out:
                    o = o + ob_ref[0, rows, vs].astype(F32)
                    r = r_ref[0, rows, vs].astype(F32)
                    y = o * _rms_scale(o, dv) * gn_ref[...] * (r * _sigmoid(r))
                    y_scr[rows, vs] = y.astype(BF16)
                else:
                    o_ref[0, rows, vs] = o.astype(BF16)
            st_ref[0, h] = st * decay_end[:, ks] + _dot_tn(v, k_st[:, ks])

    if fuse_out:
        gate = mod_ref[0][2:3]
        xo_ref[0] = x_ref[0] + gate * _dot(y_scr[...], wout_ref[...])


def _gla_scan(q, k, v, gates, s0, *, direction, want_o=True, fused=None):
    b, l, dqk = q.shape
    dvt = v.shape[2]
    tblk = min(l, 512)
    nblk = l // tblk
    reverse = direction == 1
    blk = (lambda i, j: (i, nblk - 1 - j, 0)) if reverse else (lambda i, j: (i, j, 0))
    gblk = (lambda i, j: (i, nblk - 1 - j, 1)) if reverse else (lambda i, j: (i, j, 0))
    st_spec = pl.BlockSpec((1,) + s0.shape[1:], lambda i, j: (i, 0, 0, 0))
    in_specs = [
        pl.BlockSpec((1, tblk, dqk), blk),
        pl.BlockSpec((1, tblk, dqk), blk),
        pl.BlockSpec((1, tblk, dvt), blk),
        pl.BlockSpec((1, tblk, dqk), gblk),
        st_spec,
    ]
    args = [q, k, v, gates, s0]
    st_shape = jax.ShapeDtypeStruct(s0.shape, F32)
    scratch = []
    if fused is not None:
        o_other, r, x, mod, gn, w_out = fused
        d = x.shape[2]
        in_specs += [
            pl.BlockSpec((1, tblk, dvt), blk),
            pl.BlockSpec((1, tblk, dvt), blk),
            pl.BlockSpec((1, tblk, d), blk),
            pl.BlockSpec((1, 6, d), lambda i, j: (i, 0, 0)),
            _const_spec(gn.shape),
            _const_spec(w_out.shape),
        ]
        args += [o_other, r, x, mod, gn, w_out]
        out_specs = [pl.BlockSpec((1, tblk, d), blk), st_spec]
        out_shape = [jax.ShapeDtypeStruct(x.shape, F32), st_shape]
        scratch = [pltpu.VMEM((tblk, dvt), BF16)]
    elif want_o:
        out_specs = [pl.BlockSpec((1, tblk, dvt), blk), st_spec]
        out_shape = [jax.ShapeDtypeStruct((b, l, dvt), BF16), st_shape]
    else:
        out_specs = [st_spec]
        out_shape = [st_shape]
    outs = pl.pallas_call(
        functools.partial(_gla_scan_kernel, reverse=reverse, want_o=want_o,
                          fuse_out=fused is not None, tblk=tblk),
        grid=(b, nblk),
        in_specs=in_specs,
        out_specs=out_specs,
        out_shape=out_shape,
        scratch_shapes=scratch,
        compiler_params=_cparams("parallel", "arbitrary"),
        name="gla_scan_%s%s" % ("bwd" if reverse else "fwd", "_out" if fused is not None else ""),
    )(*args)
    if want_o:
        return outs[0], outs[1]
    return None, outs[0]


def _mla_proj_kernel(*refs, want_q):
    (x_ref, mod_ref, g_ref, cos_ref, sin_ref, wdq_ref, wdkv_ref, wdpe_ref, qln_ref, kvln_ref,
     wuq_ref, wuqs_ref, wukv_ref, qn_ref, kn_ref) = refs[:15]
    if want_q:
        q_ref, k_ref, v_ref = refs[15:]
    else:
        k_ref, v_ref = refs[15:]
    mod = mod_ref[0]
    h = _modulate(x_ref[0], g_ref[...], mod[0:1], mod[1:2]).astype(BF16)
    cos = cos_ref[...]
    sin = sin_ref[...]
    qn = qn_ref[...]
    kn = kn_ref[...]

    def rope_part(raw, raw_partner, gains):
        return (raw * (cos * gains[1:2]) + raw_partner * (sin * gains[2:3])) * _rms_scale(raw, MLA_ROPE)

    c_kv = _dot(h, wdkv_ref[...])
    c_kv = (c_kv * _rms_scale(c_kv, c_kv.shape[-1]) * kvln_ref[...]).astype(BF16)
    kv = _dot(c_kv, wukv_ref[...])
    kpe = _dot(h, wdpe_ref[...])
    k_rope = rope_part(kpe[:, :128], kpe[:, 128:], kn).astype(BF16)
    if want_q:
        c_q = _dot(h, wdq_ref[...])
        c_q = (c_q * _rms_scale(c_q, c_q.shape[-1]) * qln_ref[...]).astype(BF16)
        qm = _dot(c_q, wuq_ref[...])
        qs = _dot(c_q, wuqs_ref[...])
        sm_scale = (MLA_NOPE + MLA_ROPE) ** -0.5
    for hd in range(MLA_HEADS):
        kv_base = hd * (MLA_NOPE + MLA_V)
        k_nope = kv[:, kv_base:kv_base + MLA_NOPE]
        k_ref[0, hd, :, :MLA_NOPE] = (k_nope * _rms_scale(k_nope, MLA_NOPE) * kn[0:1]).astype(BF16)
        k_ref[0, hd, :, MLA_NOPE:] = k_rope
        v_ref[0, hd] = kv[:, kv_base + MLA_NOPE:kv_base + MLA_NOPE + MLA_V].astype(BF16)
        if want_q:
            base = hd * MLA_QPAD
            q_nope = qm[:, base:base + MLA_NOPE]
            q_nope = q_nope * _rms_scale(q_nope, MLA_NOPE) * (qn[0:1] * sm_scale)
            q_ref[0, hd, :, :MLA_NOPE] = q_nope.astype(BF16)
            q_rope = rope_part(qm[:, base + MLA_NOPE:base + MLA_QPAD],
                               qs[:, hd * 128:(hd + 1) * 128], qn) * sm_scale
            q_ref[0, hd, :, MLA_NOPE:] = q_rope.astype(BF16)


def _mla_proj(x, mod, norm_g, cos, sin, w, *, want_q, tm):
    b, l, d = x.shape
    consts = [w["wdq"], w["wdkv"], w["wdpe"], w["qln"], w["kvln"], w["wuq"], w["wuqs"], w["wukv"],
              w["qn"], w["kn"]]
    hspec = lambda width: pl.BlockSpec((1, MLA_HEADS, tm, width), lambda i, j: (i, 0, j, 0))
    hshape = lambda width: jax.ShapeDtypeStruct((b, MLA_HEADS, l, width), BF16)
    out_specs = [hspec(MLA_QPAD), hspec(MLA_V)]
    out_shape = [hshape(MLA_QPAD), hshape(MLA_V)]
    if want_q:
        out_specs = [hspec(MLA_QPAD)] + out_specs
        out_shape = [hshape(MLA_QPAD)] + out_shape
    return pl.pallas_call(
        functools.partial(_mla_proj_kernel, want_q=want_q),
        grid=(b, l // tm),
        in_specs=[
            pl.BlockSpec((1, tm, d), lambda i, j: (i, j, 0)),
            pl.BlockSpec((1, 6, d), lambda i, j: (i, 0, 0)),
            _const_spec(norm_g.shape),
            pl.BlockSpec((tm, 128), lambda i, j: (j, 0)),
            pl.BlockSpec((tm, 128), lambda i, j: (j, 0)),
        ] + [_const_spec(a.shape) for a in consts],
        out_specs=out_specs,
        out_shape=out_shape,
        compiler_params=_cparams("parallel", "parallel"),
        name="mla_proj_q" if want_q else "mla_proj_kv",
    )(x, mod, norm_g, cos, sin, *consts)


def _attn_kernel(*refs, with_latent):
    if with_latent:
        q_ref, kc_ref, vc_ref, k_ref, v_ref, o_ref = refs
    else:
        q_ref, kc_ref, vc_ref, o_ref = refs
    q = q_ref[0, 0]
    s_c = _dot_nt(q, kc_ref[0, 0])
    m = jnp.max(s_c, axis=-1, keepdims=True)
    if with_latent:
        s_l = _dot_nt(q, k_ref[0, 0])
        m = jnp.maximum(m, jnp.max(s_l, axis=-1, keepdims=True))
    p_c = jnp.exp(s_c - m)
    denom = jnp.sum(p_c, axis=-1, keepdims=True)
    o = _dot(p_c.astype(BF16), vc_ref[0, 0])
    if with_latent:
        p_l = jnp.exp(s_l - m)
        denom = denom + jnp.sum(p_l, axis=-1, keepdims=True)
        o = o + _dot(p_l.astype(BF16), v_ref[0, 0])
    o_ref[0] = (o / denom).astype(BF16)


def _attention(q, kc, vc, k=None, v=None, *, tq):
    b, nh, lq, dq = q.shape
    lc = kc.shape[2]
    dv = vc.shape[3]
    with_latent = k is not None
    head = lambda i, h, j: (i, h, 0, 0)
    in_specs = [
        pl.BlockSpec((1, 1, tq, dq), lambda i, h, j: (i, h, j, 0)),
        pl.BlockSpec((1, 1, lc, dq), head),
        pl.BlockSpec((1, 1, lc, dv), head),
    ]
    args = [q, kc, vc]
    if with_latent:
        lk = k.shape[2]
        in_specs += [pl.BlockSpec((1, 1, lk, dq), head), pl.BlockSpec((1, 1, lk, dv), head)]
        args += [k, v]
    return pl.pallas_call(
        functools.partial(_attn_kernel, with_latent=with_latent),
        grid=(b, nh, lq // tq),
        in_specs=in_specs,
        out_specs=pl.BlockSpec((1, tq, dv), lambda i, h, j: (i, j, h)),
        out_shape=jax.ShapeDtypeStruct((b, lq, nh * dv), BF16),
        compiler_params=_cparams("parallel", "parallel", "parallel"),
        name="mla_attn" if with_latent else "mla_attn_ctx",
    )(*args)


def _out_proj_kernel(o_ref, x_ref, mod_ref, w_ref, xo_ref):
    gate = mod_ref[0][2:3]
    xo_ref[0] = x_ref[0] + gate * _dot(o_ref[0], w_ref[...])


def _out_proj(o, x, mod, w_out, *, tm):
    b, l, d = x.shape
    row = lambda i, j: (i, j, 0)
    return pl.pallas_call(
        _out_proj_kernel,
        grid=(b, l // tm),
        in_specs=[
            pl.BlockSpec((1, tm, o.shape[2]), row),
            pl.BlockSpec((1, tm, d), row),
            pl.BlockSpec((1, 6, d), lambda i, j: (i, 0, 0)),
            _const_spec(w_out.shape),
        ],
        out_specs=pl.BlockSpec((1, tm, d), row),
        out_shape=jax.ShapeDtypeStruct(x.shape, F32),
        compiler_params=_cparams("parallel", "parallel"),
        name="mla_out",
    )(o, x, mod, w_out)


def _ffn_kernel(xm_ref, xp_ref, xn_ref, mod_ref, g_ref, wa_ref, wg_ref, ca_ref, cg_ref, wd_ref,
                xo_ref, h_scr, ua_scr, ug_scr, *, tm, nchunk):
    i = pl.program_id(1)
    mod = mod_ref[0]
    g = g_ref[...]
    hal = CONV_HALO

    def hidden(x):
        return _modulate(x, g, mod[3:4], mod[4:5])

    h_scr[0:hal] = jnp.where(i == 0, 0.0, hidden(xp_ref[0])).astype(BF16)
    h_scr[hal:hal + tm] = hidden(xm_ref[0]).astype(BF16)
    h_scr[hal + tm:] = jnp.where(i == pl.num_programs(1) - 1, 0.0, hidden(xn_ref[0])).astype(BF16)

    fc = wa_ref.shape[1] // nchunk

    def conv(u_scr, cw):
        return (u_scr[hal - 1:hal - 1 + tm] * cw[0:1] + u_scr[hal:hal + tm] * cw[1:2]
                + u_scr[hal + 1:hal + 1 + tm] * cw[2:3] + cw[3:4])

    acc = None
    for ci in range(nchunk):
        cols = slice(ci * fc, (ci + 1) * fc)
        ua_scr[...] = _dot(h_scr[...], wa_ref[:, cols])
        ug_scr[...] = _dot(h_scr[...], wg_ref[:, cols])
        a = conv(ua_scr, ca_ref[:, cols])
        gt = conv(ug_scr, cg_ref[:, cols])
        act = (gt * _sigmoid(gt) * a).astype(BF16)
        y = _dot(act, wd_ref[cols, :])
        acc = y if acc is None else acc + y
    xo_ref[0] = xm_ref[0] + mod[5:6] * acc


def _conv_ffn(x, mod, norm_g, wa, wg, ca, cg, wd, *, tm, nchunk):
    b, l, d = x.shape
    hal = CONV_HALO
    nt = l // tm
    per = tm // hal
    last = l // hal - 1
    fc = wa.shape[1] // nchunk
    return pl.pallas_call(
        functools.partial(_ffn_kernel, tm=tm, nchunk=nchunk),
        grid=(b, nt),
        in_specs=[
            pl.BlockSpec((1, tm, d), lambda i, j: (i, j, 0)),
            pl.BlockSpec((1, hal, d), lambda i, j: (i, jnp.maximum(j * per - 1, 0), 0)),
            pl.BlockSpec((1, hal, d), lambda i, j: (i, jnp.minimum((j + 1) * per, last), 0)),
            pl.BlockSpec((1, 6, d), lambda i, j: (i, 0, 0)),
            _const_spec(norm_g.shape),
            _const_spec(wa.shape),
            _const_spec(wg.shape),
            _const_spec(ca.shape),
            _const_spec(cg.shape),
            _const_spec(wd.shape),
        ],
        out_specs=pl.BlockSpec((1, tm, d), lambda i, j: (i, j, 0)),
        out_shape=jax.ShapeDtypeStruct(x.shape, F32),
        scratch_shapes=[
            pltpu.VMEM((tm + 2 * hal, d), BF16),
            pltpu.VMEM((tm + 2 * hal, fc), F32),
            pltpu.VMEM((tm + 2 * hal, fc), F32),
        ],
        compiler_params=_cparams("parallel", "parallel"),
        name="conv_ffn",
    )(x, x, x, mod, norm_g, wa, wg, ca, cg, wd)


def _rope_tables(length):
    nf = MLA_ROPE // 4
    t = jnp.arange(length)
    pos = jnp.stack([(t // GRID_W).astype(F32), (t % GRID_W).astype(F32)], axis=1)
    inv = ROPE_THETA ** (-jnp.arange(nf, dtype=F32) / nf)
    ang = pos[:, :, None] * inv
    cos = jnp.cos(ang)[:, :, None, :] * jnp.ones((1, 1, 2, 1), F32)
    sin = jnp.sin(ang)[:, :, None, :] * jnp.array([-1.0, 1.0], F32)[None, None, :, None]
    pad = lambda a: jnp.pad(a.reshape(length, MLA_ROPE), ((0, 0), (0, 128 - MLA_ROPE)))
    return pad(cos), pad(sin)


def _rope_partner(a):
    nf = MLA_ROPE // 4
    r = a.reshape(a.shape[:-1] + (2, 2, nf))
    return jnp.flip(r, axis=-2).reshape(a.shape)


def _pad_last(a, width):
    return jnp.pad(a, [(0, 0)] * (a.ndim - 1) + [(0, width - a.shape[-1])])


def _mla_weights(w_down, q_lora_norm, kv_lora_norm, w_uq, w_ukv, q_norm, k_norm):
    q_rank = q_lora_norm.shape[0]
    kv_rank = kv_lora_norm.shape[0]
    qk = MLA_NOPE + MLA_ROPE
    w_pe = w_down[:, q_rank + kv_rank:]
    wuq = w_uq.reshape(q_rank, MLA_HEADS, qk)
    wuq_rope = wuq[..., MLA_NOPE:]

    def gains(g):
        rope = g[MLA_NOPE:]
        return jnp.stack([g[:MLA_NOPE], _pad_last(rope, 128), _pad_last(_rope_partner(rope), 128)])

    return {
        "wdq": w_down[:, :q_rank].astype(BF16),
        "wdkv": w_down[:, q_rank:q_rank + kv_rank].astype(BF16),
        "wdpe": jnp.concatenate([_pad_last(w_pe, 128), _pad_last(_rope_partner(w_pe), 128)],
                                axis=1).astype(BF16),
        "qln": q_lora_norm[None, :],
        "kvln": kv_lora_norm[None, :],
        "wuq": _pad_last(wuq, MLA_QPAD).reshape(q_rank, MLA_HEADS * MLA_QPAD).astype(BF16),
        "wuqs": _pad_last(_rope_partner(wuq_rope), 128).reshape(q_rank, MLA_HEADS * 128).astype(BF16),
        "wukv": w_ukv.astype(BF16),
        "qn": gains(q_norm),
        "kn": gains(k_norm),
    }


def kernel(x, c, ctx, c_ctx, w_ada, b_ada, norm_mix, norm_ffn, gla_w_in, gla_gate_w1, gla_gate_w2, gla_gate_b, gla_out_norm, gla_w_out, mla_w_down, mla_q_lora_norm, mla_kv_lora_norm, mla_w_uq, mla_w_ukv, mla_q_norm, mla_k_norm, mla_w_out, ffn_w_up, ffn_conv_w, ffn_conv_b, ffn_w_down):
    bsz, seq, d = x.shape
    lc = ctx.shape[1]
    depth = w_ada.shape[0]
    d_ff = ffn_w_down.shape[1]
    dqk = gla_gate_w2.shape[-1]
    rank = gla_gate_w1.shape[-1]
    dv = gla_out_norm.shape[-1]

    cond_rows = -(-(bsz + 1) // 8) * 8
    cond = jnp.zeros((cond_rows, d), F32).at[:bsz].set(c).at[bsz].set(c_ctx)
    mods = _ada_all(cond, w_ada, b_ada)

    cos_l, sin_l = _rope_tables(seq)
    cos_c = _pad_last(jnp.ones((lc, MLA_ROPE), F32), 128)
    sin_c = jnp.zeros((lc, 128), F32)

    tm_l = min(seq, 512)
    tm_c = min(lc, 512)
    xc = ctx
    for i in range(depth):
        last = i == depth - 1
        j = i // 2
        mod_l = mods[i, :bsz].reshape(bsz, 6, d)
        mod_c = jnp.broadcast_to(mods[i, bsz].reshape(1, 6, d), (bsz, 6, d))
        g_mix = norm_mix[i][None, :]
        if i % 2 == 0:
            w_in = gla_w_in[j].astype(BF16)
            w1 = _pad_last(jnp.concatenate([gla_gate_w1[j, 0], gla_gate_w1[j, 1]], axis=1),
                           GLA_RANK_PAD).astype(BF16)
            w2 = jnp.zeros((GLA_RANK_PAD, 2 * dqk), F32)
            w2 = w2.at[:rank, :dqk].set(gla_gate_w2[j, 0]).at[rank:2 * rank, dqk:].set(gla_gate_w2[j, 1])
            w2 = w2.astype(BF16)
            gb = gla_gate_b[j].reshape(1, 2 * dqk)
            gn = gla_out_norm[j][None, :]
            w_out = gla_w_out[j].astype(BF16)
            s0 = jnp.zeros((bsz, GLA_HEADS, dv, dqk // GLA_HEADS), F32)

            qc, kc, vc, rc, gc = _gla_proj(xc, mod_c, g_mix, w_in, w1, w2, gb, tm=tm_c)
            q, k, v, r, g = _gla_proj(x, mod_l, g_mix, w_in, w1, w2, gb, tm=tm_l)
            if last:
                _, s_fwd = _gla_scan(qc, kc, vc, gc, s0, direction=0, want_o=False)
                _, s_bwd = _gla_scan(qc, kc, vc, gc, s0, direction=1, want_o=False)
            else:
                oc_b, s_bwd = _gla_scan(qc, kc, vc, gc, s0, direction=1)
                xc, s_fwd = _gla_scan(qc, kc, vc, gc, s0, direction=0,
                                      fused=(oc_b, rc, xc, mod_c, gn, w_out))
            o_b, _ = _gla_scan(q, k, v, g, s_bwd, direction=1)
            x, _ = _gla_scan(q, k, v, g, s_fwd, direction=0, fused=(o_b, r, x, mod_l, gn, w_out))
        else:
            w = _mla_weights(mla_w_down[j], mla_q_lora_norm[j], mla_kv_lora_norm[j], mla_w_uq[j],
                             mla_w_ukv[j], mla_q_norm[j], mla_k_norm[j])
            w_out = mla_w_out[j].astype(BF16)
            if last:
                kc, vc = _mla_proj(xc, mod_c, g_mix, cos_c, sin_c, w, want_q=False, tm=tm_c)
            else:
                qc, kc, vc = _mla_proj(xc, mod_c, g_mix, cos_c, sin_c, w, want_q=True, tm=tm_c)
            q, k, v = _mla_proj(x, mod_l, g_mix, cos_l, sin_l, w, want_q=True, tm=tm_l)
            o = _attention(q, kc, vc, k, v, tq=256)
            x = _out_proj(o, x, mod_l, w_out, tm=tm_l)
            if not last:
                oc = _attention(qc, kc, vc, tq=min(lc, 256))
                xc = _out_proj(oc, xc, mod_c, w_out, tm=tm_c)

        g_ffn = norm_ffn[i][None, :]
        wa = ffn_w_up[i, :, :d_ff].astype(BF16)
        wg = ffn_w_up[i, :, d_ff:].astype(BF16)
        conv = jnp.concatenate([ffn_conv_w[i], ffn_conv_b[i][None, :]], axis=0)
        ca, cg = conv[:, :d_ff], conv[:, d_ff:]
        wd = ffn_w_down[i].astype(BF16)
        x = _conv_ffn(x, mod_l, g_ffn, wa, wg, ca, cg, wd, tm=tm_l, nchunk=2)
        if not last:
            xc = _conv_ffn(xc, mod_c, g_ffn, wa, wg, ca, cg, wd, tm=tm_c, nchunk=2)
    return x
```

```python
import functools

import jax
import jax.numpy as jnp
from jax import lax
from jax.experimental import pallas as pl
from jax.experimental.pallas import tpu as pltpu

F32 = jnp.float32
BF16 = jnp.bfloat16

LANES = 128
EPS = 1e-6
GRID_W = 64
ROPE_THETA = 10000.0
LOG2_E = 1.4426950408889634

GLA_HEADS = 4
GLA_GATE_NORMALIZER = 16.0
GLA_CHUNK = 128
GLA_RANK_PAD = 128

MLA_HEADS = 8
MLA_NOPE = 128
MLA_ROPE = 64
MLA_V = 128
MLA_QPAD = 256

FFN_COLS = 256
CONV_HALO = 16
VMEM_LIMIT = 56 * 1024 * 1024


def _cparams(*sem):
    return pltpu.CompilerParams(dimension_semantics=sem, vmem_limit_bytes=VMEM_LIMIT)


def _const_spec(shape):
    nd = len(shape)
    return pl.BlockSpec(shape, lambda *_: (0,) * nd, pipeline_mode=pl.Buffered(1))


def _dot(a, b):
    return jnp.dot(a, b, preferred_element_type=F32)


def _dot_nt(a, b):
    return lax.dot_general(a, b, (((1,), (1,)), ((), ())), preferred_element_type=F32)


def _dot_tn(a, b):
    return lax.dot_general(a, b, (((0,), (0,)), ((), ())), preferred_element_type=F32)


def _sigmoid(x):
    return 1.0 / (1.0 + jnp.exp(-x))


def _rms_scale(x, width):
    ss = jnp.sum(x * x, axis=-1, keepdims=True)
    return lax.rsqrt(ss * (1.0 / width) + EPS)


def _modulate(x, g, shift, scale):
    return x * _rms_scale(x, x.shape[-1]) * (g * (1.0 + scale)) + shift


def _ada_kernel(cond_ref, w_ref, b_ref, o_ref):
    cond = cond_ref[...]
    s = cond * _sigmoid(cond)
    o_ref[0] = jnp.dot(s, w_ref[0], preferred_element_type=F32,
                       precision=lax.Precision.HIGHEST) + b_ref[0]


def _ada_all(cond, w_ada, b_ada):
    depth, d, d6 = w_ada.shape
    rows = cond.shape[0]
    return pl.pallas_call(
        _ada_kernel,
        grid=(depth, d6 // d),
        in_specs=[
            pl.BlockSpec((rows, d), lambda i, j: (0, 0)),
            pl.BlockSpec((1, d, d), lambda i, j: (i, 0, j)),
            pl.BlockSpec((1, 1, d), lambda i, j: (i, 0, j)),
        ],
        out_specs=pl.BlockSpec((1, rows, d), lambda i, j: (i, 0, j)),
        out_shape=jax.ShapeDtypeStruct((depth, rows, d6), F32),
        compiler_params=_cparams("parallel", "parallel"),
        name="ada_mod",
    )(cond, w_ada, b_ada.reshape(depth, 1, d6))


def _gla_proj_kernel(x_ref, mod_ref, g_ref, win_ref, w1_ref, w2_ref, gb_ref,
                     q_ref, k_ref, v_ref, r_ref, gate_ref, *, dqk):
    mod = mod_ref[0]
    h = _modulate(x_ref[0], g_ref[...], mod[0:1], mod[1:2]).astype(BF16)
    p = _dot(h, win_ref[...])
    dk = dqk // GLA_HEADS
    q_ref[0] = (p[:, :dqk] * (dk ** -0.5)).astype(BF16)
    k_ref[0] = p[:, dqk:2 * dqk].astype(BF16)
    dv = (p.shape[1] - 2 * dqk) // 2
    v_ref[0] = p[:, 2 * dqk:2 * dqk + dv].astype(BF16)
    r_ref[0] = p[:, 2 * dqk + dv:].astype(BF16)
    low = _dot(h, w1_ref[...]).astype(BF16)
    z = _dot(low, w2_ref[...]) + gb_ref[...]
    log_sig = jnp.minimum(z, 0.0) - jnp.log(1.0 + jnp.exp(-jnp.abs(z)))
    gate_ref[0] = log_sig * (1.0 / GLA_GATE_NORMALIZER)


def _gla_proj(x, mod, norm_g, w_in, w1, w2, gb, *, tm):
    b, l, d = x.shape
    dqk = w2.shape[1] // 2
    dv = (w_in.shape[1] - 2 * dqk) // 2
    row = lambda i, j: (i, j, 0)
    return pl.pallas_call(
        functools.partial(_gla_proj_kernel, dqk=dqk),
        grid=(b, l // tm),
        in_specs=[
            pl.BlockSpec((1, tm, d), row),
            pl.BlockSpec((1, 6, d), lambda i, j: (i, 0, 0)),
            _const_spec(norm_g.shape),
            _const_spec(w_in.shape),
            _const_spec(w1.shape),
            _const_spec(w2.shape),
            _const_spec(gb.shape),
        ],
        out_specs=[
            pl.BlockSpec((1, tm, dqk), row),
            pl.BlockSpec((1, tm, dqk), row),
            pl.BlockSpec((1, tm, dv), row),
            pl.BlockSpec((1, tm, dv), row),
            pl.BlockSpec((1, tm, 2 * dqk), row),
        ],
        out_shape=[
            jax.ShapeDtypeStruct((b, l, dqk), BF16),
            jax.ShapeDtypeStruct((b, l, dqk), BF16),
            jax.ShapeDtypeStruct((b, l, dv), BF16),
            jax.ShapeDtypeStruct((b, l, dv), BF16),
            jax.ShapeDtypeStruct((b, l, 2 * dqk), F32),
        ],
        compiler_params=_cparams("parallel", "parallel"),
        name="gla_proj",
    )(x, mod, norm_g, w_in, w1, w2, gb)


def _gla_scan_kernel(*refs, reverse, want_o, fuse_out, tblk):
    if fuse_out:
        (q_ref, k_ref, v_ref, g_ref, s0_ref, ob_ref, r_ref, x_ref, mod_ref, gn_ref, wout_ref,
         xo_ref, st_ref, y_scr) = refs
    elif want_o:
        q_ref, k_ref, v_ref, g_ref, s0_ref, o_ref, st_ref = refs
    else:
        q_ref, k_ref, v_ref, g_ref, s0_ref, st_ref = refs

    @pl.when(pl.program_id(1) == 0)
    def _():
        st_ref[...] = s0_ref[...]

    c = GLA_CHUNK
    dk = q_ref.shape[2] // GLA_HEADS
    dv = v_ref.shape[2] // GLA_HEADS
    row = lax.broadcasted_iota(jnp.int32, (c, c), 0)
    col = lax.broadcasted_iota(jnp.int32, (c, c), 1)
    mask = (col >= row) if reverse else (col <= row)
    tri = mask.astype(BF16)
    i_mid = c // 2 if reverse else c // 2 - 1
    i_end = 0 if reverse else c - 1

    nch = tblk // c
    for ci in (reversed(range(nch)) if reverse else range(nch)):
        rows = slice(ci * c, (ci + 1) * c)
        g = g_ref[0, rows, :]
        g_hi = g.astype(BF16)
        rem = g - g_hi.astype(F32)
        g_mid = rem.astype(BF16)
        g_lo = (rem - g_mid.astype(F32)).astype(BF16)
        cum = _dot(tri, g_hi) + _dot(tri, g_mid) + _dot(tri, g_lo)
        c_mid = cum[i_mid:i_mid + 1, :]
        c_end = cum[i_end:i_end + 1, :]
        q = q_ref[0, rows, :].astype(F32)
        k = k_ref[0, rows, :].astype(F32)
        k_in = (k * jnp.exp(c_mid - cum)).astype(BF16)
        k_st = (k * jnp.exp(c_end - cum)).astype(BF16)
        decay_end = jnp.exp(c_end)
        if want_o:
            q_in = (q * jnp.exp(cum - c_mid)).astype(BF16)
            q_st = (q * jnp.exp(cum)).astype(BF16)
        for h in range(GLA_HEADS):
            ks = slice(h * dk, (h + 1) * dk)
            vs = slice(h * dv, (h + 1) * dv)
            v = v_ref[0, rows, vs]
            st = st_ref[0, h]
            if want_o:
                scores = jnp.where(mask, _dot_nt(q_in[:, ks], k_in[:, ks]), 0.0).astype(BF16)
                o = _dot(scores, v) + _dot_nt(q_st[:, ks], st.astype(BF16))
                if fuse_out:
                    o = o + ob_ref[0, rows, vs].astype(F32)
                    r = r_ref[0, rows, vs].astype(F32)
                    y = o * _rms_scale(o, dv) * gn_ref[...] * (r * _sigmoid(r))
                    y_scr[rows, vs] = y.astype(BF16)
                else:
                    o_ref[0, rows, vs] = o.astype(BF16)
            st_ref[0, h] = st * decay_end[:, ks] + _dot_tn(v, k_st[:, ks])

    if fuse_out:
        gate = mod_ref[0][2:3]
        xo_ref[0] = x_ref[0] + gate * _dot(y_scr[...], wout_ref[...])


def _gla_scan(q, k, v, gates, s0, *, direction, want_o=True, fused=None):
    b, l, dqk = q.shape
    dvt = v.shape[2]
    tblk = min(l, 512)
    nblk = l // tblk
    reverse = direction == 1
    blk = (lambda i, j: (i, nblk - 1 - j, 0)) if reverse else (lambda i, j: (i, j, 0))
    gblk = (lambda i, j: (i, nblk - 1 - j, 1)) if reverse else (lambda i, j: (i, j, 0))
    st_spec = pl.BlockSpec((1,) + s0.shape[1:], lambda i, j: (i, 0, 0, 0))
    in_specs = [
        pl.BlockSpec((1, tblk, dqk), blk),
        pl.BlockSpec((1, tblk, dqk), blk),
        pl.BlockSpec((1, tblk, dvt), blk),
        pl.BlockSpec((1, tblk, dqk), gblk),
        st_spec,
    ]
    args = [q, k, v, gates, s0]
    st_shape = jax.ShapeDtypeStruct(s0.shape, F32)
    scratch = []
    if fused is not None:
        o_other, r, x, mod, gn, w_out = fused
        d = x.shape[2]
        in_specs += [
            pl.BlockSpec((1, tblk, dvt), blk),
            pl.BlockSpec((1, tblk, dvt), blk),
            pl.BlockSpec((1, tblk, d), blk),
            pl.BlockSpec((1, 6, d), lambda i, j: (i, 0, 0)),
            _const_spec(gn.shape),
            _const_spec(w_out.shape),
        ]
        args += [o_other, r, x, mod, gn, w_out]
        out_specs = [pl.BlockSpec((1, tblk, d), blk), st_spec]
        out_shape = [jax.ShapeDtypeStruct(x.shape, F32), st_shape]
        scratch = [pltpu.VMEM((tblk, dvt), BF16)]
    elif want_o:
        out_specs = [pl.BlockSpec((1, tblk, dvt), blk), st_spec]
        out_shape = [jax.ShapeDtypeStruct((b, l, dvt), BF16), st_shape]
    else:
        out_specs = [st_spec]
        out_shape = [st_shape]
    outs = pl.pallas_call(
        functools.partial(_gla_scan_kernel, reverse=reverse, want_o=want_o,
                          fuse_out=fused is not None, tblk=tblk),
        grid=(b, nblk),
        in_specs=in_specs,
        out_specs=out_specs,
        out_shape=out_shape,
        scratch_shapes=scratch,
        compiler_params=_cparams("parallel", "arbitrary"),
        name="gla_scan_%s%s" % ("bwd" if reverse else "fwd", "_out" if fused is not None else ""),
    )(*args)
    if want_o:
        return outs[0], outs[1]
    return None, outs[0]


def _mla_proj_kernel(*refs, want_q):
    (x_ref, mod_ref, g_ref, cos_ref, sin_ref, wdq_ref, wdkv_ref, wdpe_ref, qln_ref, kvln_ref,
     wuq_ref, wuqs_ref, wuk_ref, wuvt_ref, qn_ref, kn_ref) = refs[:16]
    if want_q:
        q_ref, k_ref, vt_ref = refs[16:]
    else:
        k_ref, vt_ref = refs[16:]
    mod = mod_ref[0]
    h = _modulate(x_ref[0], g_ref[...], mod[0:1], mod[1:2]).astype(BF16)
    cos = cos_ref[...]
    sin = sin_ref[...]
    qn = qn_ref[...]
    kn = kn_ref[...]

    def rope_part(raw, raw_partner, gains):
        return (raw * (cos * gains[1:2]) + raw_partner * (sin * gains[2:3])) * _rms_scale(raw, MLA_ROPE)

    c_kv = _dot(h, wdkv_ref[...])
    c_kv = (c_kv * _rms_scale(c_kv, c_kv.shape[-1]) * kvln_ref[...]).astype(BF16)
    vt_ref[0] = _dot_nt(wuvt_ref[...], c_kv).astype(BF16)
    kn_all = _dot(c_kv, wuk_ref[...])
    kpe = _dot(h, wdpe_ref[...])
    k_rope = rope_part(kpe[:, :128], kpe[:, 128:], kn).astype(BF16)
    if want_q:
        c_q = _dot(h, wdq_ref[...])
        c_q = (c_q * _rms_scale(c_q, c_q.shape[-1]) * qln_ref[...]).astype(BF16)
        qm = _dot(c_q, wuq_ref[...])
        qs = _dot(c_q, wuqs_ref[...])
        q_scale = (MLA_NOPE + MLA_ROPE) ** -0.5 * LOG2_E
    for hd in range(MLA_HEADS):
        k_nope = kn_all[:, hd * MLA_NOPE:(hd + 1) * MLA_NOPE]
        k_ref[0, hd, :, :MLA_NOPE] = (k_nope * _rms_scale(k_nope, MLA_NOPE) * kn[0:1]).astype(BF16)
        k_ref[0, hd, :, MLA_NOPE:] = k_rope
        if want_q:
            base = hd * MLA_QPAD
            q_nope = qm[:, base:base + MLA_NOPE]
            q_nope = q_nope * _rms_scale(q_nope, MLA_NOPE) * (qn[0:1] * q_scale)
            q_ref[0, hd, :, :MLA_NOPE] = q_nope.astype(BF16)
            q_rope = rope_part(qm[:, base + MLA_NOPE:base + MLA_QPAD],
                               qs[:, hd * 128:(hd + 1) * 128], qn) * q_scale
            q_ref[0, hd, :, MLA_NOPE:] = q_rope.astype(BF16)


def _mla_proj(x, mod, norm_g, cos, sin, w, *, want_q, tm):
    b, l, d = x.shape
    consts = [w["wdq"], w["wdkv"], w["wdpe"], w["qln"], w["kvln"], w["wuq"], w["wuqs"], w["wuk"],
              w["wuvt"], w["qn"], w["kn"]]
    hspec = pl.BlockSpec((1, MLA_HEADS, tm, MLA_QPAD), lambda i, j: (i, 0, j, 0))
    hshape = jax.ShapeDtypeStruct((b, MLA_HEADS, l, MLA_QPAD), BF16)
    out_specs = [hspec, pl.BlockSpec((1, MLA_HEADS * MLA_V, tm), lambda i, j: (i, 0, j))]
    out_shape = [hshape, jax.ShapeDtypeStruct((b, MLA_HEADS * MLA_V, l), BF16)]
    if want_q:
        out_specs = [hspec] + out_specs
        out_shape = [hshape] + out_shape
    return pl.pallas_call(
        functools.partial(_mla_proj_kernel, want_q=want_q),
        grid=(b, l // tm),
        in_specs=[
            pl.BlockSpec((1, tm, d), lambda i, j: (i, j, 0)),
            pl.BlockSpec((1, 6, d), lambda i, j: (i, 0, 0)),
            _const_spec(norm_g.shape),
            pl.BlockSpec((tm, 128), lambda i, j: (j, 0)),
            pl.BlockSpec((tm, 128), lambda i, j: (j, 0)),
        ] + [_const_spec(a.shape) for a in consts],
        out_specs=out_specs,
        out_shape=out_shape,
        compiler_params=_cparams("parallel", "parallel"),
        name="mla_proj_q" if want_q else "mla_proj_kv",
    )(x, mod, norm_g, cos, sin, *consts)


def _attn_kernel(*refs, with_latent, tk):
    if with_latent:
        q_ref, kc_ref, vtc_ref, k_ref, vt_ref, ot_ref = refs
    else:
        q_ref, kc_ref, vtc_ref, ot_ref = refs
    q = q_ref[0, 0]
    chunks = [(kc_ref, vtc_ref, 0, kc_ref.shape[2])]
    if with_latent:
        chunks += [(k_ref, vt_ref, j * tk, tk) for j in range(k_ref.shape[2] // tk)]

    def scores(idx):
        kr, _, start, size = chunks[idx]
        return _dot_nt(kr[0, 0, start:start + size, :], q)

    s_next = scores(0)
    m = denom = acc = None
    for idx, (_, vr, start, size) in enumerate(chunks):
        s = s_next
        if idx + 1 < len(chunks):
            s_next = scores(idx + 1)
        vt = vr[0, :, start:start + size]
        cmax = jnp.max(s, axis=0, keepdims=True)
        if m is None:
            m = cmax
            p = jnp.exp2(s - m)
            denom = jnp.sum(p, axis=0, keepdims=True)
            acc = _dot(vt, p.astype(BF16))
        else:
            m_new = jnp.maximum(m, cmax)
            alpha = jnp.exp2(m - m_new)
            p = jnp.exp2(s - m_new)
            denom = denom * alpha + jnp.sum(p, axis=0, keepdims=True)
            acc = acc * alpha + _dot(vt, p.astype(BF16))
            m = m_new
    ot_ref[0] = (acc * (1.0 / denom)).astype(BF16)


def _attention(q, kc, vtc, k=None, vt=None, *, tq, tk=512):
    b, nh, lq, dq = q.shape
    lc = kc.shape[2]
    dv = vtc.shape[1] // nh
    with_latent = k is not None
    in_specs = [
        pl.BlockSpec((1, 1, tq, dq), lambda i, h, j: (i, h, j, 0)),
        pl.BlockSpec((1, 1, lc, dq), lambda i, h, j: (i, h, 0, 0)),
        pl.BlockSpec((1, dv, lc), lambda i, h, j: (i, h, 0)),
    ]
    args = [q, kc, vtc]
    if with_latent:
        lk = k.shape[2]
        in_specs += [pl.BlockSpec((1, 1, lk, dq), lambda i, h, j: (i, h, 0, 0)),
                     pl.BlockSpec((1, dv, lk), lambda i, h, j: (i, h, 0))]
        args += [k, vt]
    return pl.pallas_call(
        functools.partial(_attn_kernel, with_latent=with_latent, tk=tk),
        grid=(b, nh, lq // tq),
        in_specs=in_specs,
        out_specs=pl.BlockSpec((1, dv, tq), lambda i, h, j: (i, h, j)),
        out_shape=jax.ShapeDtypeStruct((b, nh * dv, lq), BF16),
        compiler_params=_cparams("parallel", "parallel", "parallel"),
        name="mla_attn" if with_latent else "mla_attn_ctx",
    )(*args)


def _out_proj_kernel(ot_ref, x_ref, mod_ref, w_ref, xo_ref):
    gate = mod_ref[0][2:3]
    xo_ref[0] = x_ref[0] + gate * _dot_tn(ot_ref[0], w_ref[...])


def _out_proj(ot, x, mod, w_out, *, tm):
    b, l, d = x.shape
    row = lambda i, j: (i, j, 0)
    return pl.pallas_call(
        _out_proj_kernel,
        grid=(b, l // tm),
        in_specs=[
            pl.BlockSpec((1, ot.shape[1], tm), lambda i, j: (i, 0, j)),
            pl.BlockSpec((1, tm, d), row),
            pl.BlockSpec((1, 6, d), lambda i, j: (i, 0, 0)),
            _const_spec(w_out.shape),
        ],
        out_specs=pl.BlockSpec((1, tm, d), row),
        out_shape=jax.ShapeDtypeStruct(x.shape, F32),
        compiler_params=_cparams("parallel", "parallel"),
        name="mla_out",
    )(ot, x, mod, w_out)


def _ffn_kernel(xm_ref, xp_ref, xn_ref, mod_ref, g_ref, wa_ref, wg_ref, ca_ref, cg_ref, wd_ref,
                xo_ref, h_scr, ua_scr, ug_scr, act_scr, *, tm, fc):
    i = pl.program_id(1)
    mod = mod_ref[0]
    g = g_ref[...]
    hal = CONV_HALO

    def hidden(x):
        return _modulate(x, g, mod[3:4], mod[4:5])

    h_scr[0:hal] = jnp.where(i == 0, 0.0, hidden(xp_ref[0])).astype(BF16)
    h_scr[hal:hal + tm] = hidden(xm_ref[0]).astype(BF16)
    h_scr[hal + tm:] = jnp.where(i == pl.num_programs(1) - 1, 0.0, hidden(xn_ref[0])).astype(BF16)

    def conv(u_scr, slab, cw):
        return (u_scr[slab, hal - 1:hal - 1 + tm, :] * cw[0:1] + u_scr[slab, hal:hal + tm, :] * cw[1:2]
                + u_scr[slab, hal + 1:hal + 1 + tm, :] * cw[2:3] + cw[3:4])

    per = fc // LANES
    for ci in range(wa_ref.shape[1] // fc):
        cols = slice(ci * fc, (ci + 1) * fc)
        ua = _dot(h_scr[...], wa_ref[:, cols])
        ug = _dot(h_scr[...], wg_ref[:, cols])
        for s in range(per):
            ua_scr[ci * per + s] = ua[:, s * LANES:(s + 1) * LANES]
            ug_scr[ci * per + s] = ug[:, s * LANES:(s + 1) * LANES]
        for s in range(per):
            slab = ci * per + s
            lanes = slice(slab * LANES, (slab + 1) * LANES)
            a = conv(ua_scr, slab, ca_ref[:, lanes])
            gt = conv(ug_scr, slab, cg_ref[:, lanes])
            act_scr[:, lanes] = (gt * _sigmoid(gt) * a).astype(BF16)
    xo_ref[0] = xm_ref[0] + mod[5:6] * _dot(act_scr[...], wd_ref[...])


def _conv_ffn(x, mod, norm_g, wa, wg, ca, cg, wd, *, tm, fc):
    b, l, d = x.shape
    hal = CONV_HALO
    nt = l // tm
    per = tm // hal
    last = l // hal - 1
    d_ff = wa.shape[1]
    return pl.pallas_call(
        functools.partial(_ffn_kernel, tm=tm, fc=fc),
        grid=(b, nt),
        in_specs=[
            pl.BlockSpec((1, tm, d), lambda i, j: (i, j, 0)),
            pl.BlockSpec((1, hal, d), lambda i, j: (i, jnp.maximum(j * per - 1, 0), 0)),
            pl.BlockSpec((1, hal, d), lambda i, j: (i, jnp.minimum((j + 1) * per, last), 0)),
            pl.BlockSpec((1, 6, d), lambda i, j: (i, 0, 0)),
            _const_spec(norm_g.shape),
            _const_spec(wa.shape),
            _const_spec(wg.shape),
            _const_spec(ca.shape),
            _const_spec(cg.shape),
            _const_spec(wd.shape),
        ],
        out_specs=pl.BlockSpec((1, tm, d), lambda i, j: (i, j, 0)),
        out_shape=jax.ShapeDtypeStruct(x.shape, F32),
        scratch_shapes=[
            pltpu.VMEM((tm + 2 * hal, d), BF16),
            pltpu.VMEM((d_ff // LANES, tm + 2 * hal, LANES), F32),
            pltpu.VMEM((d_ff // LANES, tm + 2 * hal, LANES), F32),
            pltpu.VMEM((tm, d_ff), BF16),
        ],
        compiler_params=_cparams("parallel", "parallel"),
        name="conv_ffn",
    )(x, x, x, mod, norm_g, wa, wg, ca, cg, wd)


def _rope_tables(length):
    nf = MLA_ROPE // 4
    t = jnp.arange(length)
    pos = jnp.stack([(t // GRID_W).astype(F32), (t % GRID_W).astype(F32)], axis=1)
    inv = ROPE_THETA ** (-jnp.arange(nf, dtype=F32) / nf)
    ang = pos[:, :, None] * inv
    cos = jnp.cos(ang)[:, :, None, :] * jnp.ones((1, 1, 2, 1), F32)
    sin = jnp.sin(ang)[:, :, None, :] * jnp.array([-1.0, 1.0], F32)[None, None, :, None]
    pad = lambda a: jnp.pad(a.reshape(length, MLA_ROPE), ((0, 0), (0, 128 - MLA_ROPE)))
    return pad(cos), pad(sin)


def _rope_partner(a):
    nf = MLA_ROPE // 4
    r = a.reshape(a.shape[:-1] + (2, 2, nf))
    return jnp.flip(r, axis=-2).reshape(a.shape)


def _pad_last(a, width):
    return jnp.pad(a, [(0, 0)] * (a.ndim - 1) + [(0, width - a.shape[-1])])


def _mla_weights(w_down, q_lora_norm, kv_lora_norm, w_uq, w_ukv, q_norm, k_norm):
    q_rank = q_lora_norm.shape[0]
    kv_rank = kv_lora_norm.shape[0]
    qk = MLA_NOPE + MLA_ROPE
    w_pe = w_down[:, q_rank + kv_rank:]
    wuq = w_uq.reshape(q_rank, MLA_HEADS, qk)
    wuq_rope = wuq[..., MLA_NOPE:]
    wukv = w_ukv.reshape(kv_rank, MLA_HEADS, MLA_NOPE + MLA_V)

    def gains(g):
        rope = g[MLA_NOPE:]
        return jnp.stack([g[:MLA_NOPE], _pad_last(rope, 128), _pad_last(_rope_partner(rope), 128)])

    return {
        "wdq": w_down[:, :q_rank].astype(BF16),
        "wdkv": w_down[:, q_rank:q_rank + kv_rank].astype(BF16),
        "wdpe": jnp.concatenate([_pad_last(w_pe, 128), _pad_last(_rope_partner(w_pe), 128)],
                                axis=1).astype(BF16),
        "qln": q_lora_norm[None, :],
        "kvln": kv_lora_norm[None, :],
        "wuq": _pad_last(wuq, MLA_QPAD).reshape(q_rank, MLA_HEADS * MLA_QPAD).astype(BF16),
        "wuqs": _pad_last(_rope_partner(wuq_rope), 128).reshape(q_rank, MLA_HEADS * 128).astype(BF16),
        "wuk": wukv[..., :MLA_NOPE].reshape(kv_rank, MLA_HEADS * MLA_NOPE).astype(BF16),
        "wuvt": wukv[..., MLA_NOPE:].reshape(kv_rank, MLA_HEADS * MLA_V).T.astype(BF16),
        "qn": gains(q_norm),
        "kn": gains(k_norm),
    }


def kernel(x, c, ctx, c_ctx, w_ada, b_ada, norm_mix, norm_ffn, gla_w_in, gla_gate_w1, gla_gate_w2, gla_gate_b, gla_out_norm, gla_w_out, mla_w_down, mla_q_lora_norm, mla_kv_lora_norm, mla_w_uq, mla_w_ukv, mla_q_norm, mla_k_norm, mla_w_out, ffn_w_up, ffn_conv_w, ffn_conv_b, ffn_w_down):
    bsz, seq, d = x.shape
    lc = ctx.shape[1]
    depth = w_ada.shape[0]
    d_ff = ffn_w_down.shape[1]
    dqk = gla_gate_w2.shape[-1]
    rank = gla_gate_w1.shape[-1]
    dv = gla_out_norm.shape[-1]

    cond_rows = -(-(bsz + 1) // 8) * 8
    cond = jnp.zeros((cond_rows, d), F32).at[:bsz].set(c).at[bsz].set(c_ctx)
    mods = _ada_all(cond, w_ada, b_ada)

    cos_l, sin_l = _rope_tables(seq)
    cos_c = _pad_last(jnp.ones((lc, MLA_ROPE), F32), 128)
    sin_c = jnp.zeros((lc, 128), F32)

    tm_l = min(seq, 512)
    tm_c = min(lc, 512)
    xc = ctx
    for i in range(depth):
        last = i == depth - 1
        j = i // 2
        mod_l = mods[i, :bsz].reshape(bsz, 6, d)
        mod_c = jnp.broadcast_to(mods[i, bsz].reshape(1, 6, d), (bsz, 6, d))
        g_mix = norm_mix[i][None, :]
        if i % 2 == 0:
            w_in = gla_w_in[j].astype(BF16)
            w1 = _pad_last(jnp.concatenate([gla_gate_w1[j, 0], gla_gate_w1[j, 1]], axis=1),
                           GLA_RANK_PAD).astype(BF16)
            w2 = jnp.zeros((GLA_RANK_PAD, 2 * dqk), F32)
            w2 = w2.at[:rank, :dqk].set(gla_gate_w2[j, 0]).at[rank:2 * rank, dqk:].set(gla_gate_w2[j, 1])
            w2 = w2.astype(BF16)
            gb = gla_gate_b[j].reshape(1, 2 * dqk)
            gn = gla_out_norm[j][None, :]
            w_out = gla_w_out[j].astype(BF16)
            s0 = jnp.zeros((bsz, GLA_HEADS, dv, dqk // GLA_HEADS), F32)

            qc, kc, vc, rc, gc = _gla_proj(xc, mod_c, g_mix, w_in, w1, w2, gb, tm=tm_c)
            q, k, v, r, g = _gla_proj(x, mod_l, g_mix, w_in, w1, w2, gb, tm=tm_l)
            if last:
                _, s_fwd = _gla_scan(qc, kc, vc, gc, s0, direction=0, want_o=False)
                _, s_bwd = _gla_scan(qc, kc, vc, gc, s0, direction=1, want_o=False)
            else:
                oc_b, s_bwd = _gla_scan(qc, kc, vc, gc, s0, direction=1)
                xc, s_fwd = _gla_scan(qc, kc, vc, gc, s0, direction=0,
                                      fused=(oc_b, rc, xc, mod_c, gn, w_out))
            o_b, _ = _gla_scan(q, k, v, g, s_bwd, direction=1)
            x, _ = _gla_scan(q, k, v, g, s_fwd, direction=0, fused=(o_b, r, x, mod_l, gn, w_out))
        else:
            w = _mla_weights(mla_w_down[j], mla_q_lora_norm[j], mla_kv_lora_norm[j], mla_w_uq[j],
                             mla_w_ukv[j], mla_q_norm[j], mla_k_norm[j])
            w_out = mla_w_out[j].astype(BF16)
            if last:
                kc, vtc = _mla_proj(xc, mod_c, g_mix, cos_c, sin_c, w, want_q=False, tm=tm_c)
            else:
                qc, kc, vtc = _mla_proj(xc, mod_c, g_mix, cos_c, sin_c, w, want_q=True, tm=tm_c)
            q, k, vt = _mla_proj(x, mod_l, g_mix, cos_l, sin_l, w, want_q=True, tm=tm_l)
            ot = _attention(q, kc, vtc, k, vt, tq=min(seq, 512))
            x = _out_proj(ot, x, mod_l, w_out, tm=tm_l)
            if not last:
                otc = _attention(qc, kc, vtc, tq=min(lc, 512))
                xc = _out_proj(otc, xc, mod_c, w_out, tm=tm_c)

        g_ffn = norm_ffn[i][None, :]
        wa = ffn_w_up[i, :, :d_ff].astype(BF16)
        wg = ffn_w_up[i, :, d_ff:].astype(BF16)
        conv = jnp.concatenate([ffn_conv_w[i], ffn_conv_b[i][None, :]], axis=0)
        ca, cg = conv[:, :d_ff], conv[:, d_ff:]
        wd = ffn_w_down[i].astype(BF16)
        x = _conv_ffn(x, mod_l, g_ffn, wa, wg, ca, cg, wd, tm=tm_l, fc=FFN_COLS)
        if not last:
            xc = _conv_ffn(xc, mod_c, g_ffn, wa, wg, ca, cg, wd, tm=tm_c, fc=FFN_COLS)
    return x
```

```python
import functools

import jax
import jax.numpy as jnp
from jax import lax
from jax.experimental import pallas as pl
from jax.experimental.pallas import tpu as pltpu

F32 = jnp.float32
BF16 = jnp.bfloat16

LANES = 128
SUBLANES = 8
EPS = 1e-6
GRID_W = 64
ROPE_THETA = 10000.0
LOG2_E = 1.4426950408889634

GLA_HEADS = 4
GLA_GATE_NORMALIZER = 16.0
GLA_CHUNK = 128
GLA_RANK_PAD = 128

MLA_HEADS = 8
MLA_NOPE = 128
MLA_ROPE = 64
MLA_V = 128
MLA_QPAD = 256
SHIFT_LANE = MLA_ROPE
ATTN_MAX_SHIFT = 50.0

FFN_COLS = 256
CONV_HALO = 16
VMEM_LIMIT = 56 * 1024 * 1024


def _cparams(*sem):
    return pltpu.CompilerParams(dimension_semantics=sem, vmem_limit_bytes=VMEM_LIMIT)


def _const_spec(shape):
    nd = len(shape)
    return pl.BlockSpec(shape, lambda *_: (0,) * nd, pipeline_mode=pl.Buffered(1))


def _dot(a, b):
    return jnp.dot(a, b, preferred_element_type=F32)


def _dot_nt(a, b):
    return lax.dot_general(a, b, (((1,), (1,)), ((), ())), preferred_element_type=F32)


def _dot_tn(a, b):
    return lax.dot_general(a, b, (((0,), (0,)), ((), ())), preferred_element_type=F32)


def _sigmoid(x):
    return 1.0 / (1.0 + jnp.exp(-x))


def _rms_scale(x, width):
    ss = jnp.sum(x * x, axis=-1, keepdims=True)
    return lax.rsqrt(ss * (1.0 / width) + EPS)


def _modulate(x, g, shift, scale):
    return x * _rms_scale(x, x.shape[-1]) * (g * (1.0 + scale)) + shift


def _ada_kernel(cond_ref, w_ref, b_ref, o_ref):
    cond = cond_ref[...]
    s = cond * _sigmoid(cond)
    o_ref[0] = jnp.dot(s, w_ref[0], preferred_element_type=F32,
                       precision=lax.Precision.HIGHEST) + b_ref[0]


def _ada_all(cond, w_ada, b_ada):
    depth, d, d6 = w_ada.shape
    rows = cond.shape[0]
    return pl.pallas_call(
        _ada_kernel,
        grid=(depth, d6 // d),
        in_specs=[
            pl.BlockSpec((rows, d), lambda i, j: (0, 0)),
            pl.BlockSpec((1, d, d), lambda i, j: (i, 0, j)),
            pl.BlockSpec((1, 1, d), lambda i, j: (i, 0, j)),
        ],
        out_specs=pl.BlockSpec((1, rows, d), lambda i, j: (i, 0, j)),
        out_shape=jax.ShapeDtypeStruct((depth, rows, d6), F32),
        compiler_params=_cparams("parallel", "parallel"),
        name="ada_mod",
    )(cond, w_ada, b_ada.reshape(depth, 1, d6))


def _gla_proj_kernel(x_ref, mod_ref, g_ref, win_ref, w1_ref, w2_ref, gb_ref,
                     q_ref, k_ref, v_ref, r_ref, gate_ref, *, dqk):
    mod = mod_ref[0]
    h = _modulate(x_ref[0], g_ref[...], mod[0:1], mod[1:2]).astype(BF16)
    p = _dot(h, win_ref[...])
    dk = dqk // GLA_HEADS
    q_ref[0] = (p[:, :dqk] * (dk ** -0.5)).astype(BF16)
    k_ref[0] = p[:, dqk:2 * dqk].astype(BF16)
    dv = (p.shape[1] - 2 * dqk) // 2
    v_ref[0] = p[:, 2 * dqk:2 * dqk + dv].astype(BF16)
    r_ref[0] = p[:, 2 * dqk + dv:].astype(BF16)
    low = _dot(h, w1_ref[...]).astype(BF16)
    z = _dot(low, w2_ref[...]) + gb_ref[...]
    log_sig = jnp.minimum(z, 0.0) - jnp.log(1.0 + jnp.exp(-jnp.abs(z)))
    gate_ref[0] = log_sig * (1.0 / GLA_GATE_NORMALIZER)


def _gla_proj(x, mod, norm_g, w_in, w1, w2, gb, *, tm):
    b, l, d = x.shape
    dqk = w2.shape[1] // 2
    dv = (w_in.shape[1] - 2 * dqk) // 2
    row = lambda i, j: (i, j, 0)
    return pl.pallas_call(
        functools.partial(_gla_proj_kernel, dqk=dqk),
        grid=(b, l // tm),
        in_specs=[
            pl.BlockSpec((1, tm, d), row),
            pl.BlockSpec((1, 6, d), lambda i, j: (i, 0, 0)),
            _const_spec(norm_g.shape),
            _const_spec(w_in.shape),
            _const_spec(w1.shape),
            _const_spec(w2.shape),
            _const_spec(gb.shape),
        ],
        out_specs=[
            pl.BlockSpec((1, tm, dqk), row),
            pl.BlockSpec((1, tm, dqk), row),
            pl.BlockSpec((1, tm, dv), row),
            pl.BlockSpec((1, tm, dv), row),
            pl.BlockSpec((1, tm, 2 * dqk), row),
        ],
        out_shape=[
            jax.ShapeDtypeStruct((b, l, dqk), BF16),
            jax.ShapeDtypeStruct((b, l, dqk), BF16),
            jax.ShapeDtypeStruct((b, l, dv), BF16),
            jax.ShapeDtypeStruct((b, l, dv), BF16),
            jax.ShapeDtypeStruct((b, l, 2 * dqk), F32),
        ],
        compiler_params=_cparams("parallel", "parallel"),
        name="gla_proj",
    )(x, mod, norm_g, w_in, w1, w2, gb)


def _gla_scan_kernel(*refs, reverse, want_o, fuse_out, tblk):
    if fuse_out:
        (q_ref, k_ref, v_ref, g_ref, s0_ref, ob_ref, r_ref, x_ref, mod_ref, gn_ref, wout_ref,
         xo_ref, st_ref, y_scr) = refs
    elif want_o:
        q_ref, k_ref, v_ref, g_ref, s0_ref, o_ref, st_ref = refs
    else:
        q_ref, k_ref, v_ref, g_ref, s0_ref, st_ref = refs

    @pl.when(pl.program_id(1) == 0)
    def _():
        st_ref[...] = s0_ref[...]

    c = GLA_CHUNK
    dk = q_ref.shape[2] // GLA_HEADS
    dv = v_ref.shape[2] // GLA_HEADS
    row = lax.broadcasted_iota(jnp.int32, (c, c), 0)
    col = lax.broadcasted_iota(jnp.int32, (c, c), 1)
    mask = (col >= row) if reverse else (col <= row)
    tri = mask.astype(BF16)
    i_mid = c // 2 if reverse else c // 2 - 1
    i_end = 0 if reverse else c - 1

    nch = tblk // c
    for ci in (reversed(range(nch)) if reverse else range(nch)):
        rows = slice(ci * c, (ci + 1) * c)
        g = g_ref[0, rows, :]
        g_hi = g.astype(BF16)
        rem = g - g_hi.astype(F32)
        g_mid = rem.astype(BF16)
        g_lo = (rem - g_mid.astype(F32)).astype(BF16)
        cum = _dot(tri, g_hi) + _dot(tri, g_mid) + _dot(tri, g_lo)
        c_mid = cum[i_mid:i_mid + 1, :]
        c_end = cum[i_end:i_end + 1, :]
        q = q_ref[0, rows, :].astype(F32)
        k = k_ref[0, rows, :].astype(F32)
        k_in = (k * jnp.exp(c_mid - cum)).astype(BF16)
        k_st = (k * jnp.exp(c_end - cum)).astype(BF16)
        decay_end = jnp.exp(c_end)
        if want_o:
            q_in = (q * jnp.exp(cum - c_mid)).astype(BF16)
            q_st = (q * jnp.exp(cum)).astype(BF16)
        for h in range(GLA_HEADS):
            ks = slice(h * dk, (h + 1) * dk)
            vs = slice(h * dv, (h + 1) * dv)
            v = v_ref[0, rows, vs]
            st = st_ref[0, h]
            if want_o:
                scores = jnp.where(mask, _dot_nt(q_in[:, ks], k_in[:, ks]), 0.0).astype(BF16)
                o = _dot(scores, v) + _dot_nt(q_st[:, ks], st.astype(BF16))
                if fuse_out:
                    o = o + ob_ref[0, rows, vs].astype(F32)
                    r = r_ref[0, rows, vs].astype(F32)
                    y = o * _rms_scale(o, dv) * gn_ref[...] * (r * _sigmoid(r))
                    y_scr[rows, vs] = y.astype(BF16)
                else:
                    o_ref[0, rows, vs] = o.astype(BF16)
            st_ref[0, h] = st * decay_end[:, ks] + _dot_tn(v, k_st[:, ks])

    if fuse_out:
        gate = mod_ref[0][2:3]
        xo_ref[0] = x_ref[0] + gate * _dot(y_scr[...], wout_ref[...])


def _gla_scan(q, k, v, gates, s0, *, direction, want_o=True, fused=None):
    b, l, dqk = q.shape
    dvt = v.shape[2]
    tblk = min(l, 512)
    nblk = l // tblk
    reverse = direction == 1
    blk = (lambda i, j: (i, nblk - 1 - j, 0)) if reverse else (lambda i, j: (i, j, 0))
    gblk = (lambda i, j: (i, nblk - 1 - j, 1)) if reverse else (lambda i, j: (i, j, 0))
    st_spec = pl.BlockSpec((1,) + s0.shape[1:], lambda i, j: (i, 0, 0, 0))
    in_specs = [
        pl.BlockSpec((1, tblk, dqk), blk),
        pl.BlockSpec((1, tblk, dqk), blk),
        pl.BlockSpec((1, tblk, dvt), blk),
        pl.BlockSpec((1, tblk, dqk), gblk),
        st_spec,
    ]
    args = [q, k, v, gates, s0]
    st_shape = jax.ShapeDtypeStruct(s0.shape, F32)
    scratch = []
    if fused is not None:
        o_other, r, x, mod, gn, w_out = fused
        d = x.shape[2]
        in_specs += [
            pl.BlockSpec((1, tblk, dvt), blk),
            pl.BlockSpec((1, tblk, dvt), blk),
            pl.BlockSpec((1, tblk, d), blk),
            pl.BlockSpec((1, 6, d), lambda i, j: (i, 0, 0)),
            _const_spec(gn.shape),
            _const_spec(w_out.shape),
        ]
        args += [o_other, r, x, mod, gn, w_out]
        out_specs = [pl.BlockSpec((1, tblk, d), blk), st_spec]
        out_shape = [jax.ShapeDtypeStruct(x.shape, F32), st_shape]
        scratch = [pltpu.VMEM((tblk, dvt), BF16)]
    elif want_o:
        out_specs = [pl.BlockSpec((1, tblk, dvt), blk), st_spec]
        out_shape = [jax.ShapeDtypeStruct((b, l, dvt), BF16), st_shape]
    else:
        out_specs = [st_spec]
        out_shape = [st_shape]
    outs = pl.pallas_call(
        functools.partial(_gla_scan_kernel, reverse=reverse, want_o=want_o,
                          fuse_out=fused is not None, tblk=tblk),
        grid=(b, nblk),
        in_specs=in_specs,
        out_specs=out_specs,
        out_shape=out_shape,
        scratch_shapes=scratch,
        compiler_params=_cparams("parallel", "arbitrary"),
        name="gla_scan_%s%s" % ("bwd" if reverse else "fwd", "_out" if fused is not None else ""),
    )(*args)
    if want_o:
        return outs[0], outs[1]
    return None, outs[0]


def _mla_proj_kernel(*refs, want_q):
    (x_ref, mod_ref, g_ref, cos_ref, sin_ref, wdq_ref, wdkv_ref, wdpe_ref, qln_ref, kvln_ref,
     wuq_ref, wuqs_ref, wuk_ref, wuvt_ref, qn_ref, kn_ref) = refs[:16]
    if want_q:
        q_ref, k_ref, vt_ref = refs[16:]
    else:
        k_ref, vt_ref = refs[16:]
    mod = mod_ref[0]
    h = _modulate(x_ref[0], g_ref[...], mod[0:1], mod[1:2]).astype(BF16)
    cos = cos_ref[...]
    sin = sin_ref[...]
    qn = qn_ref[...]
    kn = kn_ref[...]

    def rope_part(raw, raw_partner, gains):
        return (raw * (cos * gains[1:2]) + raw_partner * (sin * gains[2:3])) * _rms_scale(raw, MLA_ROPE)

    c_kv = _dot(h, wdkv_ref[...])
    c_kv = (c_kv * _rms_scale(c_kv, c_kv.shape[-1]) * kvln_ref[...]).astype(BF16)
    vt_ref[0] = _dot_nt(wuvt_ref[...], c_kv).astype(BF16)
    kn_all = _dot(c_kv, wuk_ref[...])
    kpe = _dot(h, wdpe_ref[...])
    k_rope = (rope_part(kpe[:, :128], kpe[:, 128:], kn) + kn[3:4]).astype(BF16)
    if want_q:
        c_q = _dot(h, wdq_ref[...])
        c_q = (c_q * _rms_scale(c_q, c_q.shape[-1]) * qln_ref[...]).astype(BF16)
        qm = _dot(c_q, wuq_ref[...])
        qs = _dot(c_q, wuqs_ref[...])
        q_scale = (MLA_NOPE + MLA_ROPE) ** -0.5 * LOG2_E
    for hd in range(MLA_HEADS):
        k_nope = kn_all[:, hd * MLA_NOPE:(hd + 1) * MLA_NOPE]
        k_ref[0, hd, :, :MLA_NOPE] = (k_nope * _rms_scale(k_nope, MLA_NOPE) * kn[0:1]).astype(BF16)
        k_ref[0, hd, :, MLA_NOPE:] = k_rope
        if want_q:
            base = hd * MLA_QPAD
            q_nope = qm[:, base:base + MLA_NOPE]
            q_nope = q_nope * _rms_scale(q_nope, MLA_NOPE) * (qn[0:1] * q_scale)
            q_ref[0, hd, :, :MLA_NOPE] = q_nope.astype(BF16)
            q_rope = rope_part(qm[:, base + MLA_NOPE:base + MLA_QPAD],
                               qs[:, hd * 128:(hd + 1) * 128], qn) * q_scale + qn[3:4]
            q_ref[0, hd, :, MLA_NOPE:] = q_rope.astype(BF16)


def _mla_proj(x, mod, norm_g, cos, sin, w, *, want_q, tm):
    b, l, d = x.shape
    consts = [w["wdq"], w["wdkv"], w["wdpe"], w["qln"], w["kvln"], w["wuq"], w["wuqs"], w["wuk"],
              w["wuvt"], w["qn"], w["kn"]]
    hspec = pl.BlockSpec((1, MLA_HEADS, tm, MLA_QPAD), lambda i, j: (i, 0, j, 0))
    hshape = jax.ShapeDtypeStruct((b, MLA_HEADS, l, MLA_QPAD), BF16)
    out_specs = [hspec, pl.BlockSpec((1, MLA_HEADS * MLA_V, tm), lambda i, j: (i, 0, j))]
    out_shape = [hshape, jax.ShapeDtypeStruct((b, MLA_HEADS * MLA_V, l), BF16)]
    if want_q:
        out_specs = [hspec] + out_specs
        out_shape = [hshape] + out_shape
    return pl.pallas_call(
        functools.partial(_mla_proj_kernel, want_q=want_q),
        grid=(b, l // tm),
        in_specs=[
            pl.BlockSpec((1, tm, d), lambda i, j: (i, j, 0)),
            pl.BlockSpec((1, 6, d), lambda i, j: (i, 0, 0)),
            _const_spec(norm_g.shape),
            pl.BlockSpec((tm, 128), lambda i, j: (j, 0)),
            pl.BlockSpec((tm, 128), lambda i, j: (j, 0)),
        ] + [_const_spec(a.shape) for a in consts],
        out_specs=out_specs,
        out_shape=out_shape,
        compiler_params=_cparams("parallel", "parallel"),
        name="mla_proj_q" if want_q else "mla_proj_kv",
    )(x, mod, norm_g, cos, sin, *consts)


def _attn_kernel(*refs, with_latent, tk):
    if with_latent:
        q_ref, kc_ref, vtc_ref, k_ref, vt_ref, ot_ref = refs
    else:
        q_ref, kc_ref, vtc_ref, ot_ref = refs
    q = q_ref[0, 0]
    chunks = [(kc_ref, vtc_ref, 0, kc_ref.shape[2])]
    if with_latent:
        chunks += [(k_ref, vt_ref, j * tk, tk) for j in range(k_ref.shape[2] // tk)]

    def scores(idx):
        kr, _, start, size = chunks[idx]
        return _dot_nt(kr[0, 0, start:start + size, :], q)

    s_next = scores(0)
    m = denom = acc = None
    for idx, (_, vr, start, size) in enumerate(chunks):
        s = s_next
        if idx + 1 < len(chunks):
            s_next = scores(idx + 1)
        vt = vr[0, :, start:start + size]
        cmax = jnp.max(s, axis=0, keepdims=True)
        if m is None:
            m = cmax
            p = jnp.exp2(s - m)
            denom = jnp.sum(p, axis=0, keepdims=True)
            acc = _dot(vt, p.astype(BF16))
        else:
            m_new = jnp.maximum(m, cmax)
            alpha = jnp.exp2(m - m_new)
            p = jnp.exp2(s - m_new)
            denom = denom * alpha + jnp.sum(p, axis=0, keepdims=True)
            acc = acc * alpha + _dot(vt, p.astype(BF16))
            m = m_new
    ot_ref[0] = (acc * (1.0 / denom)).astype(BF16)


def _attn_shifted_kernel(q_ref, kc_ref, vtc_ref, k_ref, vt_ref, ot_ref, *, tk):
    q = q_ref[0, 0]
    chunks = [(kc_ref, vtc_ref, 0, kc_ref.shape[2])]
    chunks += [(k_ref, vt_ref, j * tk, tk) for j in range(k_ref.shape[2] // tk)]

    def scores(idx):
        kr, _, start, size = chunks[idx]
        return _dot_nt(kr[0, 0, start:start + size, :], q)

    s_next = scores(0)
    acc = denom8 = None
    for idx, (_, vr, start, size) in enumerate(chunks):
        s = s_next
        if idx + 1 < len(chunks):
            s_next = scores(idx + 1)
        p = jnp.exp2(s)
        part = jnp.sum(p.reshape(size // SUBLANES, SUBLANES, p.shape[1]), axis=0)
        y = _dot(vr[0, :, start:start + size], p.astype(BF16))
        acc = y if acc is None else acc + y
        denom8 = part if denom8 is None else denom8 + part
    denom = jnp.sum(denom8, axis=0, keepdims=True)
    ot_ref[0] = (acc * (1.0 / denom)).astype(BF16)


def _attention(q, kc, vtc, k=None, vt=None, *, tq, tk=512, shifted=False):
    b, nh, lq, dq = q.shape
    lc = kc.shape[2]
    dv = vtc.shape[1] // nh
    with_latent = k is not None
    in_specs = [
        pl.BlockSpec((1, 1, tq, dq), lambda i, h, j: (i, h, j, 0)),
        pl.BlockSpec((1, 1, lc, dq), lambda i, h, j: (i, h, 0, 0)),
        pl.BlockSpec((1, dv, lc), lambda i, h, j: (i, h, 0)),
    ]
    args = [q, kc, vtc]
    if with_latent:
        lk = k.shape[2]
        in_specs += [pl.BlockSpec((1, 1, lk, dq), lambda i, h, j: (i, h, 0, 0)),
                     pl.BlockSpec((1, dv, lk), lambda i, h, j: (i, h, 0))]
        args += [k, vt]
    if shifted:
        body = functools.partial(_attn_shifted_kernel, tk=tk)
        name = "mla_attn_shifted"
    else:
        body = functools.partial(_attn_kernel, with_latent=with_latent, tk=tk)
        name = "mla_attn" if with_latent else "mla_attn_ctx"
    return pl.pallas_call(
        body,
        grid=(b, nh, lq // tq),
        in_specs=in_specs,
        out_specs=pl.BlockSpec((1, dv, tq), lambda i, h, j: (i, h, j)),
        out_shape=jax.ShapeDtypeStruct((b, nh * dv, lq), BF16),
        compiler_params=_cparams("parallel", "parallel", "parallel"),
        name=name,
    )(*args)


def _latent_attention(use_shift, q, kc, vtc, k, vt):
    lq = q.shape[2]
    lk = k.shape[2]
    fast = functools.partial(_attention, tq=min(lq, 1024), tk=min(lk, 2048), shifted=True)
    exact = functools.partial(_attention, tq=min(lq, 512), tk=min(lk, 512))
    return lax.cond(use_shift, fast, exact, q, kc, vtc, k, vt)


def _out_proj_kernel(ot_ref, x_ref, mod_ref, w_ref, xo_ref):
    gate = mod_ref[0][2:3]
    xo_ref[0] = x_ref[0] + gate * _dot_tn(ot_ref[0], w_ref[...])


def _out_proj(ot, x, mod, w_out, *, tm):
    b, l, d = x.shape
    row = lambda i, j: (i, j, 0)
    return pl.pallas_call(
        _out_proj_kernel,
        grid=(b, l // tm),
        in_specs=[
            pl.BlockSpec((1, ot.shape[1], tm), lambda i, j: (i, 0, j)),
            pl.BlockSpec((1, tm, d), row),
            pl.BlockSpec((1, 6, d), lambda i, j: (i, 0, 0)),
            _const_spec(w_out.shape),
        ],
        out_specs=pl.BlockSpec((1, tm, d), row),
        out_shape=jax.ShapeDtypeStruct(x.shape, F32),
        compiler_params=_cparams("parallel", "parallel"),
        name="mla_out",
    )(ot, x, mod, w_out)


def _ffn_kernel(xm_ref, xp_ref, xn_ref, mod_ref, g_ref, wa_ref, wg_ref, ca_ref, cg_ref, wd_ref,
                xo_ref, h_scr, ua_scr, ug_scr, act_scr, *, tm, fc):
    i = pl.program_id(1)
    mod = mod_ref[0]
    g = g_ref[...]
    hal = CONV_HALO

    def hidden(x):
        return _modulate(x, g, mod[3:4], mod[4:5])

    h_scr[0:hal] = jnp.where(i == 0, 0.0, hidden(xp_ref[0])).astype(BF16)
    h_scr[hal:hal + tm] = hidden(xm_ref[0]).astype(BF16)
    h_scr[hal + tm:] = jnp.where(i == pl.num_programs(1) - 1, 0.0, hidden(xn_ref[0])).astype(BF16)

    def conv(u_scr, slab, cw):
        return (u_scr[slab, hal - 1:hal - 1 + tm, :] * cw[0:1] + u_scr[slab, hal:hal + tm, :] * cw[1:2]
                + u_scr[slab, hal + 1:hal + 1 + tm, :] * cw[2:3] + cw[3:4])

    per = fc // LANES
    for ci in range(wa_ref.shape[1] // fc):
        cols = slice(ci * fc, (ci + 1) * fc)
        ua = _dot(h_scr[...], wa_ref[:, cols])
        ug = _dot(h_scr[...], wg_ref[:, cols])
        for s in range(per):
            ua_scr[ci * per + s] = ua[:, s * LANES:(s + 1) * LANES]
            ug_scr[ci * per + s] = ug[:, s * LANES:(s + 1) * LANES]
        for s in range(per):
            slab = ci * per + s
            lanes = slice(slab * LANES, (slab + 1) * LANES)
            a = conv(ua_scr, slab, ca_ref[:, lanes])
            gt = conv(ug_scr, slab, cg_ref[:, lanes])
            act_scr[:, lanes] = (gt * _sigmoid(gt) * a).astype(BF16)
    xo_ref[0] = xm_ref[0] + mod[5:6] * _dot(act_scr[...], wd_ref[...])


def _conv_ffn(x, mod, norm_g, wa, wg, ca, cg, wd, *, tm, fc):
    b, l, d = x.shape
    hal = CONV_HALO
    nt = l // tm
    per = tm // hal
    last = l // hal - 1
    d_ff = wa.shape[1]
    return pl.pallas_call(
        functools.partial(_ffn_kernel, tm=tm, fc=fc),
        grid=(b, nt),
        in_specs=[
            pl.BlockSpec((1, tm, d), lambda i, j: (i, j, 0)),
            pl.BlockSpec((1, hal, d), lambda i, j: (i, jnp.maximum(j * per - 1, 0), 0)),
            pl.BlockSpec((1, hal, d), lambda i, j: (i, jnp.minimum((j + 1) * per, last), 0)),
            pl.BlockSpec((1, 6, d), lambda i, j: (i, 0, 0)),
            _const_spec(norm_g.shape),
            _const_spec(wa.shape),
            _const_spec(wg.shape),
            _const_spec(ca.shape),
            _const_spec(cg.shape),
            _const_spec(wd.shape),
        ],
        out_specs=pl.BlockSpec((1, tm, d), lambda i, j: (i, j, 0)),
        out_shape=jax.ShapeDtypeStruct(x.shape, F32),
        scratch_shapes=[
            pltpu.VMEM((tm + 2 * hal, d), BF16),
            pltpu.VMEM((d_ff // LANES, tm + 2 * hal, LANES), F32),
            pltpu.VMEM((d_ff // LANES, tm + 2 * hal, LANES), F32),
            pltpu.VMEM((tm, d_ff), BF16),
        ],
        compiler_params=_cparams("parallel", "parallel"),
        name="conv_ffn",
    )(x, x, x, mod, norm_g, wa, wg, ca, cg, wd)


def _rope_tables(length):
    nf = MLA_ROPE // 4
    t = jnp.arange(length)
    pos = jnp.stack([(t // GRID_W).astype(F32), (t % GRID_W).astype(F32)], axis=1)
    inv = ROPE_THETA ** (-jnp.arange(nf, dtype=F32) / nf)
    ang = pos[:, :, None] * inv
    cos = jnp.cos(ang)[:, :, None, :] * jnp.ones((1, 1, 2, 1), F32)
    sin = jnp.sin(ang)[:, :, None, :] * jnp.array([-1.0, 1.0], F32)[None, None, :, None]
    pad = lambda a: jnp.pad(a.reshape(length, MLA_ROPE), ((0, 0), (0, 128 - MLA_ROPE)))
    return pad(cos), pad(sin)


def _rope_partner(a):
    nf = MLA_ROPE // 4
    r = a.reshape(a.shape[:-1] + (2, 2, nf))
    return jnp.flip(r, axis=-2).reshape(a.shape)


def _pad_last(a, width):
    return jnp.pad(a, [(0, 0)] * (a.ndim - 1) + [(0, width - a.shape[-1])])


def _mla_weights(w_down, q_lora_norm, kv_lora_norm, w_uq, w_ukv, q_norm, k_norm):
    q_rank = q_lora_norm.shape[0]
    kv_rank = kv_lora_norm.shape[0]
    qk = MLA_NOPE + MLA_ROPE
    w_pe = w_down[:, q_rank + kv_rank:]
    wuq = w_uq.reshape(q_rank, MLA_HEADS, qk)
    wuq_rope = wuq[..., MLA_NOPE:]
    wukv = w_ukv.reshape(kv_rank, MLA_HEADS, MLA_NOPE + MLA_V)

    def max_sq_norm(g):
        return MLA_NOPE * jnp.max(g[:MLA_NOPE] ** 2) + MLA_ROPE * jnp.max(g[MLA_NOPE:] ** 2)

    bound = jnp.sqrt(max_sq_norm(q_norm) * max_sq_norm(k_norm)) * (qk ** -0.5 * LOG2_E * 1.02)
    use_shift = bound <= ATTN_MAX_SHIFT
    shift = jnp.where(use_shift, bound, 0.0)
    lane = (jnp.arange(128) == SHIFT_LANE).astype(F32)

    def gains(g, shift_value):
        rope = g[MLA_NOPE:]
        return jnp.stack([g[:MLA_NOPE], _pad_last(rope, 128), _pad_last(_rope_partner(rope), 128),
                          lane * shift_value])

    return {
        "use_shift": use_shift,
        "wdq": w_down[:, :q_rank].astype(BF16),
        "wdkv": w_down[:, q_rank:q_rank + kv_rank].astype(BF16),
        "wdpe": jnp.concatenate([_pad_last(w_pe, 128), _pad_last(_rope_partner(w_pe), 128)],
                                axis=1).astype(BF16),
        "qln": q_lora_norm[None, :],
        "kvln": kv_lora_norm[None, :],
        "wuq": _pad_last(wuq, MLA_QPAD).reshape(q_rank, MLA_HEADS * MLA_QPAD).astype(BF16),
        "wuqs": _pad_last(_rope_partner(wuq_rope), 128).reshape(q_rank, MLA_HEADS * 128).astype(BF16),
        "wuk": wukv[..., :MLA_NOPE].reshape(kv_rank, MLA_HEADS * MLA_NOPE).astype(BF16),
        "wuvt": wukv[..., MLA_NOPE:].reshape(kv_rank, MLA_HEADS * MLA_V).T.astype(BF16),
        "qn": gains(q_norm, shift),
        "kn": gains(k_norm, jnp.where(use_shift, -1.0, 0.0)),
    }


def kernel(x, c, ctx, c_ctx, w_ada, b_ada, norm_mix, norm_ffn, gla_w_in, gla_gate_w1, gla_gate_w2, gla_gate_b, gla_out_norm, gla_w_out, mla_w_down, mla_q_lora_norm, mla_kv_lora_norm, mla_w_uq, mla_w_ukv, mla_q_norm, mla_k_norm, mla_w_out, ffn_w_up, ffn_conv_w, ffn_conv_b, ffn_w_down):
    bsz, seq, d = x.shape
    lc = ctx.shape[1]
    depth = w_ada.shape[0]
    d_ff = ffn_w_down.shape[1]
    dqk = gla_gate_w2.shape[-1]
    rank = gla_gate_w1.shape[-1]
    dv = gla_out_norm.shape[-1]

    cond_rows = -(-(bsz + 1) // 8) * 8
    cond = jnp.zeros((cond_rows, d), F32).at[:bsz].set(c).at[bsz].set(c_ctx)
    mods = _ada_all(cond, w_ada, b_ada)

    cos_l, sin_l = _rope_tables(seq)
    cos_c = _pad_last(jnp.ones((lc, MLA_ROPE), F32), 128)
    sin_c = jnp.zeros((lc, 128), F32)

    tm_l = min(seq, 512)
    tm_c = min(lc, 512)
    xc = ctx
    for i in range(depth):
        last = i == depth - 1
        j = i // 2
        mod_l = mods[i, :bsz].reshape(bsz, 6, d)
        mod_c = jnp.broadcast_to(mods[i, bsz].reshape(1, 6, d), (bsz, 6, d))
        g_mix = norm_mix[i][None, :]
        if i % 2 == 0:
            w_in = gla_w_in[j].astype(BF16)
            w1 = _pad_last(jnp.concatenate([gla_gate_w1[j, 0], gla_gate_w1[j, 1]], axis=1),
                           GLA_RANK_PAD).astype(BF16)
            w2 = jnp.zeros((GLA_RANK_PAD, 2 * dqk), F32)
            w2 = w2.at[:rank, :dqk].set(gla_gate_w2[j, 0]).at[rank:2 * rank, dqk:].set(gla_gate_w2[j, 1])
            w2 = w2.astype(BF16)
            gb = gla_gate_b[j].reshape(1, 2 * dqk)
            gn = gla_out_norm[j][None, :]
            w_out = gla_w_out[j].astype(BF16)
            s0 = jnp.zeros((bsz, GLA_HEADS, dv, dqk // GLA_HEADS), F32)

            qc, kc, vc, rc, gc = _gla_proj(xc, mod_c, g_mix, w_in, w1, w2, gb, tm=tm_c)
            q, k, v, r, g = _gla_proj(x, mod_l, g_mix, w_in, w1, w2, gb, tm=tm_l)
            if last:
                _, s_fwd = _gla_scan(qc, kc, vc, gc, s0, direction=0, want_o=False)
                _, s_bwd = _gla_scan(qc, kc, vc, gc, s0, direction=1, want_o=False)
            else:
                oc_b, s_bwd = _gla_scan(qc, kc, vc, gc, s0, direction=1)
                xc, s_fwd = _gla_scan(qc, kc, vc, gc, s0, direction=0,
                                      fused=(oc_b, rc, xc, mod_c, gn, w_out))
            o_b, _ = _gla_scan(q, k, v, g, s_bwd, direction=1)
            x, _ = _gla_scan(q, k, v, g, s_fwd, direction=0, fused=(o_b, r, x, mod_l, gn, w_out))
        else:
            w = _mla_weights(mla_w_down[j], mla_q_lora_norm[j], mla_kv_lora_norm[j], mla_w_uq[j],
                             mla_w_ukv[j], mla_q_norm[j], mla_k_norm[j])
            w_out = mla_w_out[j].astype(BF16)
            if last:
                kc, vtc = _mla_proj(xc, mod_c, g_mix, cos_c, sin_c, w, want_q=False, tm=tm_c)
            else:
                qc, kc, vtc = _mla_proj(xc, mod_c, g_mix, cos_c, sin_c, w, want_q=True, tm=tm_c)
            q, k, vt = _mla_proj(x, mod_l, g_mix, cos_l, sin_l, w, want_q=True, tm=tm_l)
            ot = _latent_attention(w["use_shift"], q, kc, vtc, k, vt)
            x = _out_proj(ot, x, mod_l, w_out, tm=tm_l)
            if not last:
                otc = _attention(qc, kc, vtc, tq=min(lc, 512))
                xc = _out_proj(otc, xc, mod_c, w_out, tm=tm_c)

        g_ffn = norm_ffn[i][None, :]
        wa = ffn_w_up[i, :, :d_ff].astype(BF16)
        wg = ffn_w_up[i, :, d_ff:].astype(BF16)
        conv = jnp.concatenate([ffn_conv_w[i], ffn_conv_b[i][None, :]], axis=0)
        ca, cg = conv[:, :d_ff], conv[:, d_ff:]
        wd = ffn_w_down[i].astype(BF16)
        x = _conv_ffn(x, mod_l, g_ffn, wa, wg, ca, cg, wd, tm=tm_l, fc=FFN_COLS)
        if not last:
            xc = _conv_ffn(xc, mod_c, g_ffn, wa, wg, ca, cg, wd, tm=tm_c, fc=FFN_COLS)
    return x
```

```python
import functools

import jax
import jax.numpy as jnp
from jax import lax
from jax.experimental import pallas as pl
from jax.experimental.pallas import tpu as pltpu

F32 = jnp.float32
BF16 = jnp.bfloat16

LANES = 128
SUBLANES = 8
EPS = 1e-6
GRID_W = 64
ROPE_THETA = 10000.0
LOG2_E = 1.4426950408889634

GLA_HEADS = 4
GLA_GATE_NORMALIZER = 16.0
GLA_CHUNK = 128
GLA_RANK_PAD = 128

MLA_HEADS = 8
MLA_NOPE = 128
MLA_ROPE = 64
MLA_V = 128
MLA_QPAD = 256
SHIFT_LANE = MLA_ROPE
ATTN_MAX_SHIFT = 50.0

FFN_COLS = 256
CONV_HALO = 16
VMEM_LIMIT = 56 * 1024 * 1024


def _cparams(*sem):
    return pltpu.CompilerParams(dimension_semantics=sem, vmem_limit_bytes=VMEM_LIMIT)


def _const_spec(shape):
    nd = len(shape)
    return pl.BlockSpec(shape, lambda *_: (0,) * nd, pipeline_mode=pl.Buffered(1))


def _dot(a, b):
    return jnp.dot(a, b, preferred_element_type=F32)


def _dot_nt(a, b):
    return lax.dot_general(a, b, (((1,), (1,)), ((), ())), preferred_element_type=F32)


def _dot_tn(a, b):
    return lax.dot_general(a, b, (((0,), (0,)), ((), ())), preferred_element_type=F32)


def _sigmoid(x):
    return 1.0 / (1.0 + jnp.exp(-x))


def _rms_scale(x, width):
    ss = jnp.sum(x * x, axis=-1, keepdims=True)
    return lax.rsqrt(ss * (1.0 / width) + EPS)


def _modulate(x, g, shift, scale):
    return x * _rms_scale(x, x.shape[-1]) * (g * (1.0 + scale)) + shift


def _ada_kernel(cond_ref, w_ref, b_ref, o_ref):
    cond = cond_ref[...]
    s = cond * _sigmoid(cond)
    o_ref[0] = jnp.dot(s, w_ref[0], preferred_element_type=F32,
                       precision=lax.Precision.HIGHEST) + b_ref[0]


def _ada_all(cond, w_ada, b_ada):
    depth, d, d6 = w_ada.shape
    rows = cond.shape[0]
    return pl.pallas_call(
        _ada_kernel,
        grid=(depth, d6 // d),
        in_specs=[
            pl.BlockSpec((rows, d), lambda i, j: (0, 0)),
            pl.BlockSpec((1, d, d), lambda i, j: (i, 0, j)),
            pl.BlockSpec((1, 1, d), lambda i, j: (i, 0, j)),
        ],
        out_specs=pl.BlockSpec((1, rows, d), lambda i, j: (i, 0, j)),
        out_shape=jax.ShapeDtypeStruct((depth, rows, d6), F32),
        compiler_params=_cparams("parallel", "parallel"),
        name="ada_mod",
    )(cond, w_ada, b_ada.reshape(depth, 1, d6))


def _gla_proj_kernel(x_ref, mod_ref, g_ref, win_ref, w1_ref, w2_ref, gb_ref,
                     v_ref, r_ref, qf_ref, kf_ref, df_ref, qb_ref, kb_ref, db_ref, *, dqk):
    mod = mod_ref[0]
    h = _modulate(x_ref[0], g_ref[...], mod[0:1], mod[1:2]).astype(BF16)
    c = GLA_CHUNK
    dk = dqk // GLA_HEADS
    dv = v_ref.shape[2]
    low = _dot(h, w1_ref[...]).astype(BF16)
    z = _dot(low, w2_ref[...]) + gb_ref[...]
    pqk = _dot(h, win_ref[:, :2 * dqk])
    unit = LOG2_E / GLA_GATE_NORMALIZER
    gates = jnp.minimum(z, 0.0) * unit - jnp.log(1.0 + jnp.exp(-jnp.abs(z))) * unit
    row = lax.broadcasted_iota(jnp.int32, (c, c), 0)
    col = lax.broadcasted_iota(jnp.int32, (c, c), 1)
    for direction, (q_ref, k_ref, d_ref) in enumerate(((qf_ref, kf_ref, df_ref), (qb_ref, kb_ref, db_ref))):
        reverse = direction == 1
        tri = ((col >= row) if reverse else (col <= row)).astype(BF16)
        i_mid = c // 2 if reverse else c // 2 - 1
        i_end = 0 if reverse else c - 1
        g = gates[:, direction * dqk:(direction + 1) * dqk]
        g_hi = g.astype(BF16)
        rem = g - g_hi.astype(F32)
        g_mid = rem.astype(BF16)
        g_lo = (rem - g_mid.astype(F32)).astype(BF16)
        for ci in range(x_ref.shape[1] // c):
            rows = slice(ci * c, (ci + 1) * c)
            cum = _dot(tri, g_hi[rows]) + _dot(tri, g_mid[rows]) + _dot(tri, g_lo[rows])
            c_mid = cum[i_mid:i_mid + 1, :]
            c_end = cum[i_end:i_end + 1, :]
            rel = cum - c_mid
            q_in = pqk[rows, :dqk] * (jnp.exp2(rel) * (dk ** -0.5))
            k_in = pqk[rows, dqk:] * jnp.exp2(-rel)
            q_ref[0, rows, :dqk] = q_in.astype(BF16)
            q_ref[0, rows, dqk:] = (q_in * jnp.exp2(c_mid)).astype(BF16)
            k_ref[0, rows, :dqk] = k_in.astype(BF16)
            k_ref[0, rows, dqk:] = (k_in * jnp.exp2(c_end - c_mid)).astype(BF16)
            d_ref[0, ci * SUBLANES:(ci + 1) * SUBLANES, :] = jnp.broadcast_to(jnp.exp2(c_end), (SUBLANES, dqk))
    pvr = _dot(h, win_ref[:, 2 * dqk:])
    v_ref[0] = pvr[:, :dv].astype(BF16)
    r_ref[0] = pvr[:, dv:].astype(BF16)


def _gla_proj(x, mod, norm_g, w_in, w1, w2, gb, *, tm):
    b, l, d = x.shape
    dqk = w2.shape[1] // 2
    dv = (w_in.shape[1] - 2 * dqk) // 2
    drows = tm // GLA_CHUNK * SUBLANES
    row = lambda i, j: (i, j, 0)
    tok = lambda width, dtype: (pl.BlockSpec((1, tm, width), row), jax.ShapeDtypeStruct((b, l, width), dtype))
    dec = (pl.BlockSpec((1, drows, dqk), row),
           jax.ShapeDtypeStruct((b, l // GLA_CHUNK * SUBLANES, dqk), F32))
    outs = [tok(dv, BF16), tok(dv, BF16)] + 2 * [tok(2 * dqk, BF16), tok(2 * dqk, BF16), dec]
    return pl.pallas_call(
        functools.partial(_gla_proj_kernel, dqk=dqk),
        grid=(b, l // tm),
        in_specs=[
            pl.BlockSpec((1, tm, d), row),
            pl.BlockSpec((1, 6, d), lambda i, j: (i, 0, 0)),
            _const_spec(norm_g.shape),
            _const_spec(w_in.shape),
            _const_spec(w1.shape),
            _const_spec(w2.shape),
            _const_spec(gb.shape),
        ],
        out_specs=[o[0] for o in outs],
        out_shape=[o[1] for o in outs],
        compiler_params=_cparams("parallel", "parallel"),
        name="gla_proj",
    )(x, mod, norm_g, w_in, w1, w2, gb)


def _gla_scan_kernel(*refs, reverse, want_o, fuse_out, tblk):
    if fuse_out:
        (q_ref, k_ref, v_ref, d_ref, s0_ref, ob_ref, r_ref, x_ref, mod_ref, gn_ref, wout_ref,
         xo_ref, st_ref, y_scr) = refs
    elif want_o:
        q_ref, k_ref, v_ref, d_ref, s0_ref, o_ref, st_ref = refs
    else:
        q_ref, k_ref, v_ref, d_ref, s0_ref, st_ref = refs

    @pl.when(pl.program_id(1) == 0)
    def _():
        st_ref[...] = s0_ref[...]

    c = GLA_CHUNK
    dqk = q_ref.shape[2] // 2
    dk = dqk // GLA_HEADS
    dv = v_ref.shape[2] // GLA_HEADS
    row = lax.broadcasted_iota(jnp.int32, (c, c), 0)
    col = lax.broadcasted_iota(jnp.int32, (c, c), 1)
    mask = (col >= row) if reverse else (col <= row)

    nch = tblk // c
    states = [st_ref[0, h] for h in range(GLA_HEADS)]
    for ci in (reversed(range(nch)) if reverse else range(nch)):
        rows = slice(ci * c, (ci + 1) * c)
        for h in range(GLA_HEADS):
            intra = slice(h * dk, (h + 1) * dk)
            state = slice(dqk + h * dk, dqk + (h + 1) * dk)
            vs = slice(h * dv, (h + 1) * dv)
            v = v_ref[0, rows, vs]
            st = states[h]
            if want_o:
                scores = jnp.where(mask, _dot_nt(q_ref[0, rows, intra], k_ref[0, rows, intra]), 0.0)
                o = _dot(scores.astype(BF16), v) + _dot_nt(q_ref[0, rows, state], st.astype(BF16))
                if fuse_out:
                    o = o + ob_ref[0, rows, vs].astype(F32)
                    r = r_ref[0, rows, vs].astype(F32)
                    y = o * _rms_scale(o, dv) * gn_ref[...] * (r * _sigmoid(r))
                    y_scr[rows, vs] = y.astype(BF16)
                else:
                    o_ref[0, rows, vs] = o.astype(BF16)
            decay_end = d_ref[0, ci * SUBLANES:ci * SUBLANES + 1, intra]
            states[h] = st * decay_end + _dot_tn(v, k_ref[0, rows, state])
    for h in range(GLA_HEADS):
        st_ref[0, h] = states[h]

    if fuse_out:
        gate = mod_ref[0][2:3]
        xo_ref[0] = x_ref[0] + gate * _dot(y_scr[...], wout_ref[...])


def _gla_scan(q, k, v, decay, s0, *, direction, want_o=True, fused=None):
    b, l, dq2 = q.shape
    dvt = v.shape[2]
    tblk = min(l, 512)
    nblk = l // tblk
    reverse = direction == 1
    blk = (lambda i, j: (i, nblk - 1 - j, 0)) if reverse else (lambda i, j: (i, j, 0))
    st_spec = pl.BlockSpec((1,) + s0.shape[1:], lambda i, j: (i, 0, 0, 0))
    in_specs = [
        pl.BlockSpec((1, tblk, dq2), blk),
        pl.BlockSpec((1, tblk, dq2), blk),
        pl.BlockSpec((1, tblk, dvt), blk),
        pl.BlockSpec((1, tblk // GLA_CHUNK * SUBLANES, decay.shape[2]), blk),
        st_spec,
    ]
    args = [q, k, v, decay, s0]
    st_shape = jax.ShapeDtypeStruct(s0.shape, F32)
    scratch = []
    if fused is not None:
        o_other, r, x, mod, gn, w_out = fused
        d = x.shape[2]
        in_specs += [
            pl.BlockSpec((1, tblk, dvt), blk),
            pl.BlockSpec((1, tblk, dvt), blk),
            pl.BlockSpec((1, tblk, d), blk),
            pl.BlockSpec((1, 6, d), lambda i, j: (i, 0, 0)),
            _const_spec(gn.shape),
            _const_spec(w_out.shape),
        ]
        args += [o_other, r, x, mod, gn, w_out]
        out_specs = [pl.BlockSpec((1, tblk, d), blk), st_spec]
        out_shape = [jax.ShapeDtypeStruct(x.shape, F32), st_shape]
        scratch = [pltpu.VMEM((tblk, dvt), BF16)]
    elif want_o:
        out_specs = [pl.BlockSpec((1, tblk, dvt), blk), st_spec]
        out_shape = [jax.ShapeDtypeStruct((b, l, dvt), BF16), st_shape]
    else:
        out_specs = [st_spec]
        out_shape = [st_shape]
    outs = pl.pallas_call(
        functools.partial(_gla_scan_kernel, reverse=reverse, want_o=want_o,
                          fuse_out=fused is not None, tblk=tblk),
        grid=(b, nblk),
        in_specs=in_specs,
        out_specs=out_specs,
        out_shape=out_shape,
        scratch_shapes=scratch,
        compiler_params=_cparams("parallel", "arbitrary"),
        name="gla_scan_%s%s" % ("bwd" if reverse else "fwd", "_out" if fused is not None else ""),
    )(*args)
    if want_o:
        return outs[0], outs[1]
    return None, outs[0]


def _mla_proj_kernel(*refs, want_q):
    (x_ref, mod_ref, g_ref, cos_ref, sin_ref, wdq_ref, wdkv_ref, wdpe_ref, qln_ref, kvln_ref,
     wuq_ref, wuqs_ref, wuk_ref, wuvt_ref, qn_ref, kn_ref) = refs[:16]
    if want_q:
        q_ref, k_ref, vt_ref = refs[16:]
    else:
        k_ref, vt_ref = refs[16:]
    mod = mod_ref[0]
    h = _modulate(x_ref[0], g_ref[...], mod[0:1], mod[1:2]).astype(BF16)
    cos = cos_ref[...]
    sin = sin_ref[...]
    qn = qn_ref[...]
    kn = kn_ref[...]

    def rope_part(raw, raw_partner, gains):
        return (raw * (cos * gains[1:2]) + raw_partner * (sin * gains[2:3])) * _rms_scale(raw, MLA_ROPE)

    c_kv = _dot(h, wdkv_ref[...])
    c_kv = (c_kv * _rms_scale(c_kv, c_kv.shape[-1]) * kvln_ref[...]).astype(BF16)
    vt_ref[0] = _dot_nt(wuvt_ref[...], c_kv).astype(BF16)
    kn_all = _dot(c_kv, wuk_ref[...])
    kpe = _dot(h, wdpe_ref[...])
    k_rope = (rope_part(kpe[:, :128], kpe[:, 128:], kn) + kn[3:4]).astype(BF16)
    if want_q:
        c_q = _dot(h, wdq_ref[...])
        c_q = (c_q * _rms_scale(c_q, c_q.shape[-1]) * qln_ref[...]).astype(BF16)
        qm = _dot(c_q, wuq_ref[...])
        qs = _dot(c_q, wuqs_ref[...])
        q_scale = (MLA_NOPE + MLA_ROPE) ** -0.5 * LOG2_E
    for hd in range(MLA_HEADS):
        k_nope = kn_all[:, hd * MLA_NOPE:(hd + 1) * MLA_NOPE]
        k_ref[0, hd, :, :MLA_NOPE] = (k_nope * _rms_scale(k_nope, MLA_NOPE) * kn[0:1]).astype(BF16)
        k_ref[0, hd, :, MLA_NOPE:] = k_rope
        if want_q:
            base = hd * MLA_QPAD
            q_nope = qm[:, base:base + MLA_NOPE]
            q_nope = q_nope * _rms_scale(q_nope, MLA_NOPE) * (qn[0:1] * q_scale)
            q_ref[0, hd, :, :MLA_NOPE] = q_nope.astype(BF16)
            q_rope = rope_part(qm[:, base + MLA_NOPE:base + MLA_QPAD],
                               qs[:, hd * 128:(hd + 1) * 128], qn) * q_scale + qn[3:4]
            q_ref[0, hd, :, MLA_NOPE:] = q_rope.astype(BF16)


def _mla_proj(x, mod, norm_g, cos, sin, w, *, want_q, tm):
    b, l, d = x.shape
    consts = [w["wdq"], w["wdkv"], w["wdpe"], w["qln"], w["kvln"], w["wuq"], w["wuqs"], w["wuk"],
              w["wuvt"], w["qn"], w["kn"]]
    hspec = pl.BlockSpec((1, MLA_HEADS, tm, MLA_QPAD), lambda i, j: (i, 0, j, 0))
    hshape = jax.ShapeDtypeStruct((b, MLA_HEADS, l, MLA_QPAD), BF16)
    out_specs = [hspec, pl.BlockSpec((1, MLA_HEADS * MLA_V, tm), lambda i, j: (i, 0, j))]
    out_shape = [hshape, jax.ShapeDtypeStruct((b, MLA_HEADS * MLA_V, l), BF16)]
    if want_q:
        out_specs = [hspec] + out_specs
        out_shape = [hshape] + out_shape
    return pl.pallas_call(
        functools.partial(_mla_proj_kernel, want_q=want_q),
        grid=(b, l // tm),
        in_specs=[
            pl.BlockSpec((1, tm, d), lambda i, j: (i, j, 0)),
            pl.BlockSpec((1, 6, d), lambda i, j: (i, 0, 0)),
            _const_spec(norm_g.shape),
            pl.BlockSpec((tm, 128), lambda i, j: (j, 0)),
            pl.BlockSpec((tm, 128), lambda i, j: (j, 0)),
        ] + [_const_spec(a.shape) for a in consts],
        out_specs=out_specs,
        out_shape=out_shape,
        compiler_params=_cparams("parallel", "parallel"),
        name="mla_proj_q" if want_q else "mla_proj_kv",
    )(x, mod, norm_g, cos, sin, *consts)


def _attn_kernel(*refs, with_latent, tk):
    if with_latent:
        q_ref, kc_ref, vtc_ref, k_ref, vt_ref, ot_ref = refs
    else:
        q_ref, kc_ref, vtc_ref, ot_ref = refs
    q = q_ref[0, 0]
    chunks = [(kc_ref, vtc_ref, 0, kc_ref.shape[2])]
    if with_latent:
        chunks += [(k_ref, vt_ref, j * tk, tk) for j in range(k_ref.shape[2] // tk)]

    def scores(idx):
        kr, _, start, size = chunks[idx]
        return _dot_nt(kr[0, 0, start:start + size, :], q)

    s_next = scores(0)
    m = denom = acc = None
    for idx, (_, vr, start, size) in enumerate(chunks):
        s = s_next
        if idx + 1 < len(chunks):
            s_next = scores(idx + 1)
        vt = vr[0, :, start:start + size]
        cmax = jnp.max(s, axis=0, keepdims=True)
        if m is None:
            m = cmax
            p = jnp.exp2(s - m)
            denom = jnp.sum(p, axis=0, keepdims=True)
            acc = _dot(vt, p.astype(BF16))
        else:
            m_new = jnp.maximum(m, cmax)
            alpha = jnp.exp2(m - m_new)
            p = jnp.exp2(s - m_new)
            denom = denom * alpha + jnp.sum(p, axis=0, keepdims=True)
            acc = acc * alpha + _dot(vt, p.astype(BF16))
            m = m_new
    ot_ref[0] = (acc * (1.0 / denom)).astype(BF16)


def _attn_shifted_kernel(q_ref, kc_ref, vtc_ref, k_ref, vt_ref, ot_ref, *, tk):
    q = q_ref[0, 0]
    chunks = [(kc_ref, vtc_ref, 0, kc_ref.shape[2])]
    chunks += [(k_ref, vt_ref, j * tk, tk) for j in range(k_ref.shape[2] // tk)]

    def scores(idx):
        kr, _, start, size = chunks[idx]
        return _dot_nt(kr[0, 0, start:start + size, :], q)

    s_next = scores(0)
    acc = denom8 = None
    for idx, (_, vr, start, size) in enumerate(chunks):
        s = s_next
        if idx + 1 < len(chunks):
            s_next = scores(idx + 1)
        p = jnp.exp2(s)
        part = jnp.sum(p.reshape(size // SUBLANES, SUBLANES, p.shape[1]), axis=0)
        y = _dot(vr[0, :, start:start + size], p.astype(BF16))
        acc = y if acc is None else acc + y
        denom8 = part if denom8 is None else denom8 + part
    denom = jnp.sum(denom8, axis=0, keepdims=True)
    ot_ref[0] = (acc * (1.0 / denom)).astype(BF16)


def _attention(q, kc, vtc, k=None, vt=None, *, tq, tk=512, shifted=False):
    b, nh, lq, dq = q.shape
    lc = kc.shape[2]
    dv = vtc.shape[1] // nh
    with_latent = k is not None
    in_specs = [
        pl.BlockSpec((1, 1, tq, dq), lambda i, h, j: (i, h, j, 0)),
        pl.BlockSpec((1, 1, lc, dq), lambda i, h, j: (i, h, 0, 0)),
        pl.BlockSpec((1, dv, lc), lambda i, h, j: (i, h, 0)),
    ]
    args = [q, kc, vtc]
    if with_latent:
        lk = k.shape[2]
        in_specs += [pl.BlockSpec((1, 1, lk, dq), lambda i, h, j: (i, h, 0, 0)),
                     pl.BlockSpec((1, dv, lk), lambda i, h, j: (i, h, 0))]
        args += [k, vt]
    if shifted:
        body = functools.partial(_attn_shifted_kernel, tk=tk)
        name = "mla_attn_shifted"
    else:
        body = functools.partial(_attn_kernel, with_latent=with_latent, tk=tk)
        name = "mla_attn" if with_latent else "mla_attn_ctx"
    return pl.pallas_call(
        body,
        grid=(b, nh, lq // tq),
        in_specs=in_specs,
        out_specs=pl.BlockSpec((1, dv, tq), lambda i, h, j: (i, h, j)),
        out_shape=jax.ShapeDtypeStruct((b, nh * dv, lq), BF16),
        compiler_params=_cparams("parallel", "parallel", "parallel"),
        name=name,
    )(*args)


def _latent_attention(use_shift, q, kc, vtc, k, vt):
    lq = q.shape[2]
    lk = k.shape[2]
    fast = functools.partial(_attention, tq=min(lq, 1024), tk=min(lk, 2048), shifted=True)
    exact = functools.partial(_attention, tq=min(lq, 512), tk=min(lk, 512))
    return lax.cond(use_shift, fast, exact, q, kc, vtc, k, vt)


def _out_proj_kernel(ot_ref, x_ref, mod_ref, w_ref, xo_ref):
    gate = mod_ref[0][2:3]
    xo_ref[0] = x_ref[0] + gate * _dot_tn(ot_ref[0], w_ref[...])


def _out_proj(ot, x, mod, w_out, *, tm):
    b, l, d = x.shape
    row = lambda i, j: (i, j, 0)
    return pl.pallas_call(
        _out_proj_kernel,
        grid=(b, l // tm),
        in_specs=[
            pl.BlockSpec((1, ot.shape[1], tm), lambda i, j: (i, 0, j)),
            pl.BlockSpec((1, tm, d), row),
            pl.BlockSpec((1, 6, d), lambda i, j: (i, 0, 0)),
            _const_spec(w_out.shape),
        ],
        out_specs=pl.BlockSpec((1, tm, d), row),
        out_shape=jax.ShapeDtypeStruct(x.shape, F32),
        compiler_params=_cparams("parallel", "parallel"),
        name="mla_out",
    )(ot, x, mod, w_out)


def _ffn_kernel(xm_ref, xp_ref, xn_ref, mod_ref, g_ref, wa_ref, wg_ref, ca_ref, cg_ref, wd_ref,
                xo_ref, h_scr, ua_scr, ug_scr, act_scr, *, tm, fc):
    i = pl.program_id(1)
    mod = mod_ref[0]
    g = g_ref[...]
    hal = CONV_HALO

    def hidden(x):
        return _modulate(x, g, mod[3:4], mod[4:5])

    h_scr[0:hal] = jnp.where(i == 0, 0.0, hidden(xp_ref[0])).astype(BF16)
    h_scr[hal:hal + tm] = hidden(xm_ref[0]).astype(BF16)
    h_scr[hal + tm:] = jnp.where(i == pl.num_programs(1) - 1, 0.0, hidden(xn_ref[0])).astype(BF16)

    def conv(u_scr, slab, cw):
        return (u_scr[slab, hal - 1:hal - 1 + tm, :] * cw[0:1] + u_scr[slab, hal:hal + tm, :] * cw[1:2]
                + u_scr[slab, hal + 1:hal + 1 + tm, :] * cw[2:3] + cw[3:4])

    per = fc // LANES
    for ci in range(wa_ref.shape[1] // fc):
        cols = slice(ci * fc, (ci + 1) * fc)
        ua = _dot(h_scr[...], wa_ref[:, cols])
        ug = _dot(h_scr[...], wg_ref[:, cols])
        for s in range(per):
            ua_scr[ci * per + s] = ua[:, s * LANES:(s + 1) * LANES]
            ug_scr[ci * per + s] = ug[:, s * LANES:(s + 1) * LANES]
        for s in range(per):
            slab = ci * per + s
            lanes = slice(slab * LANES, (slab + 1) * LANES)
            a = conv(ua_scr, slab, ca_ref[:, lanes])
            gt = conv(ug_scr, slab, cg_ref[:, lanes])
            act_scr[:, lanes] = (gt * _sigmoid(gt) * a).astype(BF16)
    xo_ref[0] = xm_ref[0] + mod[5:6] * _dot(act_scr[...], wd_ref[...])


def _conv_ffn(x, mod, norm_g, wa, wg, ca, cg, wd, *, tm, fc):
    b, l, d = x.shape
    hal = CONV_HALO
    nt = l // tm
    per = tm // hal
    last = l // hal - 1
    d_ff = wa.shape[1]
    return pl.pallas_call(
        functools.partial(_ffn_kernel, tm=tm, fc=fc),
        grid=(b, nt),
        in_specs=[
            pl.BlockSpec((1, tm, d), lambda i, j: (i, j, 0)),
            pl.BlockSpec((1, hal, d), lambda i, j: (i, jnp.maximum(j * per - 1, 0), 0)),
            pl.BlockSpec((1, hal, d), lambda i, j: (i, jnp.minimum((j + 1) * per, last), 0)),
            pl.BlockSpec((1, 6, d), lambda i, j: (i, 0, 0)),
            _const_spec(norm_g.shape),
            _const_spec(wa.shape),
            _const_spec(wg.shape),
            _const_spec(ca.shape),
            _const_spec(cg.shape),
            _const_spec(wd.shape),
        ],
        out_specs=pl.BlockSpec((1, tm, d), lambda i, j: (i, j, 0)),
        out_shape=jax.ShapeDtypeStruct(x.shape, F32),
        scratch_shapes=[
            pltpu.VMEM((tm + 2 * hal, d), BF16),
            pltpu.VMEM((d_ff // LANES, tm + 2 * hal, LANES), F32),
            pltpu.VMEM((d_ff // LANES, tm + 2 * hal, LANES), F32),
            pltpu.VMEM((tm, d_ff), BF16),
        ],
        compiler_params=_cparams("parallel", "parallel"),
        name="conv_ffn",
    )(x, x, x, mod, norm_g, wa, wg, ca, cg, wd)


def _rope_tables(length):
    nf = MLA_ROPE // 4
    t = jnp.arange(length)
    pos = jnp.stack([(t // GRID_W).astype(F32), (t % GRID_W).astype(F32)], axis=1)
    inv = ROPE_THETA ** (-jnp.arange(nf, dtype=F32) / nf)
    ang = pos[:, :, None] * inv
    cos = jnp.cos(ang)[:, :, None, :] * jnp.ones((1, 1, 2, 1), F32)
    sin = jnp.sin(ang)[:, :, None, :] * jnp.array([-1.0, 1.0], F32)[None, None, :, None]
    pad = lambda a: jnp.pad(a.reshape(length, MLA_ROPE), ((0, 0), (0, 128 - MLA_ROPE)))
    return pad(cos), pad(sin)


def _rope_partner(a):
    nf = MLA_ROPE // 4
    r = a.reshape(a.shape[:-1] + (2, 2, nf))
    return jnp.flip(r, axis=-2).reshape(a.shape)


def _pad_last(a, width):
    return jnp.pad(a, [(0, 0)] * (a.ndim - 1) + [(0, width - a.shape[-1])])


def _mla_weights(w_down, q_lora_norm, kv_lora_norm, w_uq, w_ukv, q_norm, k_norm):
    q_rank = q_lora_norm.shape[0]
    kv_rank = kv_lora_norm.shape[0]
    qk = MLA_NOPE + MLA_ROPE
    w_pe = w_down[:, q_rank + kv_rank:]
    wuq = w_uq.reshape(q_rank, MLA_HEADS, qk)
    wuq_rope = wuq[..., MLA_NOPE:]
    wukv = w_ukv.reshape(kv_rank, MLA_HEADS, MLA_NOPE + MLA_V)

    def max_sq_norm(g):
        return MLA_NOPE * jnp.max(g[:MLA_NOPE] ** 2) + MLA_ROPE * jnp.max(g[MLA_NOPE:] ** 2)

    bound = jnp.sqrt(max_sq_norm(q_norm) * max_sq_norm(k_norm)) * (qk ** -0.5 * LOG2_E * 1.02)
    use_shift = bound <= ATTN_MAX_SHIFT
    shift = jnp.where(use_shift, bound, 0.0)
    lane = (jnp.arange(128) == SHIFT_LANE).astype(F32)

    def gains(g, shift_value):
        rope = g[MLA_NOPE:]
        return jnp.stack([g[:MLA_NOPE], _pad_last(rope, 128), _pad_last(_rope_partner(rope), 128),
                          lane * shift_value])

    return {
        "use_shift": use_shift,
        "wdq": w_down[:, :q_rank].astype(BF16),
        "wdkv": w_down[:, q_rank:q_rank + kv_rank].astype(BF16),
        "wdpe": jnp.concatenate([_pad_last(w_pe, 128), _pad_last(_rope_partner(w_pe), 128)],
                                axis=1).astype(BF16),
        "qln": q_lora_norm[None, :],
        "kvln": kv_lora_norm[None, :],
        "wuq": _pad_last(wuq, MLA_QPAD).reshape(q_rank, MLA_HEADS * MLA_QPAD).astype(BF16),
        "wuqs": _pad_last(_rope_partner(wuq_rope), 128).reshape(q_rank, MLA_HEADS * 128).astype(BF16),
        "wuk": wukv[..., :MLA_NOPE].reshape(kv_rank, MLA_HEADS * MLA_NOPE).astype(BF16),
        "wuvt": wukv[..., MLA_NOPE:].reshape(kv_rank, MLA_HEADS * MLA_V).T.astype(BF16),
        "qn": gains(q_norm, shift),
        "kn": gains(k_norm, jnp.where(use_shift, -1.0, 0.0)),
    }


def kernel(x, c, ctx, c_ctx, w_ada, b_ada, norm_mix, norm_ffn, gla_w_in, gla_gate_w1, gla_gate_w2, gla_gate_b, gla_out_norm, gla_w_out, mla_w_down, mla_q_lora_norm, mla_kv_lora_norm, mla_w_uq, mla_w_ukv, mla_q_norm, mla_k_norm, mla_w_out, ffn_w_up, ffn_conv_w, ffn_conv_b, ffn_w_down):
    bsz, seq, d = x.shape
    lc = ctx.shape[1]
    depth = w_ada.shape[0]
    d_ff = ffn_w_down.shape[1]
    dqk = gla_gate_w2.shape[-1]
    rank = gla_gate_w1.shape[-1]
    dv = gla_out_norm.shape[-1]

    cond_rows = -(-(bsz + 1) // 8) * 8
    cond = jnp.zeros((cond_rows, d), F32).at[:bsz].set(c).at[bsz].set(c_ctx)
    mods = _ada_all(cond, w_ada, b_ada)

    cos_l, sin_l = _rope_tables(seq)
    cos_c = _pad_last(jnp.ones((lc, MLA_ROPE), F32), 128)
    sin_c = jnp.zeros((lc, 128), F32)

    tm_l = min(seq, 512)
    tm_c = min(lc, 512)
    xc = ctx
    for i in range(depth):
        last = i == depth - 1
        j = i // 2
        mod_l = mods[i, :bsz].reshape(bsz, 6, d)
        mod_c = jnp.broadcast_to(mods[i, bsz].reshape(1, 6, d), (bsz, 6, d))
        g_mix = norm_mix[i][None, :]
        if i % 2 == 0:
            w_in = gla_w_in[j].astype(BF16)
            w1 = _pad_last(jnp.concatenate([gla_gate_w1[j, 0], gla_gate_w1[j, 1]], axis=1),
                           GLA_RANK_PAD).astype(BF16)
            w2 = jnp.zeros((GLA_RANK_PAD, 2 * dqk), F32)
            w2 = w2.at[:rank, :dqk].set(gla_gate_w2[j, 0]).at[rank:2 * rank, dqk:].set(gla_gate_w2[j, 1])
            w2 = w2.astype(BF16)
            gb = gla_gate_b[j].reshape(1, 2 * dqk)
            gn = gla_out_norm[j][None, :]
            w_out = gla_w_out[j].astype(BF16)
            s0 = jnp.zeros((bsz, GLA_HEADS, dv, dqk // GLA_HEADS), F32)

            proj_c = _gla_proj(xc, mod_c, g_mix, w_in, w1, w2, gb, tm=tm_c)
            proj_l = _gla_proj(x, mod_l, g_mix, w_in, w1, w2, gb, tm=tm_l)
            vc, rc, fwd_c, bwd_c = proj_c[0], proj_c[1], proj_c[2:5], proj_c[5:8]
            v, r, fwd_l, bwd_l = proj_l[0], proj_l[1], proj_l[2:5], proj_l[5:8]
            if last:
                _, s_fwd = _gla_scan(*fwd_c[:2], vc, fwd_c[2], s0, direction=0, want_o=False)
                _, s_bwd = _gla_scan(*bwd_c[:2], vc, bwd_c[2], s0, direction=1, want_o=False)
            else:
                oc_b, s_bwd = _gla_scan(*bwd_c[:2], vc, bwd_c[2], s0, direction=1)
                xc, s_fwd = _gla_scan(*fwd_c[:2], vc, fwd_c[2], s0, direction=0,
                                      fused=(oc_b, rc, xc, mod_c, gn, w_out))
            o_b, _ = _gla_scan(*bwd_l[:2], v, bwd_l[2], s_bwd, direction=1)
            x, _ = _gla_scan(*fwd_l[:2], v, fwd_l[2], s_fwd, direction=0,
                             fused=(o_b, r, x, mod_l, gn, w_out))
        else:
            w = _mla_weights(mla_w_down[j], mla_q_lora_norm[j], mla_kv_lora_norm[j], mla_w_uq[j],
                             mla_w_ukv[j], mla_q_norm[j], mla_k_norm[j])
            w_out = mla_w_out[j].astype(BF16)
            if last:
                kc, vtc = _mla_proj(xc, mod_c, g_mix, cos_c, sin_c, w, want_q=False, tm=tm_c)
            else:
                qc, kc, vtc = _mla_proj(xc, mod_c, g_mix, cos_c, sin_c, w, want_q=True, tm=tm_c)
            q, k, vt = _mla_proj(x, mod_l, g_mix, cos_l, sin_l, w, want_q=True, tm=tm_l)
            ot = _latent_attention(w["use_shift"], q, kc, vtc, k, vt)
            x = _out_proj(ot, x, mod_l, w_out, tm=tm_l)
            if not last:
                otc = _attention(qc, kc, vtc, tq=min(lc, 512))
                xc = _out_proj(otc, xc, mod_c, w_out, tm=tm_c)

        g_ffn = norm_ffn[i][None, :]
        wa = ffn_w_up[i, :, :d_ff].astype(BF16)
        wg = ffn_w_up[i, :, d_ff:].astype(BF16)
        conv = jnp.concatenate([ffn_conv_w[i], ffn_conv_b[i][None, :]], axis=0)
        ca, cg = conv[:, :d_ff], conv[:, d_ff:]
        wd = ffn_w_down[i].astype(BF16)
        x = _conv_ffn(x, mod_l, g_ffn, wa, wg, ca, cg, wd, tm=tm_l, fc=FFN_COLS)
        if not last:
            xc = _conv_ffn(xc, mod_c, g_ffn, wa, wg, ca, cg, wd, tm=tm_c, fc=FFN_COLS)
    return x
```

```python
import functools

import jax
import jax.numpy as jnp
from jax import lax
from jax.experimental import pallas as pl
from jax.experimental.pallas import tpu as pltpu

F32 = jnp.float32
BF16 = jnp.bfloat16

LANES = 128
SUBLANES = 8
EPS = 1e-6
GRID_W = 64
ROPE_THETA = 10000.0
LOG2_E = 1.4426950408889634

GLA_HEADS = 4
GLA_GATE_NORMALIZER = 16.0
GLA_CHUNK = 128
GLA_RANK_PAD = 128

MLA_HEADS = 8
MLA_NOPE = 128
MLA_ROPE = 64
MLA_V = 128
MLA_QPAD = 256
MLA_PROJ_PARTS = 2
SHIFT_LANE = MLA_ROPE
ATTN_MAX_SHIFT = 50.0

FFN_COLS = 256
CONV_HALO = SUBLANES
VMEM_LIMIT = 56 * 1024 * 1024


def _cparams(*sem):
    return pltpu.CompilerParams(dimension_semantics=sem, vmem_limit_bytes=VMEM_LIMIT)


def _const_spec(shape):
    nd = len(shape)
    return pl.BlockSpec(shape, lambda *_: (0,) * nd, pipeline_mode=pl.Buffered(1))


def _dot(a, b):
    return jnp.dot(a, b, preferred_element_type=F32)


def _dot_nt(a, b):
    return lax.dot_general(a, b, (((1,), (1,)), ((), ())), preferred_element_type=F32)


def _dot_tn(a, b):
    return lax.dot_general(a, b, (((0,), (0,)), ((), ())), preferred_element_type=F32)


def _sigmoid(x):
    return 1.0 / (1.0 + jnp.exp(-x))


def _rms_scale(x, width):
    ss = jnp.sum(x * x, axis=-1, keepdims=True)
    return lax.rsqrt(ss * (1.0 / width) + EPS)


def _modulate(x, g, shift, scale):
    return x * _rms_scale(x, x.shape[-1]) * (g * (1.0 + scale)) + shift


def _ada_kernel(cond_ref, w_ref, b_ref, o_ref):
    cond = cond_ref[...]
    s = cond * _sigmoid(cond)
    o_ref[0] = jnp.dot(s, w_ref[0], preferred_element_type=F32,
                       precision=lax.Precision.HIGHEST) + b_ref[0]


def _ada_all(cond, w_ada, b_ada):
    depth, d, d6 = w_ada.shape
    rows = cond.shape[0]
    return pl.pallas_call(
        _ada_kernel,
        grid=(depth, d6 // d),
        in_specs=[
            pl.BlockSpec((rows, d), lambda i, j: (0, 0)),
            pl.BlockSpec((1, d, d), lambda i, j: (i, 0, j)),
            pl.BlockSpec((1, 1, d), lambda i, j: (i, 0, j)),
        ],
        out_specs=pl.BlockSpec((1, rows, d), lambda i, j: (i, 0, j)),
        out_shape=jax.ShapeDtypeStruct((depth, rows, d6), F32),
        compiler_params=_cparams("parallel", "parallel"),
        name="ada_mod",
    )(cond, w_ada, b_ada.reshape(depth, 1, d6))


def _gla_proj_kernel(x_ref, mod_ref, g_ref, win_ref, w1_ref, w2_ref, gb_ref,
                     v_ref, r_ref, qf_ref, kf_ref, df_ref, qb_ref, kb_ref, db_ref, *, dqk):
    mod = mod_ref[0]
    h = _modulate(x_ref[0], g_ref[...], mod[0:1], mod[1:2]).astype(BF16)
    c = GLA_CHUNK
    dk = dqk // GLA_HEADS
    dv = v_ref.shape[2]
    low = _dot(h, w1_ref[...]).astype(BF16)
    z = _dot(low, w2_ref[...]) + gb_ref[...]
    pqk = _dot(h, win_ref[:, :2 * dqk])
    unit = LOG2_E / GLA_GATE_NORMALIZER
    gates = jnp.minimum(z, 0.0) * unit - jnp.log(1.0 + jnp.exp(-jnp.abs(z))) * unit
    row = lax.broadcasted_iota(jnp.int32, (c, c), 0)
    col = lax.broadcasted_iota(jnp.int32, (c, c), 1)
    for direction, (q_ref, k_ref, d_ref) in enumerate(((qf_ref, kf_ref, df_ref), (qb_ref, kb_ref, db_ref))):
        reverse = direction == 1
        tri = ((col >= row) if reverse else (col <= row)).astype(BF16)
        tri2 = jnp.concatenate([tri, tri], axis=1)
        i_mid = c // 2 if reverse else c // 2 - 1
        i_end = 0 if reverse else c - 1
        g = gates[:, direction * dqk:(direction + 1) * dqk]
        g_hi = g.astype(BF16)
        g_lo = (g - g_hi.astype(F32)).astype(BF16)
        for ci in range(x_ref.shape[1] // c):
            rows = slice(ci * c, (ci + 1) * c)
            cum = _dot(tri2, jnp.concatenate([g_hi[rows], g_lo[rows]], axis=0))
            c_mid = cum[i_mid:i_mid + 1, :]
            c_end = cum[i_end:i_end + 1, :]
            rel = cum - c_mid
            q_in = pqk[rows, :dqk] * (jnp.exp2(rel) * (dk ** -0.5))
            k_in = pqk[rows, dqk:] * jnp.exp2(-rel)
            q_ref[0, rows, :dqk] = q_in.astype(BF16)
            q_ref[0, rows, dqk:] = (q_in * jnp.exp2(c_mid)).astype(BF16)
            k_ref[0, rows, :dqk] = k_in.astype(BF16)
            k_ref[0, rows, dqk:] = (k_in * jnp.exp2(c_end - c_mid)).astype(BF16)
            d_ref[0, ci * SUBLANES:(ci + 1) * SUBLANES, :] = jnp.broadcast_to(jnp.exp2(c_end), (SUBLANES, dqk))
    pvr = _dot(h, win_ref[:, 2 * dqk:])
    v_ref[0] = pvr[:, :dv].astype(BF16)
    r_ref[0] = pvr[:, dv:].astype(BF16)


def _gla_proj(x, mod, norm_g, w_in, w1, w2, gb, *, tm):
    b, l, d = x.shape
    dqk = w2.shape[1] // 2
    dv = (w_in.shape[1] - 2 * dqk) // 2
    drows = tm // GLA_CHUNK * SUBLANES
    row = lambda i, j: (i, j, 0)
    tok = lambda width, dtype: (pl.BlockSpec((1, tm, width), row), jax.ShapeDtypeStruct((b, l, width), dtype))
    dec = (pl.BlockSpec((1, drows, dqk), row),
           jax.ShapeDtypeStruct((b, l // GLA_CHUNK * SUBLANES, dqk), F32))
    outs = [tok(dv, BF16), tok(dv, BF16)] + 2 * [tok(2 * dqk, BF16), tok(2 * dqk, BF16), dec]
    return pl.pallas_call(
        functools.partial(_gla_proj_kernel, dqk=dqk),
        grid=(b, l // tm),
        in_specs=[
            pl.BlockSpec((1, tm, d), row),
            pl.BlockSpec((1, 6, d), lambda i, j: (i, 0, 0)),
            _const_spec(norm_g.shape),
            _const_spec(w_in.shape),
            _const_spec(w1.shape),
            _const_spec(w2.shape),
            _const_spec(gb.shape),
        ],
        out_specs=[o[0] for o in outs],
        out_shape=[o[1] for o in outs],
        compiler_params=_cparams("parallel", "parallel"),
        name="gla_proj",
    )(x, mod, norm_g, w_in, w1, w2, gb)


def _gla_scan_kernel(*refs, reverse, want_o, fuse_out, tblk):
    if fuse_out:
        (q_ref, k_ref, v_ref, d_ref, s0_ref, ob_ref, r_ref, x_ref, mod_ref, gn_ref, wout_ref,
         xo_ref, st_ref, y_scr) = refs
    elif want_o:
        q_ref, k_ref, v_ref, d_ref, s0_ref, o_ref, st_ref = refs
    else:
        q_ref, k_ref, v_ref, d_ref, s0_ref, st_ref = refs

    @pl.when(pl.program_id(1) == 0)
    def _():
        st_ref[...] = s0_ref[...]

    c = GLA_CHUNK
    dqk = q_ref.shape[2] // 2
    dk = dqk // GLA_HEADS
    dv = v_ref.shape[2] // GLA_HEADS
    row = lax.broadcasted_iota(jnp.int32, (c, c), 0)
    col = lax.broadcasted_iota(jnp.int32, (c, c), 1)
    mask = (col >= row) if reverse else (col <= row)

    nch = tblk // c
    states = [st_ref[0, h] for h in range(GLA_HEADS)]
    for ci in (reversed(range(nch)) if reverse else range(nch)):
        rows = slice(ci * c, (ci + 1) * c)
        for h in range(GLA_HEADS):
            intra = slice(h * dk, (h + 1) * dk)
            state = slice(dqk + h * dk, dqk + (h + 1) * dk)
            vs = slice(h * dv, (h + 1) * dv)
            v = v_ref[0, rows, vs]
            st = states[h]
            if want_o:
                scores = jnp.where(mask, _dot_nt(q_ref[0, rows, intra], k_ref[0, rows, intra]), 0.0)
                o = _dot(scores.astype(BF16), v) + _dot_nt(q_ref[0, rows, state], st.astype(BF16))
                if fuse_out:
                    o = o + ob_ref[0, rows, vs].astype(F32)
                    r = r_ref[0, rows, vs].astype(F32)
                    y = o * _rms_scale(o, dv) * gn_ref[...] * (r * _sigmoid(r))
                    y_scr[rows, vs] = y.astype(BF16)
                else:
                    o_ref[0, rows, vs] = o.astype(BF16)
            decay_end = d_ref[0, ci * SUBLANES:ci * SUBLANES + 1, intra]
            states[h] = st * decay_end + _dot_tn(v, k_ref[0, rows, state])
    for h in range(GLA_HEADS):
        st_ref[0, h] = states[h]

    if fuse_out:
        gate = mod_ref[0][2:3]
        xo_ref[0] = x_ref[0] + gate * _dot(y_scr[...], wout_ref[...])


def _gla_scan(q, k, v, decay, s0, *, direction, want_o=True, fused=None):
    b, l, dq2 = q.shape
    dvt = v.shape[2]
    tblk = min(l, 512)
    nblk = l // tblk
    reverse = direction == 1
    blk = (lambda i, j: (i, nblk - 1 - j, 0)) if reverse else (lambda i, j: (i, j, 0))
    st_spec = pl.BlockSpec((1,) + s0.shape[1:], lambda i, j: (i, 0, 0, 0))
    in_specs = [
        pl.BlockSpec((1, tblk, dq2), blk),
        pl.BlockSpec((1, tblk, dq2), blk),
        pl.BlockSpec((1, tblk, dvt), blk),
        pl.BlockSpec((1, tblk // GLA_CHUNK * SUBLANES, decay.shape[2]), blk),
        st_spec,
    ]
    args = [q, k, v, decay, s0]
    st_shape = jax.ShapeDtypeStruct(s0.shape, F32)
    scratch = []
    if fused is not None:
        o_other, r, x, mod, gn, w_out = fused
        d = x.shape[2]
        in_specs += [
            pl.BlockSpec((1, tblk, dvt), blk),
            pl.BlockSpec((1, tblk, dvt), blk),
            pl.BlockSpec((1, tblk, d), blk),
            pl.BlockSpec((1, 6, d), lambda i, j: (i, 0, 0)),
            _const_spec(gn.shape),
            _const_spec(w_out.shape),
        ]
        args += [o_other, r, x, mod, gn, w_out]
        out_specs = [pl.BlockSpec((1, tblk, d), blk), st_spec]
        out_shape = [jax.ShapeDtypeStruct(x.shape, F32), st_shape]
        scratch = [pltpu.VMEM((tblk, dvt), BF16)]
    elif want_o:
        out_specs = [pl.BlockSpec((1, tblk, dvt), blk), st_spec]
        out_shape = [jax.ShapeDtypeStruct((b, l, dvt), BF16), st_shape]
    else:
        out_specs = [st_spec]
        out_shape = [st_shape]
    outs = pl.pallas_call(
        functools.partial(_gla_scan_kernel, reverse=reverse, want_o=want_o,
                          fuse_out=fused is not None, tblk=tblk),
        grid=(b, nblk),
        in_specs=in_specs,
        out_specs=out_specs,
        out_shape=out_shape,
        scratch_shapes=scratch,
        compiler_params=_cparams("parallel", "arbitrary"),
        name="gla_scan_%s%s" % ("bwd" if reverse else "fwd", "_out" if fused is not None else ""),
    )(*args)
    if want_o:
        return outs[0], outs[1]
    return None, outs[0]


def _mla_proj_kernel(*refs, want_q):
    (x_ref, mod_ref, g_ref, cos_ref, sin_ref, wdq_ref, wdkv_ref, wdpe_ref, qln_ref, kvln_ref,
     wuq_ref, wuqs_ref, wuk_ref, wuvt_ref, qn_ref, kn_ref) = refs[:16]
    if want_q:
        q_ref, k_ref, vt_ref = refs[16:]
    else:
        k_ref, vt_ref = refs[16:]
    mod = mod_ref[0]
    qn = qn_ref[...]
    kn = kn_ref[...]
    q_scale = (MLA_NOPE + MLA_ROPE) ** -0.5 * LOG2_E
    sub = x_ref.shape[1] // MLA_PROJ_PARTS
    parts = [slice(i * sub, (i + 1) * sub) for i in range(MLA_PROJ_PARTS)]

    def rope_part(raw, raw_partner, gains, rows):
        return ((raw * (cos_ref[rows, :] * gains[1:2]) + raw_partner * (sin_ref[rows, :] * gains[2:3]))
                * _rms_scale(raw, MLA_ROPE))

    def normed(c, gain_ref):
        return (c * _rms_scale(c, c.shape[-1]) * gain_ref[...]).astype(BF16)

    hs = [_modulate(x_ref[0, r, :], g_ref[...], mod[0:1], mod[1:2]).astype(BF16) for r in parts]
    c_kv = [_dot(h, wdkv_ref[...]) for h in hs]
    kpe = [_dot(h, wdpe_ref[...]) for h in hs]
    if want_q:
        c_q = [_dot(h, wdq_ref[...]) for h in hs]
    c_kv = [normed(c, kvln_ref) for c in c_kv]
    for c, r in zip(c_kv, parts):
        vt_ref[0, :, r] = _dot_nt(wuvt_ref[...], c).astype(BF16)
    kn_all = [_dot(c, wuk_ref[...]) for c in c_kv]
    k_rope = [(rope_part(p[:, :128], p[:, 128:], kn, r) + kn[3:4]).astype(BF16) for p, r in zip(kpe, parts)]
    if want_q:
        c_q = [normed(c, qln_ref) for c in c_q]
        qm = [_dot(c, wuq_ref[...]) for c in c_q]
        qs = [_dot(c, wuqs_ref[...]) for c in c_q]
    for i, r in enumerate(parts):
        for hd in range(MLA_HEADS):
            k_nope = kn_all[i][:, hd * MLA_NOPE:(hd + 1) * MLA_NOPE]
            k_ref[0, hd, r, :MLA_NOPE] = (k_nope * _rms_scale(k_nope, MLA_NOPE) * kn[0:1]).astype(BF16)
            k_ref[0, hd, r, MLA_NOPE:] = k_rope[i]
            if want_q:
                base = hd * MLA_QPAD
                q_nope = qm[i][:, base:base + MLA_NOPE]
                q_nope = q_nope * _rms_scale(q_nope, MLA_NOPE) * (qn[0:1] * q_scale)
                q_ref[0, hd, r, :MLA_NOPE] = q_nope.astype(BF16)
                q_rope = rope_part(qm[i][:, base + MLA_NOPE:base + MLA_QPAD],
                                   qs[i][:, hd * 128:(hd + 1) * 128], qn, r) * q_scale + qn[3:4]
                q_ref[0, hd, r, MLA_NOPE:] = q_rope.astype(BF16)


def _mla_proj(x, mod, norm_g, cos, sin, w, *, want_q, tm):
    b, l, d = x.shape
    consts = [w["wdq"], w["wdkv"], w["wdpe"], w["qln"], w["kvln"], w["wuq"], w["wuqs"], w["wuk"],
              w["wuvt"], w["qn"], w["kn"]]
    hspec = pl.BlockSpec((1, MLA_HEADS, tm, MLA_QPAD), lambda i, j: (i, 0, j, 0))
    hshape = jax.ShapeDtypeStruct((b, MLA_HEADS, l, MLA_QPAD), BF16)
    out_specs = [hspec, pl.BlockSpec((1, MLA_HEADS * MLA_V, tm), lambda i, j: (i, 0, j))]
    out_shape = [hshape, jax.ShapeDtypeStruct((b, MLA_HEADS * MLA_V, l), BF16)]
    if want_q:
        out_specs = [hspec] + out_specs
        out_shape = [hshape] + out_shape
    return pl.pallas_call(
        functools.partial(_mla_proj_kernel, want_q=want_q),
        grid=(b, l // tm),
        in_specs=[
            pl.BlockSpec((1, tm, d), lambda i, j: (i, j, 0)),
            pl.BlockSpec((1, 6, d), lambda i, j: (i, 0, 0)),
            _const_spec(norm_g.shape),
            pl.BlockSpec((tm, 128), lambda i, j: (j, 0)),
            pl.BlockSpec((tm, 128), lambda i, j: (j, 0)),
        ] + [_const_spec(a.shape) for a in consts],
        out_specs=out_specs,
        out_shape=out_shape,
        compiler_params=_cparams("parallel", "parallel"),
        name="mla_proj_q" if want_q else "mla_proj_kv",
    )(x, mod, norm_g, cos, sin, *consts)


def _attn_kernel(*refs, with_latent, tk):
    if with_latent:
        q_ref, kc_ref, vtc_ref, k_ref, vt_ref, ot_ref = refs
    else:
        q_ref, kc_ref, vtc_ref, ot_ref = refs
    q = q_ref[0, 0]
    chunks = [(kc_ref, vtc_ref, 0, kc_ref.shape[2])]
    if with_latent:
        chunks += [(k_ref, vt_ref, j * tk, tk) for j in range(k_ref.shape[2] // tk)]

    def scores(idx):
        kr, _, start, size = chunks[idx]
        return _dot_nt(kr[0, 0, start:start + size, :], q)

    s_next = scores(0)
    m = denom = acc = None
    for idx, (_, vr, start, size) in enumerate(chunks):
        s = s_next
        if idx + 1 < len(chunks):
            s_next = scores(idx + 1)
        vt = vr[0, :, start:start + size]
        cmax = jnp.max(s, axis=0, keepdims=True)
        if m is None:
            m = cmax
            p = jnp.exp2(s - m)
            denom = jnp.sum(p, axis=0, keepdims=True)
            acc = _dot(vt, p.astype(BF16))
        else:
            m_new = jnp.maximum(m, cmax)
            alpha = jnp.exp2(m - m_new)
            p = jnp.exp2(s - m_new)
            denom = denom * alpha + jnp.sum(p, axis=0, keepdims=True)
            acc = acc * alpha + _dot(vt, p.astype(BF16))
            m = m_new
    ot_ref[0] = (acc * (1.0 / denom)).astype(BF16)


def _attn_shifted_kernel(q_ref, kc_ref, vtc_ref, k_ref, vt_ref, ot_ref, *, tk):
    q = q_ref[0, 0]
    chunks = [(kc_ref, vtc_ref, 0, kc_ref.shape[2])]
    chunks += [(k_ref, vt_ref, j * tk, tk) for j in range(k_ref.shape[2] // tk)]

    def scores(idx):
        kr, _, start, size = chunks[idx]
        return _dot_nt(kr[0, 0, start:start + size, :], q)

    s_next = scores(0)
    acc = denom8 = None
    for idx, (_, vr, start, size) in enumerate(chunks):
        s = s_next
        if idx + 1 < len(chunks):
            s_next = scores(idx + 1)
        p = jnp.exp2(s)
        part = jnp.sum(p.reshape(size // SUBLANES, SUBLANES, p.shape[1]), axis=0)
        y = _dot(vr[0, :, start:start + size], p.astype(BF16))
        acc = y if acc is None else acc + y
        denom8 = part if denom8 is None else denom8 + part
    denom = jnp.sum(denom8, axis=0, keepdims=True)
    ot_ref[0] = (acc * (1.0 / denom)).astype(BF16)


def _attention(q, kc, vtc, k=None, vt=None, *, tq, tk=512, shifted=False):
    b, nh, lq, dq = q.shape
    lc = kc.shape[2]
    dv = vtc.shape[1] // nh
    with_latent = k is not None
    in_specs = [
        pl.BlockSpec((1, 1, tq, dq), lambda i, h, j: (i, h, j, 0)),
        pl.BlockSpec((1, 1, lc, dq), lambda i, h, j: (i, h, 0, 0)),
        pl.BlockSpec((1, dv, lc), lambda i, h, j: (i, h, 0)),
    ]
    args = [q, kc, vtc]
    if with_latent:
        lk = k.shape[2]
        in_specs += [pl.BlockSpec((1, 1, lk, dq), lambda i, h, j: (i, h, 0, 0)),
                     pl.BlockSpec((1, dv, lk), lambda i, h, j: (i, h, 0))]
        args += [k, vt]
    if shifted:
        body = functools.partial(_attn_shifted_kernel, tk=tk)
        name = "mla_attn_shifted"
    else:
        body = functools.partial(_attn_kernel, with_latent=with_latent, tk=tk)
        name = "mla_attn" if with_latent else "mla_attn_ctx"
    return pl.pallas_call(
        body,
        grid=(b, nh, lq // tq),
        in_specs=in_specs,
        out_specs=pl.BlockSpec((1, dv, tq), lambda i, h, j: (i, h, j)),
        out_shape=jax.ShapeDtypeStruct((b, nh * dv, lq), BF16),
        compiler_params=_cparams("parallel", "parallel", "parallel"),
        name=name,
    )(*args)


def _latent_attention(use_shift, q, kc, vtc, k, vt):
    lq = q.shape[2]
    lk = k.shape[2]
    fast = functools.partial(_attention, tq=min(lq, 2048), tk=min(lk, 2048), shifted=True)
    exact = functools.partial(_attention, tq=min(lq, 512), tk=min(lk, 512))
    return lax.cond(use_shift, fast, exact, q, kc, vtc, k, vt)


def _out_proj_kernel(ot_ref, x_ref, mod_ref, w_ref, xo_ref):
    gate = mod_ref[0][2:3]
    xo_ref[0] = x_ref[0] + gate * _dot_tn(ot_ref[0], w_ref[...])


def _out_proj(ot, x, mod, w_out, *, tm):
    b, l, d = x.shape
    row = lambda i, j: (i, j, 0)
    return pl.pallas_call(
        _out_proj_kernel,
        grid=(b, l // tm),
        in_specs=[
            pl.BlockSpec((1, ot.shape[1], tm), lambda i, j: (i, 0, j)),
            pl.BlockSpec((1, tm, d), row),
            pl.BlockSpec((1, 6, d), lambda i, j: (i, 0, 0)),
            _const_spec(w_out.shape),
        ],
        out_specs=pl.BlockSpec((1, tm, d), row),
        out_shape=jax.ShapeDtypeStruct(x.shape, F32),
        compiler_params=_cparams("parallel", "parallel"),
        name="mla_out",
    )(ot, x, mod, w_out)


def _ffn_kernel(xm_ref, xp_ref, xn_ref, mod_ref, g_ref, wa_ref, wg_ref, ca_ref, cg_ref, wd_ref,
                xo_ref, h_scr, ua_scr, ug_scr, act_scr, *, tm, fc):
    i = pl.program_id(1)
    mod = mod_ref[0]
    g = g_ref[...]
    hal = CONV_HALO

    def hidden(x):
        return _modulate(x, g, mod[3:4], mod[4:5])

    h_scr[0:tm] = hidden(xm_ref[0]).astype(BF16)
    nxt = jnp.where(i == pl.num_programs(1) - 1, 0.0, hidden(xn_ref[0]))
    prv = jnp.where(i == 0, 0.0, hidden(xp_ref[0]))
    h_scr[tm:] = jnp.concatenate([nxt, prv], axis=0).astype(BF16)

    def put(u_scr, slab, u):
        u_scr[slab, hal:hal + tm, :] = u[0:tm]
        u_scr[slab, hal + tm:, :] = u[tm:tm + hal]
        u_scr[slab, 0:hal, :] = u[tm + hal:]

    def conv(u_scr, slab, cw):
        return (u_scr[slab, hal - 1:hal - 1 + tm, :] * cw[0:1] + u_scr[slab, hal:hal + tm, :] * cw[1:2]
                + u_scr[slab, hal + 1:hal + 1 + tm, :] * cw[2:3] + cw[3:4])

    per = fc // LANES
    for ci in range(wa_ref.shape[1] // fc):
        cols = slice(ci * fc, (ci + 1) * fc)
        ua = _dot(h_scr[...], wa_ref[:, cols])
        ug = _dot(h_scr[...], wg_ref[:, cols])
        for s in range(per):
            put(ua_scr, ci * per + s, ua[:, s * LANES:(s + 1) * LANES])
            put(ug_scr, ci * per + s, ug[:, s * LANES:(s + 1) * LANES])
        for s in range(per):
            slab = ci * per + s
            lanes = slice(slab * LANES, (slab + 1) * LANES)
            a = conv(ua_scr, slab, ca_ref[:, lanes])
            gt = conv(ug_scr, slab, cg_ref[:, lanes])
            act_scr[:, lanes] = (gt * _sigmoid(gt) * a).astype(BF16)
    xo_ref[0] = xm_ref[0] + mod[5:6] * _dot(act_scr[...], wd_ref[...])


def _conv_ffn(x, mod, norm_g, wa, wg, ca, cg, wd, *, tm, fc):
    b, l, d = x.shape
    hal = CONV_HALO
    nt = l // tm
    per = tm // hal
    last = l // hal - 1
    d_ff = wa.shape[1]
    return pl.pallas_call(
        functools.partial(_ffn_kernel, tm=tm, fc=fc),
        grid=(b, nt),
        in_specs=[
            pl.BlockSpec((1, tm, d), lambda i, j: (i, j, 0)),
            pl.BlockSpec((1, hal, d), lambda i, j: (i, jnp.maximum(j * per - 1, 0), 0)),
            pl.BlockSpec((1, hal, d), lambda i, j: (i, jnp.minimum((j + 1) * per, last), 0)),
            pl.BlockSpec((1, 6, d), lambda i, j: (i, 0, 0)),
            _const_spec(norm_g.shape),
            _const_spec(wa.shape),
            _const_spec(wg.shape),
            _const_spec(ca.shape),
            _const_spec(cg.shape),
            _const_spec(wd.shape),
        ],
        out_specs=pl.BlockSpec((1, tm, d), lambda i, j: (i, j, 0)),
        out_shape=jax.ShapeDtypeStruct(x.shape, F32),
        scratch_shapes=[
            pltpu.VMEM((tm + 2 * hal, d), BF16),
            pltpu.VMEM((d_ff // LANES, tm + 2 * hal, LANES), F32),
            pltpu.VMEM((d_ff // LANES, tm + 2 * hal, LANES), F32),
            pltpu.VMEM((tm, d_ff), BF16),
        ],
        compiler_params=_cparams("parallel", "parallel"),
        name="conv_ffn",
    )(x, x, x, mod, norm_g, wa, wg, ca, cg, wd)


def _rope_tables(length):
    nf = MLA_ROPE // 4
    t = jnp.arange(length)
    pos = jnp.stack([(t // GRID_W).astype(F32), (t % GRID_W).astype(F32)], axis=1)
    inv = ROPE_THETA ** (-jnp.arange(nf, dtype=F32) / nf)
    ang = pos[:, :, None] * inv
    cos = jnp.cos(ang)[:, :, None, :] * jnp.ones((1, 1, 2, 1), F32)
    sin = jnp.sin(ang)[:, :, None, :] * jnp.array([-1.0, 1.0], F32)[None, None, :, None]
    pad = lambda a: jnp.pad(a.reshape(length, MLA_ROPE), ((0, 0), (0, 128 - MLA_ROPE)))
    return pad(cos), pad(sin)


def _rope_partner(a):
    nf = MLA_ROPE // 4
    r = a.reshape(a.shape[:-1] + (2, 2, nf))
    return jnp.flip(r, axis=-2).reshape(a.shape)


def _pad_last(a, width):
    return jnp.pad(a, [(0, 0)] * (a.ndim - 1) + [(0, width - a.shape[-1])])


def _mla_weights(w_down, q_lora_norm, kv_lora_norm, w_uq, w_ukv, q_norm, k_norm):
    q_rank = q_lora_norm.shape[0]
    kv_rank = kv_lora_norm.shape[0]
    qk = MLA_NOPE + MLA_ROPE
    w_pe = w_down[:, q_rank + kv_rank:]
    wuq = w_uq.reshape(q_rank, MLA_HEADS, qk)
    wuq_rope = wuq[..., MLA_NOPE:]
    wukv = w_ukv.reshape(kv_rank, MLA_HEADS, MLA_NOPE + MLA_V)

    def max_sq_norm(g):
        return MLA_NOPE * jnp.max(g[:MLA_NOPE] ** 2) + MLA_ROPE * jnp.max(g[MLA_NOPE:] ** 2)

    bound = jnp.sqrt(max_sq_norm(q_norm) * max_sq_norm(k_norm)) * (qk ** -0.5 * LOG2_E * 1.02)
    use_shift = bound <= ATTN_MAX_SHIFT
    shift = jnp.where(use_shift, bound, 0.0)
    lane = (jnp.arange(128) == SHIFT_LANE).astype(F32)

    def gains(g, shift_value):
        rope = g[MLA_NOPE:]
        return jnp.stack([g[:MLA_NOPE], _pad_last(rope, 128), _pad_last(_rope_partner(rope), 128),
                          lane * shift_value])

    return {
        "use_shift": use_shift,
        "wdq": w_down[:, :q_rank].astype(BF16),
        "wdkv": w_down[:, q_rank:q_rank + kv_rank].astype(BF16),
        "wdpe": jnp.concatenate([_pad_last(w_pe, 128), _pad_last(_rope_partner(w_pe), 128)],
                                axis=1).astype(BF16),
        "qln": q_lora_norm[None, :],
        "kvln": kv_lora_norm[None, :],
        "wuq": _pad_last(wuq, MLA_QPAD).reshape(q_rank, MLA_HEADS * MLA_QPAD).astype(BF16),
        "wuqs": _pad_last(_rope_partner(wuq_rope), 128).reshape(q_rank, MLA_HEADS * 128).astype(BF16),
        "wuk": wukv[..., :MLA_NOPE].reshape(kv_rank, MLA_HEADS * MLA_NOPE).astype(BF16),
        "wuvt": wukv[..., MLA_NOPE:].reshape(kv_rank, MLA_HEADS * MLA_V).T.astype(BF16),
        "qn": gains(q_norm, shift),
        "kn": gains(k_norm, jnp.where(use_shift, -1.0, 0.0)),
    }


def kernel(x, c, ctx, c_ctx, w_ada, b_ada, norm_mix, norm_ffn, gla_w_in, gla_gate_w1, gla_gate_w2, gla_gate_b, gla_out_norm, gla_w_out, mla_w_down, mla_q_lora_norm, mla_kv_lora_norm, mla_w_uq, mla_w_ukv, mla_q_norm, mla_k_norm, mla_w_out, ffn_w_up, ffn_conv_w, ffn_conv_b, ffn_w_down):
    bsz, seq, d = x.shape
    lc = ctx.shape[1]
    depth = w_ada.shape[0]
    d_ff = ffn_w_down.shape[1]
    dqk = gla_gate_w2.shape[-1]
    rank = gla_gate_w1.shape[-1]
    dv = gla_out_norm.shape[-1]

    cond_rows = -(-(bsz + 1) // 8) * 8
    cond = jnp.zeros((cond_rows, d), F32).at[:bsz].set(c).at[bsz].set(c_ctx)
    mods = _ada_all(cond, w_ada, b_ada)

    cos_l, sin_l = _rope_tables(seq)
    cos_c = _pad_last(jnp.ones((lc, MLA_ROPE), F32), 128)
    sin_c = jnp.zeros((lc, 128), F32)

    tm_l = min(seq, 512)
    tm_c = min(lc, 512)
    xc = ctx
    for i in range(depth):
        last = i == depth - 1
        j = i // 2
        mod_l = mods[i, :bsz].reshape(bsz, 6, d)
        mod_c = jnp.broadcast_to(mods[i, bsz].reshape(1, 6, d), (bsz, 6, d))
        g_mix = norm_mix[i][None, :]
        if i % 2 == 0:
            w_in = gla_w_in[j].astype(BF16)
            w1 = _pad_last(jnp.concatenate([gla_gate_w1[j, 0], gla_gate_w1[j, 1]], axis=1),
                           GLA_RANK_PAD).astype(BF16)
            w2 = jnp.zeros((GLA_RANK_PAD, 2 * dqk), F32)
            w2 = w2.at[:rank, :dqk].set(gla_gate_w2[j, 0]).at[rank:2 * rank, dqk:].set(gla_gate_w2[j, 1])
            w2 = w2.astype(BF16)
            gb = gla_gate_b[j].reshape(1, 2 * dqk)
            gn = gla_out_norm[j][None, :]
            w_out = gla_w_out[j].astype(BF16)
            s0 = jnp.zeros((bsz, GLA_HEADS, dv, dqk // GLA_HEADS), F32)

            proj_c = _gla_proj(xc, mod_c, g_mix, w_in, w1, w2, gb, tm=tm_c)
            proj_l = _gla_proj(x, mod_l, g_mix, w_in, w1, w2, gb, tm=tm_l)
            vc, rc, fwd_c, bwd_c = proj_c[0], proj_c[1], proj_c[2:5], proj_c[5:8]
            v, r, fwd_l, bwd_l = proj_l[0], proj_l[1], proj_l[2:5], proj_l[5:8]
            if last:
                _, s_fwd = _gla_scan(*fwd_c[:2], vc, fwd_c[2], s0, direction=0, want_o=False)
                _, s_bwd = _gla_scan(*bwd_c[:2], vc, bwd_c[2], s0, direction=1, want_o=False)
            else:
                oc_b, s_bwd = _gla_scan(*bwd_c[:2], vc, bwd_c[2], s0, direction=1)
                xc, s_fwd = _gla_scan(*fwd_c[:2], vc, fwd_c[2], s0, direction=0,
                                      fused=(oc_b, rc, xc, mod_c, gn, w_out))
            o_b, _ = _gla_scan(*bwd_l[:2], v, bwd_l[2], s_bwd, direction=1)
            x, _ = _gla_scan(*fwd_l[:2], v, fwd_l[2], s_fwd, direction=0,
                             fused=(o_b, r, x, mod_l, gn, w_out))
        else:
            w = _mla_weights(mla_w_down[j], mla_q_lora_norm[j], mla_kv_lora_norm[j], mla_w_uq[j],
                             mla_w_ukv[j], mla_q_norm[j], mla_k_norm[j])
            w_out = mla_w_out[j].astype(BF16)
            if last:
                kc, vtc = _mla_proj(xc, mod_c, g_mix, cos_c, sin_c, w, want_q=False, tm=tm_c)
            else:
                qc, kc, vtc = _mla_proj(xc, mod_c, g_mix, cos_c, sin_c, w, want_q=True, tm=tm_c)
            q, k, vt = _mla_proj(x, mod_l, g_mix, cos_l, sin_l, w, want_q=True, tm=tm_l)
            ot = _latent_attention(w["use_shift"], q, kc, vtc, k, vt)
            x = _out_proj(ot, x, mod_l, w_out, tm=tm_l)
            if not last:
                otc = _attention(qc, kc, vtc, tq=min(lc, 512))
                xc = _out_proj(otc, xc, mod_c, w_out, tm=tm_c)

        g_ffn = norm_ffn[i][None, :]
        wa = ffn_w_up[i, :, :d_ff].astype(BF16)
        wg = ffn_w_up[i, :, d_ff:].astype(BF16)
        conv = jnp.concatenate([ffn_conv_w[i], ffn_conv_b[i][None, :]], axis=0)
        ca, cg = conv[:, :d_ff], conv[:, d_ff:]
        wd = ffn_w_down[i].astype(BF16)
        x = _conv_ffn(x, mod_l, g_ffn, wa, wg, ca, cg, wd, tm=tm_l, fc=FFN_COLS)
        if not last:
            xc = _conv_ffn(xc, mod_c, g_ffn, wa, wg, ca, cg, wd, tm=tm_c, fc=FFN_COLS)
    return x
```

```python
import functools

import jax
import jax.numpy as jnp
from jax import lax
from jax.experimental import pallas as pl
from jax.experimental.pallas import tpu as pltpu

F32 = jnp.float32
BF16 = jnp.bfloat16

LANES = 128
SUBLANES = 8
EPS = 1e-6
GRID_W = 64
ROPE_THETA = 10000.0
LOG2_E = 1.4426950408889634

GLA_HEADS = 4
GLA_GATE_NORMALIZER = 16.0
GLA_CHUNK = 128
GLA_RANK_PAD = 128

MLA_HEADS = 8
MLA_NOPE = 128
MLA_ROPE = 64
MLA_V = 128
MLA_QPAD = 256
MLA_PROJ_ROWS = 256
SHIFT_LANE = MLA_ROPE
ATTN_MAX_SHIFT = 50.0

FFN_COLS = 256
CONV_HALO = SUBLANES
VMEM_LIMIT = 56 * 1024 * 1024


def _cparams(*sem):
    return pltpu.CompilerParams(dimension_semantics=sem, vmem_limit_bytes=VMEM_LIMIT)


def _const_spec(shape):
    nd = len(shape)
    return pl.BlockSpec(shape, lambda *_: (0,) * nd, pipeline_mode=pl.Buffered(1))


def _dot(a, b):
    return jnp.dot(a, b, preferred_element_type=F32)


def _dot_nt(a, b):
    return lax.dot_general(a, b, (((1,), (1,)), ((), ())), preferred_element_type=F32)


def _dot_tn(a, b):
    return lax.dot_general(a, b, (((0,), (0,)), ((), ())), preferred_element_type=F32)


def _sigmoid(x):
    return 1.0 / (1.0 + jnp.exp(-x))


def _rms_scale(x, width):
    ss = jnp.sum(x * x, axis=-1, keepdims=True)
    return lax.rsqrt(ss * (1.0 / width) + EPS)


def _modulate(x, g, shift, scale):
    return x * _rms_scale(x, x.shape[-1]) * (g * (1.0 + scale)) + shift


def _ada_kernel(cond_ref, w_ref, b_ref, o_ref):
    cond = cond_ref[...]
    s = cond * _sigmoid(cond)
    o_ref[0] = jnp.dot(s, w_ref[0], preferred_element_type=F32,
                       precision=lax.Precision.HIGHEST) + b_ref[0]


def _ada_all(cond, w_ada, b_ada):
    depth, d, d6 = w_ada.shape
    rows = cond.shape[0]
    return pl.pallas_call(
        _ada_kernel,
        grid=(depth, d6 // d),
        in_specs=[
            pl.BlockSpec((rows, d), lambda i, j: (0, 0)),
            pl.BlockSpec((1, d, d), lambda i, j: (i, 0, j)),
            pl.BlockSpec((1, 1, d), lambda i, j: (i, 0, j)),
        ],
        out_specs=pl.BlockSpec((1, rows, d), lambda i, j: (i, 0, j)),
        out_shape=jax.ShapeDtypeStruct((depth, rows, d6), F32),
        compiler_params=_cparams("parallel", "parallel"),
        name="ada_mod",
    )(cond, w_ada, b_ada.reshape(depth, 1, d6))


def _gla_proj_kernel(x_ref, mod_ref, g_ref, win_ref, w1_ref, w2_ref, gb_ref,
                     v_ref, r_ref, qf_ref, kf_ref, df_ref, qb_ref, kb_ref, db_ref, *, dqk):
    mod = mod_ref[0]
    h = _modulate(x_ref[0], g_ref[...], mod[0:1], mod[1:2]).astype(BF16)
    c = GLA_CHUNK
    dk = dqk // GLA_HEADS
    dv = v_ref.shape[2]
    low = _dot(h, w1_ref[...]).astype(BF16)
    z = _dot(low, w2_ref[...]) + gb_ref[...]
    pqk = _dot(h, win_ref[:, :2 * dqk])
    unit = LOG2_E / GLA_GATE_NORMALIZER
    gates = jnp.minimum(z, 0.0) * unit - jnp.log(1.0 + jnp.exp(-jnp.abs(z))) * unit
    row = lax.broadcasted_iota(jnp.int32, (c, c), 0)
    col = lax.broadcasted_iota(jnp.int32, (c, c), 1)
    for direction, (q_ref, k_ref, d_ref) in enumerate(((qf_ref, kf_ref, df_ref), (qb_ref, kb_ref, db_ref))):
        reverse = direction == 1
        tri = ((col >= row) if reverse else (col <= row)).astype(BF16)
        tri2 = jnp.concatenate([tri, tri], axis=1)
        i_mid = c // 2 if reverse else c // 2 - 1
        i_end = 0 if reverse else c - 1
        g = gates[:, direction * dqk:(direction + 1) * dqk]
        g_hi = g.astype(BF16)
        g_lo = (g - g_hi.astype(F32)).astype(BF16)
        for ci in range(x_ref.shape[1] // c):
            rows = slice(ci * c, (ci + 1) * c)
            cum = _dot(tri2, jnp.concatenate([g_hi[rows], g_lo[rows]], axis=0))
            c_mid = cum[i_mid:i_mid + 1, :]
            c_end = cum[i_end:i_end + 1, :]
            rel = cum - c_mid
            q_in = pqk[rows, :dqk] * (jnp.exp2(rel) * (dk ** -0.5))
            k_in = pqk[rows, dqk:] * jnp.exp2(-rel)
            q_ref[0, rows, :dqk] = q_in.astype(BF16)
            q_ref[0, rows, dqk:] = (q_in * jnp.exp2(c_mid)).astype(BF16)
            k_ref[0, rows, :dqk] = k_in.astype(BF16)
            k_ref[0, rows, dqk:] = (k_in * jnp.exp2(c_end - c_mid)).astype(BF16)
            d_ref[0, ci * SUBLANES:(ci + 1) * SUBLANES, :] = jnp.broadcast_to(jnp.exp2(c_end), (SUBLANES, dqk))
    pvr = _dot(h, win_ref[:, 2 * dqk:])
    v_ref[0] = pvr[:, :dv].astype(BF16)
    r_ref[0] = pvr[:, dv:].astype(BF16)


def _gla_proj(x, mod, norm_g, w_in, w1, w2, gb, *, tm):
    b, l, d = x.shape
    dqk = w2.shape[1] // 2
    dv = (w_in.shape[1] - 2 * dqk) // 2
    drows = tm // GLA_CHUNK * SUBLANES
    row = lambda i, j: (i, j, 0)
    tok = lambda width, dtype: (pl.BlockSpec((1, tm, width), row), jax.ShapeDtypeStruct((b, l, width), dtype))
    dec = (pl.BlockSpec((1, drows, dqk), row),
           jax.ShapeDtypeStruct((b, l // GLA_CHUNK * SUBLANES, dqk), F32))
    outs = [tok(dv, BF16), tok(dv, BF16)] + 2 * [tok(2 * dqk, BF16), tok(2 * dqk, BF16), dec]
    return pl.pallas_call(
        functools.partial(_gla_proj_kernel, dqk=dqk),
        grid=(b, l // tm),
        in_specs=[
            pl.BlockSpec((1, tm, d), row),
            pl.BlockSpec((1, 6, d), lambda i, j: (i, 0, 0)),
            _const_spec(norm_g.shape),
            _const_spec(w_in.shape),
            _const_spec(w1.shape),
            _const_spec(w2.shape),
            _const_spec(gb.shape),
        ],
        out_specs=[o[0] for o in outs],
        out_shape=[o[1] for o in outs],
        compiler_params=_cparams("parallel", "parallel"),
        name="gla_proj",
    )(x, mod, norm_g, w_in, w1, w2, gb)


def _gla_scan_kernel(*refs, reverse, want_o, fuse_out, tblk):
    if fuse_out:
        (q_ref, k_ref, v_ref, d_ref, s0_ref, ob_ref, r_ref, x_ref, mod_ref, gn_ref, wout_ref,
         xo_ref, st_ref, y_scr) = refs
    elif want_o:
        q_ref, k_ref, v_ref, d_ref, s0_ref, o_ref, st_ref = refs
    else:
        q_ref, k_ref, v_ref, d_ref, s0_ref, st_ref = refs

    @pl.when(pl.program_id(1) == 0)
    def _():
        st_ref[...] = s0_ref[...]

    c = GLA_CHUNK
    dqk = q_ref.shape[2] // 2
    dk = dqk // GLA_HEADS
    dv = v_ref.shape[2] // GLA_HEADS
    row = lax.broadcasted_iota(jnp.int32, (c, c), 0)
    col = lax.broadcasted_iota(jnp.int32, (c, c), 1)
    mask = (col >= row) if reverse else (col <= row)

    nch = tblk // c
    states = [st_ref[0, h] for h in range(GLA_HEADS)]
    for ci in (reversed(range(nch)) if reverse else range(nch)):
        rows = slice(ci * c, (ci + 1) * c)
        for h in range(GLA_HEADS):
            intra = slice(h * dk, (h + 1) * dk)
            state = slice(dqk + h * dk, dqk + (h + 1) * dk)
            vs = slice(h * dv, (h + 1) * dv)
            v = v_ref[0, rows, vs]
            st = states[h]
            if want_o:
                scores = jnp.where(mask, _dot_nt(q_ref[0, rows, intra], k_ref[0, rows, intra]), 0.0)
                o = _dot(scores.astype(BF16), v) + _dot_nt(q_ref[0, rows, state], st.astype(BF16))
                if fuse_out:
                    o = o + ob_ref[0, rows, vs].astype(F32)
                    r = r_ref[0, rows, vs].astype(F32)
                    y = o * _rms_scale(o, dv) * gn_ref[...] * (r * _sigmoid(r))
                    y_scr[rows, vs] = y.astype(BF16)
                else:
                    o_ref[0, rows, vs] = o.astype(BF16)
            decay_end = d_ref[0, ci * SUBLANES:ci * SUBLANES + 1, intra]
            states[h] = st * decay_end + _dot_tn(v, k_ref[0, rows, state])
    for h in range(GLA_HEADS):
        st_ref[0, h] = states[h]

    if fuse_out:
        gate = mod_ref[0][2:3]
        xo_ref[0] = x_ref[0] + gate * _dot(y_scr[...], wout_ref[...])


def _gla_scan(q, k, v, decay, s0, *, direction, want_o=True, fused=None):
    b, l, dq2 = q.shape
    dvt = v.shape[2]
    tblk = min(l, 1024)
    nblk = l // tblk
    reverse = direction == 1
    blk = (lambda i, j: (i, nblk - 1 - j, 0)) if reverse else (lambda i, j: (i, j, 0))
    st_spec = pl.BlockSpec((1,) + s0.shape[1:], lambda i, j: (i, 0, 0, 0))
    in_specs = [
        pl.BlockSpec((1, tblk, dq2), blk),
        pl.BlockSpec((1, tblk, dq2), blk),
        pl.BlockSpec((1, tblk, dvt), blk),
        pl.BlockSpec((1, tblk // GLA_CHUNK * SUBLANES, decay.shape[2]), blk),
        st_spec,
    ]
    args = [q, k, v, decay, s0]
    st_shape = jax.ShapeDtypeStruct(s0.shape, F32)
    scratch = []
    if fused is not None:
        o_other, r, x, mod, gn, w_out = fused
        d = x.shape[2]
        in_specs += [
            pl.BlockSpec((1, tblk, dvt), blk),
            pl.BlockSpec((1, tblk, dvt), blk),
            pl.BlockSpec((1, tblk, d), blk),
            pl.BlockSpec((1, 6, d), lambda i, j: (i, 0, 0)),
            _const_spec(gn.shape),
            _const_spec(w_out.shape),
        ]
        args += [o_other, r, x, mod, gn, w_out]
        out_specs = [pl.BlockSpec((1, tblk, d), blk), st_spec]
        out_shape = [jax.ShapeDtypeStruct(x.shape, F32), st_shape]
        scratch = [pltpu.VMEM((tblk, dvt), BF16)]
    elif want_o:
        out_specs = [pl.BlockSpec((1, tblk, dvt), blk), st_spec]
        out_shape = [jax.ShapeDtypeStruct((b, l, dvt), BF16), st_shape]
    else:
        out_specs = [st_spec]
        out_shape = [st_shape]
    outs = pl.pallas_call(
        functools.partial(_gla_scan_kernel, reverse=reverse, want_o=want_o,
                          fuse_out=fused is not None, tblk=tblk),
        grid=(b, nblk),
        in_specs=in_specs,
        out_specs=out_specs,
        out_shape=out_shape,
        scratch_shapes=scratch,
        compiler_params=_cparams("parallel", "arbitrary"),
        name="gla_scan_%s%s" % ("bwd" if reverse else "fwd", "_out" if fused is not None else ""),
    )(*args)
    if want_o:
        return outs[0], outs[1]
    return None, outs[0]


def _mla_proj_kernel(*refs, want_q):
    (x_ref, mod_ref, g_ref, cos_ref, sin_ref, wdq_ref, wdkv_ref, wdpe_ref, qln_ref, kvln_ref,
     wuq_ref, wuqs_ref, wuk_ref, wuvt_ref, qn_ref, kn_ref) = refs[:16]
    if want_q:
        q_ref, k_ref, vt_ref = refs[16:]
    else:
        k_ref, vt_ref = refs[16:]
    mod = mod_ref[0]
    qn = qn_ref[...]
    kn = kn_ref[...]
    q_scale = (MLA_NOPE + MLA_ROPE) ** -0.5 * LOG2_E
    sub = min(x_ref.shape[1], MLA_PROJ_ROWS)
    parts = [slice(i * sub, (i + 1) * sub) for i in range(x_ref.shape[1] // sub)]

    def rope_part(raw, raw_partner, gains, rows):
        return ((raw * (cos_ref[rows, :] * gains[1:2]) + raw_partner * (sin_ref[rows, :] * gains[2:3]))
                * _rms_scale(raw, MLA_ROPE))

    def normed(c, gain_ref):
        return (c * _rms_scale(c, c.shape[-1]) * gain_ref[...]).astype(BF16)

    hs = [_modulate(x_ref[0, r, :], g_ref[...], mod[0:1], mod[1:2]).astype(BF16) for r in parts]
    c_kv = [_dot(h, wdkv_ref[...]) for h in hs]
    kpe = [_dot(h, wdpe_ref[...]) for h in hs]
    if want_q:
        c_q = [_dot(h, wdq_ref[...]) for h in hs]
    c_kv = [normed(c, kvln_ref) for c in c_kv]
    for c, r in zip(c_kv, parts):
        vt_ref[0, :, r] = _dot_nt(wuvt_ref[...], c).astype(BF16)
    kn_all = [_dot(c, wuk_ref[...]) for c in c_kv]
    k_rope = [(rope_part(p[:, :128], p[:, 128:], kn, r) + kn[3:4]).astype(BF16) for p, r in zip(kpe, parts)]
    if want_q:
        c_q = [normed(c, qln_ref) for c in c_q]
        qm = [_dot(c, wuq_ref[...]) for c in c_q]
        qs = [_dot(c, wuqs_ref[...]) for c in c_q]
    for i, r in enumerate(parts):
        for hd in range(MLA_HEADS):
            k_nope = kn_all[i][:, hd * MLA_NOPE:(hd + 1) * MLA_NOPE]
            k_ref[0, hd, r, :MLA_NOPE] = (k_nope * _rms_scale(k_nope, MLA_NOPE) * kn[0:1]).astype(BF16)
            k_ref[0, hd, r, MLA_NOPE:] = k_rope[i]
            if want_q:
                base = hd * MLA_QPAD
                q_nope = qm[i][:, base:base + MLA_NOPE]
                q_nope = q_nope * _rms_scale(q_nope, MLA_NOPE) * (qn[0:1] * q_scale)
                q_ref[0, hd, r, :MLA_NOPE] = q_nope.astype(BF16)
                q_rope = rope_part(qm[i][:, base + MLA_NOPE:base + MLA_QPAD],
                                   qs[i][:, hd * 128:(hd + 1) * 128], qn, r) * q_scale + qn[3:4]
                q_ref[0, hd, r, MLA_NOPE:] = q_rope.astype(BF16)


def _mla_proj(x, mod, norm_g, cos, sin, w, *, want_q, tm):
    b, l, d = x.shape
    consts = [w["wdq"], w["wdkv"], w["wdpe"], w["qln"], w["kvln"], w["wuq"], w["wuqs"], w["wuk"],
              w["wuvt"], w["qn"], w["kn"]]
    hspec = pl.BlockSpec((1, MLA_HEADS, tm, MLA_QPAD), lambda i, j: (i, 0, j, 0))
    hshape = jax.ShapeDtypeStruct((b, MLA_HEADS, l, MLA_QPAD), BF16)
    out_specs = [hspec, pl.BlockSpec((1, MLA_HEADS * MLA_V, tm), lambda i, j: (i, 0, j))]
    out_shape = [hshape, jax.ShapeDtypeStruct((b, MLA_HEADS * MLA_V, l), BF16)]
    if want_q:
        out_specs = [hspec] + out_specs
        out_shape = [hshape] + out_shape
    return pl.pallas_call(
        functools.partial(_mla_proj_kernel, want_q=want_q),
        grid=(b, l // tm),
        in_specs=[
            pl.BlockSpec((1, tm, d), lambda i, j: (i, j, 0)),
            pl.BlockSpec((1, 6, d), lambda i, j: (i, 0, 0)),
            _const_spec(norm_g.shape),
            pl.BlockSpec((tm, 128), lambda i, j: (j, 0)),
            pl.BlockSpec((tm, 128), lambda i, j: (j, 0)),
        ] + [_const_spec(a.shape) for a in consts],
        out_specs=out_specs,
        out_shape=out_shape,
        compiler_params=_cparams("parallel", "parallel"),
        name="mla_proj_q" if want_q else "mla_proj_kv",
    )(x, mod, norm_g, cos, sin, *consts)


def _attn_kernel(*refs, with_latent, tk):
    if with_latent:
        q_ref, kc_ref, vtc_ref, k_ref, vt_ref, ot_ref = refs
    else:
        q_ref, kc_ref, vtc_ref, ot_ref = refs
    q = q_ref[0, 0]
    chunks = [(kc_ref, vtc_ref, 0, kc_ref.shape[2])]
    if with_latent:
        chunks += [(k_ref, vt_ref, j * tk, tk) for j in range(k_ref.shape[2] // tk)]

    def scores(idx):
        kr, _, start, size = chunks[idx]
        return _dot_nt(kr[0, 0, start:start + size, :], q)

    s_next = scores(0)
    m = denom = acc = None
    for idx, (_, vr, start, size) in enumerate(chunks):
        s = s_next
        if idx + 1 < len(chunks):
            s_next = scores(idx + 1)
        vt = vr[0, :, start:start + size]
        cmax = jnp.max(s, axis=0, keepdims=True)
        if m is None:
            m = cmax
            p = jnp.exp2(s - m)
            denom = jnp.sum(p, axis=0, keepdims=True)
            acc = _dot(vt, p.astype(BF16))
        else:
            m_new = jnp.maximum(m, cmax)
            alpha = jnp.exp2(m - m_new)
            p = jnp.exp2(s - m_new)
            denom = denom * alpha + jnp.sum(p, axis=0, keepdims=True)
            acc = acc * alpha + _dot(vt, p.astype(BF16))
            m = m_new
    ot_ref[0] = (acc * (1.0 / denom)).astype(BF16)


def _attn_shifted_kernel(q_ref, kc_ref, vtc_ref, k_ref, vt_ref, ot_ref, *, tk):
    q = q_ref[0, 0]
    chunks = [(kc_ref, vtc_ref, 0, kc_ref.shape[2])]
    chunks += [(k_ref, vt_ref, j * tk, tk) for j in range(k_ref.shape[2] // tk)]

    def scores(idx):
        kr, _, start, size = chunks[idx]
        return _dot_nt(kr[0, 0, start:start + size, :], q)

    s_next = scores(0)
    acc = denom8 = None
    for idx, (_, vr, start, size) in enumerate(chunks):
        s = s_next
        if idx + 1 < len(chunks):
            s_next = scores(idx + 1)
        p = jnp.exp2(s)
        part = jnp.sum(p.reshape(size // SUBLANES, SUBLANES, p.shape[1]), axis=0)
        y = _dot(vr[0, :, start:start + size], p.astype(BF16))
        acc = y if acc is None else acc + y
        denom8 = part if denom8 is None else denom8 + part
    denom = jnp.sum(denom8, axis=0, keepdims=True)
    ot_ref[0] = (acc * (1.0 / denom)).astype(BF16)


def _attention(q, kc, vtc, k=None, vt=None, *, tq, tk=512, shifted=False):
    b, nh, lq, dq = q.shape
    lc = kc.shape[2]
    dv = vtc.shape[1] // nh
    with_latent = k is not None
    in_specs = [
        pl.BlockSpec((1, 1, tq, dq), lambda i, h, j: (i, h, j, 0)),
        pl.BlockSpec((1, 1, lc, dq), lambda i, h, j: (i, h, 0, 0)),
        pl.BlockSpec((1, dv, lc), lambda i, h, j: (i, h, 0)),
    ]
    args = [q, kc, vtc]
    if with_latent:
        lk = k.shape[2]
        in_specs += [pl.BlockSpec((1, 1, lk, dq), lambda i, h, j: (i, h, 0, 0)),
                     pl.BlockSpec((1, dv, lk), lambda i, h, j: (i, h, 0))]
        args += [k, vt]
    if shifted:
        body = functools.partial(_attn_shifted_kernel, tk=tk)
        name = "mla_attn_shifted"
    else:
        body = functools.partial(_attn_kernel, with_latent=with_latent, tk=tk)
        name = "mla_attn" if with_latent else "mla_attn_ctx"
    return pl.pallas_call(
        body,
        grid=(b, nh, lq // tq),
        in_specs=in_specs,
        out_specs=pl.BlockSpec((1, dv, tq), lambda i, h, j: (i, h, j)),
        out_shape=jax.ShapeDtypeStruct((b, nh * dv, lq), BF16),
        compiler_params=_cparams("parallel", "parallel", "parallel"),
        name=name,
    )(*args)


def _latent_attention(use_shift, q, kc, vtc, k, vt):
    lq = q.shape[2]
    lk = k.shape[2]
    fast = functools.partial(_attention, tq=min(lq, 2048), tk=min(lk, 2048), shifted=True)
    exact = functools.partial(_attention, tq=min(lq, 512), tk=min(lk, 512))
    return lax.cond(use_shift, fast, exact, q, kc, vtc, k, vt)


def _out_proj_kernel(ot_ref, x_ref, mod_ref, w_ref, xo_ref):
    gate = mod_ref[0][2:3]
    xo_ref[0] = x_ref[0] + gate * _dot_tn(ot_ref[0], w_ref[...])


def _out_proj(ot, x, mod, w_out, *, tm):
    b, l, d = x.shape
    row = lambda i, j: (i, j, 0)
    return pl.pallas_call(
        _out_proj_kernel,
        grid=(b, l // tm),
        in_specs=[
            pl.BlockSpec((1, ot.shape[1], tm), lambda i, j: (i, 0, j)),
            pl.BlockSpec((1, tm, d), row),
            pl.BlockSpec((1, 6, d), lambda i, j: (i, 0, 0)),
            _const_spec(w_out.shape),
        ],
        out_specs=pl.BlockSpec((1, tm, d), row),
        out_shape=jax.ShapeDtypeStruct(x.shape, F32),
        compiler_params=_cparams("parallel", "parallel"),
        name="mla_out",
    )(ot, x, mod, w_out)


def _ffn_kernel(xm_ref, xp_ref, xn_ref, mod_ref, g_ref, wa_ref, wg_ref, ca_ref, cg_ref, wd_ref,
                xo_ref, h_scr, ua_scr, ug_scr, act_scr, *, tm, fc):
    i = pl.program_id(1)
    mod = mod_ref[0]
    g = g_ref[...]
    hal = CONV_HALO

    def hidden(x):
        return _modulate(x, g, mod[3:4], mod[4:5])

    h_scr[0:tm] = hidden(xm_ref[0]).astype(BF16)
    nxt = jnp.where(i == pl.num_programs(1) - 1, 0.0, hidden(xn_ref[0]))
    prv = jnp.where(i == 0, 0.0, hidden(xp_ref[0]))
    h_scr[tm:] = jnp.concatenate([nxt, prv], axis=0).astype(BF16)

    def put(u_scr, slab, u):
        u_scr[slab, hal:hal + tm, :] = u[0:tm]
        u_scr[slab, hal + tm:, :] = u[tm:tm + hal]
        u_scr[slab, 0:hal, :] = u[tm + hal:]

    def conv(u_scr, slab, cw):
        return (u_scr[slab, hal - 1:hal - 1 + tm, :] * cw[0:1] + u_scr[slab, hal:hal + tm, :] * cw[1:2]
                + u_scr[slab, hal + 1:hal + 1 + tm, :] * cw[2:3] + cw[3:4])

    per = fc // LANES
    for ci in range(wa_ref.shape[1] // fc):
        cols = slice(ci * fc, (ci + 1) * fc)
        ua = _dot(h_scr[...], wa_ref[:, cols])
        ug = _dot(h_scr[...], wg_ref[:, cols])
        for s in range(per):
            put(ua_scr, ci * per + s, ua[:, s * LANES:(s + 1) * LANES])
            put(ug_scr, ci * per + s, ug[:, s * LANES:(s + 1) * LANES])
        for s in range(per):
            slab = ci * per + s
            lanes = slice(slab * LANES, (slab + 1) * LANES)
            a = conv(ua_scr, slab, ca_ref[:, lanes])
            gt = conv(ug_scr, slab, cg_ref[:, lanes])
            act_scr[:, lanes] = (gt * _sigmoid(gt) * a).astype(BF16)
    xo_ref[0] = xm_ref[0] + mod[5:6] * _dot(act_scr[...], wd_ref[...])


def _conv_ffn(x, mod, norm_g, wa, wg, ca, cg, wd, *, tm, fc):
    b, l, d = x.shape
    hal = CONV_HALO
    nt = l // tm
    per = tm // hal
    last = l // hal - 1
    d_ff = wa.shape[1]
    return pl.pallas_call(
        functools.partial(_ffn_kernel, tm=tm, fc=fc),
        grid=(b, nt),
        in_specs=[
            pl.BlockSpec((1, tm, d), lambda i, j: (i, j, 0)),
            pl.BlockSpec((1, hal, d), lambda i, j: (i, jnp.maximum(j * per - 1, 0), 0)),
            pl.BlockSpec((1, hal, d), lambda i, j: (i, jnp.minimum((j + 1) * per, last), 0)),
            pl.BlockSpec((1, 6, d), lambda i, j: (i, 0, 0)),
            _const_spec(norm_g.shape),
            _const_spec(wa.shape),
            _const_spec(wg.shape),
            _const_spec(ca.shape),
            _const_spec(cg.shape),
            _const_spec(wd.shape),
        ],
        out_specs=pl.BlockSpec((1, tm, d), lambda i, j: (i, j, 0)),
        out_shape=jax.ShapeDtypeStruct(x.shape, F32),
        scratch_shapes=[
            pltpu.VMEM((tm + 2 * hal, d), BF16),
            pltpu.VMEM((d_ff // LANES, tm + 2 * hal, LANES), F32),
            pltpu.VMEM((d_ff // LANES, tm + 2 * hal, LANES), F32),
            pltpu.VMEM((tm, d_ff), BF16),
        ],
        compiler_params=_cparams("parallel", "parallel"),
        name="conv_ffn",
    )(x, x, x, mod, norm_g, wa, wg, ca, cg, wd)


def _rope_tables(length):
    nf = MLA_ROPE // 4
    t = jnp.arange(length)
    pos = jnp.stack([(t // GRID_W).astype(F32), (t % GRID_W).astype(F32)], axis=1)
    inv = ROPE_THETA ** (-jnp.arange(nf, dtype=F32) / nf)
    ang = pos[:, :, None] * inv
    cos = jnp.cos(ang)[:, :, None, :] * jnp.ones((1, 1, 2, 1), F32)
    sin = jnp.sin(ang)[:, :, None, :] * jnp.array([-1.0, 1.0], F32)[None, None, :, None]
    pad = lambda a: jnp.pad(a.reshape(length, MLA_ROPE), ((0, 0), (0, 128 - MLA_ROPE)))
    return pad(cos), pad(sin)


def _rope_partner(a):
    nf = MLA_ROPE // 4
    r = a.reshape(a.shape[:-1] + (2, 2, nf))
    return jnp.flip(r, axis=-2).reshape(a.shape)


def _pad_last(a, width):
    return jnp.pad(a, [(0, 0)] * (a.ndim - 1) + [(0, width - a.shape[-1])])


def _mla_weights(w_down, q_lora_norm, kv_lora_norm, w_uq, w_ukv, q_norm, k_norm):
    q_rank = q_lora_norm.shape[0]
    kv_rank = kv_lora_norm.shape[0]
    qk = MLA_NOPE + MLA_ROPE
    w_pe = w_down[:, q_rank + kv_rank:]
    wuq = w_uq.reshape(q_rank, MLA_HEADS, qk)
    wuq_rope = wuq[..., MLA_NOPE:]
    wukv = w_ukv.reshape(kv_rank, MLA_HEADS, MLA_NOPE + MLA_V)

    def max_sq_norm(g):
        return MLA_NOPE * jnp.max(g[:MLA_NOPE] ** 2) + MLA_ROPE * jnp.max(g[MLA_NOPE:] ** 2)

    bound = jnp.sqrt(max_sq_norm(q_norm) * max_sq_norm(k_norm)) * (qk ** -0.5 * LOG2_E * 1.02)
    use_shift = bound <= ATTN_MAX_SHIFT
    shift = jnp.where(use_shift, bound, 0.0)
    lane = (jnp.arange(128) == SHIFT_LANE).astype(F32)

    def gains(g, shift_value):
        rope = g[MLA_NOPE:]
        return jnp.stack([g[:MLA_NOPE], _pad_last(rope, 128), _pad_last(_rope_partner(rope), 128),
                          lane * shift_value])

    return {
        "use_shift": use_shift,
        "wdq": w_down[:, :q_rank].astype(BF16),
        "wdkv": w_down[:, q_rank:q_rank + kv_rank].astype(BF16),
        "wdpe": jnp.concatenate([_pad_last(w_pe, 128), _pad_last(_rope_partner(w_pe), 128)],
                                axis=1).astype(BF16),
        "qln": q_lora_norm[None, :],
        "kvln": kv_lora_norm[None, :],
        "wuq": _pad_last(wuq, MLA_QPAD).reshape(q_rank, MLA_HEADS * MLA_QPAD).astype(BF16),
        "wuqs": _pad_last(_rope_partner(wuq_rope), 128).reshape(q_rank, MLA_HEADS * 128).astype(BF16),
        "wuk": wukv[..., :MLA_NOPE].reshape(kv_rank, MLA_HEADS * MLA_NOPE).astype(BF16),
        "wuvt": wukv[..., MLA_NOPE:].reshape(kv_rank, MLA_HEADS * MLA_V).T.astype(BF16),
        "qn": gains(q_norm, shift),
        "kn": gains(k_norm, jnp.where(use_shift, -1.0, 0.0)),
    }


def kernel(x, c, ctx, c_ctx, w_ada, b_ada, norm_mix, norm_ffn, gla_w_in, gla_gate_w1, gla_gate_w2, gla_gate_b, gla_out_norm, gla_w_out, mla_w_down, mla_q_lora_norm, mla_kv_lora_norm, mla_w_uq, mla_w_ukv, mla_q_norm, mla_k_norm, mla_w_out, ffn_w_up, ffn_conv_w, ffn_conv_b, ffn_w_down):
    bsz, seq, d = x.shape
    lc = ctx.shape[1]
    depth = w_ada.shape[0]
    d_ff = ffn_w_down.shape[1]
    dqk = gla_gate_w2.shape[-1]
    rank = gla_gate_w1.shape[-1]
    dv = gla_out_norm.shape[-1]

    cond_rows = -(-(bsz + 1) // 8) * 8
    cond = jnp.zeros((cond_rows, d), F32).at[:bsz].set(c).at[bsz].set(c_ctx)
    mods = _ada_all(cond, w_ada, b_ada)

    cos_l, sin_l = _rope_tables(seq)
    cos_c = _pad_last(jnp.ones((lc, MLA_ROPE), F32), 128)
    sin_c = jnp.zeros((lc, 128), F32)

    tm_l = min(seq, 512)
    tm_c = min(lc, 512)
    xc = ctx
    for i in range(depth):
        last = i == depth - 1
        j = i // 2
        mod_l = mods[i, :bsz].reshape(bsz, 6, d)
        mod_c = jnp.broadcast_to(mods[i, bsz].reshape(1, 6, d), (bsz, 6, d))
        g_mix = norm_mix[i][None, :]
        if i % 2 == 0:
            w_in = gla_w_in[j].astype(BF16)
            w1 = _pad_last(jnp.concatenate([gla_gate_w1[j, 0], gla_gate_w1[j, 1]], axis=1),
                           GLA_RANK_PAD).astype(BF16)
            w2 = jnp.zeros((GLA_RANK_PAD, 2 * dqk), F32)
            w2 = w2.at[:rank, :dqk].set(gla_gate_w2[j, 0]).at[rank:2 * rank, dqk:].set(gla_gate_w2[j, 1])
            w2 = w2.astype(BF16)
            gb = gla_gate_b[j].reshape(1, 2 * dqk)
            gn = gla_out_norm[j][None, :]
            w_out = gla_w_out[j].astype(BF16)
            s0 = jnp.zeros((bsz, GLA_HEADS, dv, dqk // GLA_HEADS), F32)

            proj_c = _gla_proj(xc, mod_c, g_mix, w_in, w1, w2, gb, tm=tm_c)
            proj_l = _gla_proj(x, mod_l, g_mix, w_in, w1, w2, gb, tm=tm_l)
            vc, rc, fwd_c, bwd_c = proj_c[0], proj_c[1], proj_c[2:5], proj_c[5:8]
            v, r, fwd_l, bwd_l = proj_l[0], proj_l[1], proj_l[2:5], proj_l[5:8]
            if last:
                _, s_fwd = _gla_scan(*fwd_c[:2], vc, fwd_c[2], s0, direction=0, want_o=False)
                _, s_bwd = _gla_scan(*bwd_c[:2], vc, bwd_c[2], s0, direction=1, want_o=False)
            else:
                oc_b, s_bwd = _gla_scan(*bwd_c[:2], vc, bwd_c[2], s0, direction=1)
                xc, s_fwd = _gla_scan(*fwd_c[:2], vc, fwd_c[2], s0, direction=0,
                                      fused=(oc_b, rc, xc, mod_c, gn, w_out))
            o_b, _ = _gla_scan(*bwd_l[:2], v, bwd_l[2], s_bwd, direction=1)
            x, _ = _gla_scan(*fwd_l[:2], v, fwd_l[2], s_fwd, direction=0,
                             fused=(o_b, r, x, mod_l, gn, w_out))
        else:
            w = _mla_weights(mla_w_down[j], mla_q_lora_norm[j], mla_kv_lora_norm[j], mla_w_uq[j],
                             mla_w_ukv[j], mla_q_norm[j], mla_k_norm[j])
            w_out = mla_w_out[j].astype(BF16)
            if last:
                kc, vtc = _mla_proj(xc, mod_c, g_mix, cos_c, sin_c, w, want_q=False, tm=tm_c)
            else:
                qc, kc, vtc = _mla_proj(xc, mod_c, g_mix, cos_c, sin_c, w, want_q=True, tm=tm_c)
            q, k, vt = _mla_proj(x, mod_l, g_mix, cos_l, sin_l, w, want_q=True, tm=min(seq, 1024))
            ot = _latent_attention(w["use_shift"], q, kc, vtc, k, vt)
            x = _out_proj(ot, x, mod_l, w_out, tm=min(seq, 1024))
            if not last:
                otc = _attention(qc, kc, vtc, tq=min(lc, 512))
                xc = _out_proj(otc, xc, mod_c, w_out, tm=tm_c)

        g_ffn = norm_ffn[i][None, :]
        wa = ffn_w_up[i, :, :d_ff].astype(BF16)
        wg = ffn_w_up[i, :, d_ff:].astype(BF16)
        conv = jnp.concatenate([ffn_conv_w[i], ffn_conv_b[i][None, :]], axis=0)
        ca, cg = conv[:, :d_ff], conv[:, d_ff:]
        wd = ffn_w_down[i].astype(BF16)
        x = _conv_ffn(x, mod_l, g_ffn, wa, wg, ca, cg, wd, tm=tm_l, fc=FFN_COLS)
        if not last:
            xc = _conv_ffn(xc, mod_c, g_ffn, wa, wg, ca, cg, wd, tm=tm_c, fc=FFN_COLS)
    return x
```

```python
import functools

import jax
import jax.numpy as jnp
from jax import lax
from jax.experimental import pallas as pl
from jax.experimental.pallas import tpu as pltpu

F32 = jnp.float32
BF16 = jnp.bfloat16

LANES = 128
SUBLANES = 8
EPS = 1e-6
GRID_W = 64
ROPE_THETA = 10000.0
LOG2_E = 1.4426950408889634

GLA_HEADS = 4
GLA_GATE_NORMALIZER = 16.0
GLA_CHUNK = 128
GLA_RANK_PAD = 128

MLA_HEADS = 8
MLA_NOPE = 128
MLA_ROPE = 64
MLA_V = 128
MLA_QPAD = 256
MLA_PROJ_ROWS = 256
ATTN_MAX_SCORE = 50.0

FFN_COLS = 256
CONV_HALO = SUBLANES
VMEM_LIMIT = 56 * 1024 * 1024


def _cparams(*sem):
    return pltpu.CompilerParams(dimension_semantics=sem, vmem_limit_bytes=VMEM_LIMIT)


def _const_spec(shape):
    nd = len(shape)
    return pl.BlockSpec(shape, lambda *_: (0,) * nd, pipeline_mode=pl.Buffered(1))


def _dot(a, b):
    return jnp.dot(a, b, preferred_element_type=F32)


def _dot_nt(a, b):
    return lax.dot_general(a, b, (((1,), (1,)), ((), ())), preferred_element_type=F32)


def _dot_tn(a, b):
    return lax.dot_general(a, b, (((0,), (0,)), ((), ())), preferred_element_type=F32)


def _sigmoid(x):
    return 1.0 / (1.0 + jnp.exp(-x))


def _rms_scale(x, width):
    ss = jnp.sum(x * x, axis=-1, keepdims=True)
    return lax.rsqrt(ss * (1.0 / width) + EPS)


def _modulate(x, g, shift, scale):
    return x * _rms_scale(x, x.shape[-1]) * (g * (1.0 + scale)) + shift


def _ada_kernel(cond_ref, w_ref, b_ref, o_ref):
    cond = cond_ref[...]
    s = cond * _sigmoid(cond)
    o_ref[0] = jnp.dot(s, w_ref[0], preferred_element_type=F32,
                       precision=lax.Precision.HIGHEST) + b_ref[0]


def _ada_all(cond, w_ada, b_ada):
    depth, d, d6 = w_ada.shape
    rows = cond.shape[0]
    return pl.pallas_call(
        _ada_kernel,
        grid=(depth, d6 // d),
        in_specs=[
            pl.BlockSpec((rows, d), lambda i, j: (0, 0)),
            pl.BlockSpec((1, d, d), lambda i, j: (i, 0, j)),
            pl.BlockSpec((1, 1, d), lambda i, j: (i, 0, j)),
        ],
        out_specs=pl.BlockSpec((1, rows, d), lambda i, j: (i, 0, j)),
        out_shape=jax.ShapeDtypeStruct((depth, rows, d6), F32),
        compiler_params=_cparams("parallel", "parallel"),
        name="ada_mod",
    )(cond, w_ada, b_ada.reshape(depth, 1, d6))


def _gla_proj_kernel(x_ref, mod_ref, g_ref, win_ref, w1_ref, w2_ref, gb_ref,
                     v_ref, r_ref, qf_ref, kf_ref, df_ref, qb_ref, kb_ref, db_ref, *, dqk):
    mod = mod_ref[0]
    h = _modulate(x_ref[0], g_ref[...], mod[0:1], mod[1:2]).astype(BF16)
    c = GLA_CHUNK
    dk = dqk // GLA_HEADS
    dv = v_ref.shape[2]
    low = _dot(h, w1_ref[...]).astype(BF16)
    z = _dot(low, w2_ref[...]) + gb_ref[...]
    pqk = _dot(h, win_ref[:, :2 * dqk])
    unit = LOG2_E / GLA_GATE_NORMALIZER
    gates = jnp.minimum(z, 0.0) * unit - jnp.log(1.0 + jnp.exp(-jnp.abs(z))) * unit
    row = lax.broadcasted_iota(jnp.int32, (c, c), 0)
    col = lax.broadcasted_iota(jnp.int32, (c, c), 1)
    for direction, (q_ref, k_ref, d_ref) in enumerate(((qf_ref, kf_ref, df_ref), (qb_ref, kb_ref, db_ref))):
        reverse = direction == 1
        tri = ((col >= row) if reverse else (col <= row)).astype(BF16)
        tri2 = jnp.concatenate([tri, tri], axis=1)
        i_mid = c // 2 if reverse else c // 2 - 1
        i_end = 0 if reverse else c - 1
        g = gates[:, direction * dqk:(direction + 1) * dqk]
        g_hi = g.astype(BF16)
        g_lo = (g - g_hi.astype(F32)).astype(BF16)
        for ci in range(x_ref.shape[1] // c):
            rows = slice(ci * c, (ci + 1) * c)
            cum = _dot(tri2, jnp.concatenate([g_hi[rows], g_lo[rows]], axis=0))
            c_mid = cum[i_mid:i_mid + 1, :]
            c_end = cum[i_end:i_end + 1, :]
            rel = cum - c_mid
            q_in = pqk[rows, :dqk] * (jnp.exp2(rel) * (dk ** -0.5))
            k_in = pqk[rows, dqk:] * jnp.exp2(-rel)
            q_ref[0, rows, :dqk] = q_in.astype(BF16)
            q_ref[0, rows, dqk:] = (q_in * jnp.exp2(c_mid)).astype(BF16)
            k_ref[0, rows, :dqk] = k_in.astype(BF16)
            k_ref[0, rows, dqk:] = (k_in * jnp.exp2(c_end - c_mid)).astype(BF16)
            d_ref[0, ci * SUBLANES:(ci + 1) * SUBLANES, :] = jnp.broadcast_to(jnp.exp2(c_end), (SUBLANES, dqk))
    pvr = _dot(h, win_ref[:, 2 * dqk:])
    v_ref[0] = pvr[:, :dv].astype(BF16)
    r_ref[0] = pvr[:, dv:].astype(BF16)


def _gla_proj(x, mod, norm_g, w_in, w1, w2, gb, *, tm):
    b, l, d = x.shape
    dqk = w2.shape[1] // 2
    dv = (w_in.shape[1] - 2 * dqk) // 2
    drows = tm // GLA_CHUNK * SUBLANES
    row = lambda i, j: (i, j, 0)
    tok = lambda width, dtype: (pl.BlockSpec((1, tm, width), row), jax.ShapeDtypeStruct((b, l, width), dtype))
    dec = (pl.BlockSpec((1, drows, dqk), row),
           jax.ShapeDtypeStruct((b, l // GLA_CHUNK * SUBLANES, dqk), F32))
    outs = [tok(dv, BF16), tok(dv, BF16)] + 2 * [tok(2 * dqk, BF16), tok(2 * dqk, BF16), dec]
    return pl.pallas_call(
        functools.partial(_gla_proj_kernel, dqk=dqk),
        grid=(b, l // tm),
        in_specs=[
            pl.BlockSpec((1, tm, d), row),
            pl.BlockSpec((1, 6, d), lambda i, j: (i, 0, 0)),
            _const_spec(norm_g.shape),
            _const_spec(w_in.shape),
            _const_spec(w1.shape),
            _const_spec(w2.shape),
            _const_spec(gb.shape),
        ],
        out_specs=[o[0] for o in outs],
        out_shape=[o[1] for o in outs],
        compiler_params=_cparams("parallel", "parallel"),
        name="gla_proj",
    )(x, mod, norm_g, w_in, w1, w2, gb)


def _gla_scan_kernel(*refs, reverse, want_o, fuse_out, tblk):
    if fuse_out:
        (q_ref, k_ref, v_ref, d_ref, s0_ref, ob_ref, r_ref, x_ref, mod_ref, gn_ref, wout_ref,
         xo_ref, st_ref, y_scr) = refs
    elif want_o:
        q_ref, k_ref, v_ref, d_ref, s0_ref, o_ref, st_ref = refs
    else:
        q_ref, k_ref, v_ref, d_ref, s0_ref, st_ref = refs

    @pl.when(pl.program_id(1) == 0)
    def _():
        st_ref[...] = s0_ref[...]

    c = GLA_CHUNK
    dqk = q_ref.shape[2] // 2
    dk = dqk // GLA_HEADS
    dv = v_ref.shape[2] // GLA_HEADS
    row = lax.broadcasted_iota(jnp.int32, (c, c), 0)
    col = lax.broadcasted_iota(jnp.int32, (c, c), 1)
    mask = (col >= row) if reverse else (col <= row)

    nch = tblk // c
    states = [st_ref[0, h] for h in range(GLA_HEADS)]
    for ci in (reversed(range(nch)) if reverse else range(nch)):
        rows = slice(ci * c, (ci + 1) * c)
        for h in range(GLA_HEADS):
            intra = slice(h * dk, (h + 1) * dk)
            state = slice(dqk + h * dk, dqk + (h + 1) * dk)
            vs = slice(h * dv, (h + 1) * dv)
            v = v_ref[0, rows, vs]
            st = states[h]
            if want_o:
                scores = jnp.where(mask, _dot_nt(q_ref[0, rows, intra], k_ref[0, rows, intra]), 0.0)
                o = _dot(scores.astype(BF16), v) + _dot_nt(q_ref[0, rows, state], st.astype(BF16))
                if fuse_out:
                    o = o + ob_ref[0, rows, vs].astype(F32)
                    r = r_ref[0, rows, vs].astype(F32)
                    y = o * _rms_scale(o, dv) * gn_ref[...] * (r * _sigmoid(r))
                    y_scr[rows, vs] = y.astype(BF16)
                else:
                    o_ref[0, rows, vs] = o.astype(BF16)
            decay_end = d_ref[0, ci * SUBLANES:ci * SUBLANES + 1, intra]
            states[h] = st * decay_end + _dot_tn(v, k_ref[0, rows, state])
    for h in range(GLA_HEADS):
        st_ref[0, h] = states[h]

    if fuse_out:
        gate = mod_ref[0][2:3]
        xo_ref[0] = x_ref[0] + gate * _dot(y_scr[...], wout_ref[...])


def _gla_scan(q, k, v, decay, s0, *, direction, want_o=True, fused=None):
    b, l, dq2 = q.shape
    dvt = v.shape[2]
    tblk = min(l, 1024)
    nblk = l // tblk
    reverse = direction == 1
    blk = (lambda i, j: (i, nblk - 1 - j, 0)) if reverse else (lambda i, j: (i, j, 0))
    st_spec = pl.BlockSpec((1,) + s0.shape[1:], lambda i, j: (i, 0, 0, 0))
    in_specs = [
        pl.BlockSpec((1, tblk, dq2), blk),
        pl.BlockSpec((1, tblk, dq2), blk),
        pl.BlockSpec((1, tblk, dvt), blk),
        pl.BlockSpec((1, tblk // GLA_CHUNK * SUBLANES, decay.shape[2]), blk),
        st_spec,
    ]
    args = [q, k, v, decay, s0]
    st_shape = jax.ShapeDtypeStruct(s0.shape, F32)
    scratch = []
    if fused is not None:
        o_other, r, x, mod, gn, w_out = fused
        d = x.shape[2]
        in_specs += [
            pl.BlockSpec((1, tblk, dvt), blk),
            pl.BlockSpec((1, tblk, dvt), blk),
            pl.BlockSpec((1, tblk, d), blk),
            pl.BlockSpec((1, 6, d), lambda i, j: (i, 0, 0)),
            _const_spec(gn.shape),
            _const_spec(w_out.shape),
        ]
        args += [o_other, r, x, mod, gn, w_out]
        out_specs = [pl.BlockSpec((1, tblk, d), blk), st_spec]
        out_shape = [jax.ShapeDtypeStruct(x.shape, F32), st_shape]
        scratch = [pltpu.VMEM((tblk, dvt), BF16)]
    elif want_o:
        out_specs = [pl.BlockSpec((1, tblk, dvt), blk), st_spec]
        out_shape = [jax.ShapeDtypeStruct((b, l, dvt), BF16), st_shape]
    else:
        out_specs = [st_spec]
        out_shape = [st_shape]
    outs = pl.pallas_call(
        functools.partial(_gla_scan_kernel, reverse=reverse, want_o=want_o,
                          fuse_out=fused is not None, tblk=tblk),
        grid=(b, nblk),
        in_specs=in_specs,
        out_specs=out_specs,
        out_shape=out_shape,
        scratch_shapes=scratch,
        compiler_params=_cparams("parallel", "arbitrary"),
        name="gla_scan_%s%s" % ("bwd" if reverse else "fwd", "_out" if fused is not None else ""),
    )(*args)
    if want_o:
        return outs[0], outs[1]
    return None, outs[0]


def _mla_proj_kernel(*refs, want_q):
    (x_ref, mod_ref, g_ref, cos_ref, sin_ref, wdq_ref, wdkv_ref, wdpe_ref, qln_ref, kvln_ref,
     wuq_ref, wuqs_ref, wuk_ref, wuvt_ref, qn_ref, kn_ref) = refs[:16]
    if want_q:
        q_ref, k_ref, vt_ref = refs[16:]
    else:
        k_ref, vt_ref = refs[16:]
    mod = mod_ref[0]
    kn = kn_ref[...]
    qn = qn_ref[...] * ((MLA_NOPE + MLA_ROPE) ** -0.5 * LOG2_E)
    sub = min(x_ref.shape[1], MLA_PROJ_ROWS)
    parts = [slice(i * sub, (i + 1) * sub) for i in range(x_ref.shape[1] // sub)]

    def rope_tables(gains, rows):
        return cos_ref[rows, :] * gains[1:2], sin_ref[rows, :] * gains[2:3]

    def rope_part(raw, raw_partner, tables):
        return (raw * tables[0] + raw_partner * tables[1]) * _rms_scale(raw, MLA_ROPE)

    def normed(c, gain_ref):
        return (c * _rms_scale(c, c.shape[-1]) * gain_ref[...]).astype(BF16)

    hs = [_modulate(x_ref[0, r, :], g_ref[...], mod[0:1], mod[1:2]).astype(BF16) for r in parts]
    c_kv = [_dot(h, wdkv_ref[...]) for h in hs]
    kpe = [_dot(h, wdpe_ref[...]) for h in hs]
    if want_q:
        c_q = [_dot(h, wdq_ref[...]) for h in hs]
    c_kv = [normed(c, kvln_ref) for c in c_kv]
    for c, r in zip(c_kv, parts):
        vt_ref[0, :, r] = _dot_nt(wuvt_ref[...], c).astype(BF16)
    kn_all = [_dot(c, wuk_ref[...]) for c in c_kv]
    k_rope = [rope_part(p[:, :128], p[:, 128:], rope_tables(kn, r)).astype(BF16) for p, r in zip(kpe, parts)]
    if want_q:
        c_q = [normed(c, qln_ref) for c in c_q]
        qm = [_dot(c, wuq_ref[...]) for c in c_q]
        qs = [_dot(c, wuqs_ref[...]) for c in c_q]
    for i, r in enumerate(parts):
        if want_q:
            q_tables = rope_tables(qn, r)
        for hd in range(MLA_HEADS):
            k_nope = kn_all[i][:, hd * MLA_NOPE:(hd + 1) * MLA_NOPE]
            k_ref[0, hd, r, :MLA_NOPE] = (k_nope * _rms_scale(k_nope, MLA_NOPE) * kn[0:1]).astype(BF16)
            k_ref[0, hd, r, MLA_NOPE:] = k_rope[i]
            if want_q:
                base = hd * MLA_QPAD
                q_nope = qm[i][:, base:base + MLA_NOPE]
                q_ref[0, hd, r, :MLA_NOPE] = (q_nope * _rms_scale(q_nope, MLA_NOPE) * qn[0:1]).astype(BF16)
                q_rope = rope_part(qm[i][:, base + MLA_NOPE:base + MLA_QPAD],
                                   qs[i][:, hd * 128:(hd + 1) * 128], q_tables)
                q_ref[0, hd, r, MLA_NOPE:] = q_rope.astype(BF16)


def _mla_proj(x, mod, norm_g, cos, sin, w, *, want_q, tm):
    b, l, d = x.shape
    consts = [w["wdq"], w["wdkv"], w["wdpe"], w["qln"], w["kvln"], w["wuq"], w["wuqs"], w["wuk"],
              w["wuvt"], w["qn"], w["kn"]]
    hspec = pl.BlockSpec((1, MLA_HEADS, tm, MLA_QPAD), lambda i, j: (i, 0, j, 0))
    hshape = jax.ShapeDtypeStruct((b, MLA_HEADS, l, MLA_QPAD), BF16)
    out_specs = [hspec, pl.BlockSpec((1, MLA_HEADS * MLA_V, tm), lambda i, j: (i, 0, j))]
    out_shape = [hshape, jax.ShapeDtypeStruct((b, MLA_HEADS * MLA_V, l), BF16)]
    if want_q:
        out_specs = [hspec] + out_specs
        out_shape = [hshape] + out_shape
    return pl.pallas_call(
        functools.partial(_mla_proj_kernel, want_q=want_q),
        grid=(b, l // tm),
        in_specs=[
            pl.BlockSpec((1, tm, d), lambda i, j: (i, j, 0)),
            pl.BlockSpec((1, 6, d), lambda i, j: (i, 0, 0)),
            _const_spec(norm_g.shape),
            pl.BlockSpec((tm, 128), lambda i, j: (j, 0)),
            pl.BlockSpec((tm, 128), lambda i, j: (j, 0)),
        ] + [_const_spec(a.shape) for a in consts],
        out_specs=out_specs,
        out_shape=out_shape,
        compiler_params=_cparams("parallel", "parallel"),
        name="mla_proj_q" if want_q else "mla_proj_kv",
    )(x, mod, norm_g, cos, sin, *consts)


def _attn_kernel(*refs, with_latent, tk):
    if with_latent:
        q_ref, kc_ref, vtc_ref, k_ref, vt_ref, ot_ref = refs
    else:
        q_ref, kc_ref, vtc_ref, ot_ref = refs
    q = q_ref[0, 0]
    chunks = [(kc_ref, vtc_ref, 0, kc_ref.shape[2])]
    if with_latent:
        chunks += [(k_ref, vt_ref, j * tk, tk) for j in range(k_ref.shape[2] // tk)]

    def scores(idx):
        kr, _, start, size = chunks[idx]
        return _dot_nt(kr[0, 0, start:start + size, :], q)

    s_next = scores(0)
    m = denom = acc = None
    for idx, (_, vr, start, size) in enumerate(chunks):
        s = s_next
        if idx + 1 < len(chunks):
            s_next = scores(idx + 1)
        vt = vr[0, :, start:start + size]
        cmax = jnp.max(s, axis=0, keepdims=True)
        if m is None:
            m = cmax
            p = jnp.exp2(s - m)
            denom = jnp.sum(p, axis=0, keepdims=True)
            acc = _dot(vt, p.astype(BF16))
        else:
            m_new = jnp.maximum(m, cmax)
            alpha = jnp.exp2(m - m_new)
            p = jnp.exp2(s - m_new)
            denom = denom * alpha + jnp.sum(p, axis=0, keepdims=True)
            acc = acc * alpha + _dot(vt, p.astype(BF16))
            m = m_new
    ot_ref[0] = (acc * (1.0 / denom)).astype(BF16)


def _attn_bounded_kernel(q_ref, kc_ref, vtc_ref, k_ref, vt_ref, ot_ref, *, tq, tk):
    chunks = [(kc_ref, vtc_ref, 0, kc_ref.shape[2])]
    chunks += [(k_ref, vt_ref, j * tk, tk) for j in range(k_ref.shape[2] // tk)]
    for qi in range(q_ref.shape[2] // tq):
        cols = slice(qi * tq, (qi + 1) * tq)
        q = q_ref[0, 0, cols, :]

        def scores(idx):
            kr, _, start, size = chunks[idx]
            return _dot_nt(kr[0, 0, start:start + size, :], q)

        s_next = scores(0)
        acc = denom8 = None
        for idx, (_, vr, start, size) in enumerate(chunks):
            s = s_next
            if idx + 1 < len(chunks):
                s_next = scores(idx + 1)
            p = jnp.exp2(s)
            part = jnp.sum(p.reshape(size // SUBLANES, SUBLANES, p.shape[1]), axis=0)
            y = _dot(vr[0, :, start:start + size], p.astype(BF16))
            acc = y if acc is None else acc + y
            denom8 = part if denom8 is None else denom8 + part
        denom = jnp.sum(denom8, axis=0, keepdims=True)
        ot_ref[0, :, cols] = (acc * (1.0 / denom)).astype(BF16)


def _attention(q, kc, vtc, k=None, vt=None, *, tq, tk=512, bounded_tq=None):
    b, nh, lq, dq = q.shape
    lc = kc.shape[2]
    dv = vtc.shape[1] // nh
    with_latent = k is not None
    in_specs = [
        pl.BlockSpec((1, 1, tq, dq), lambda i, h, j: (i, h, j, 0)),
        pl.BlockSpec((1, 1, lc, dq), lambda i, h, j: (i, h, 0, 0)),
        pl.BlockSpec((1, dv, lc), lambda i, h, j: (i, h, 0)),
    ]
    args = [q, kc, vtc]
    if with_latent:
        lk = k.shape[2]
        in_specs += [pl.BlockSpec((1, 1, lk, dq), lambda i, h, j: (i, h, 0, 0)),
                     pl.BlockSpec((1, dv, lk), lambda i, h, j: (i, h, 0))]
        args += [k, vt]
    if bounded_tq is not None:
        body = functools.partial(_attn_bounded_kernel, tq=bounded_tq, tk=tk)
        name = "mla_attn_bounded"
    else:
        body = functools.partial(_attn_kernel, with_latent=with_latent, tk=tk)
        name = "mla_attn" if with_latent else "mla_attn_ctx"
    return pl.pallas_call(
        body,
        grid=(b, nh, lq // tq),
        in_specs=in_specs,
        out_specs=pl.BlockSpec((1, dv, tq), lambda i, h, j: (i, h, j)),
        out_shape=jax.ShapeDtypeStruct((b, nh * dv, lq), BF16),
        compiler_params=_cparams("parallel", "parallel", "parallel"),
        name=name,
    )(*args)


def _latent_attention(bounded, q, kc, vtc, k, vt):
    lq = q.shape[2]
    lk = k.shape[2]
    fast = functools.partial(_attention, tq=min(lq, 4096), tk=min(lk, 2048), bounded_tq=min(lq, 2048))
    exact = functools.partial(_attention, tq=min(lq, 512), tk=min(lk, 512))
    return lax.cond(bounded, fast, exact, q, kc, vtc, k, vt)


def _out_proj_kernel(ot_ref, x_ref, mod_ref, w_ref, xo_ref):
    gate = mod_ref[0][2:3]
    xo_ref[0] = x_ref[0] + gate * _dot_tn(ot_ref[0], w_ref[...])


def _out_proj(ot, x, mod, w_out, *, tm):
    b, l, d = x.shape
    row = lambda i, j: (i, j, 0)
    return pl.pallas_call(
        _out_proj_kernel,
        grid=(b, l // tm),
        in_specs=[
            pl.BlockSpec((1, ot.shape[1], tm), lambda i, j: (i, 0, j)),
            pl.BlockSpec((1, tm, d), row),
            pl.BlockSpec((1, 6, d), lambda i, j: (i, 0, 0)),
            _const_spec(w_out.shape),
        ],
        out_specs=pl.BlockSpec((1, tm, d), row),
        out_shape=jax.ShapeDtypeStruct(x.shape, F32),
        compiler_params=_cparams("parallel", "parallel"),
        name="mla_out",
    )(ot, x, mod, w_out)


def _ffn_kernel(xm_ref, xp_ref, xn_ref, mod_ref, g_ref, wa_ref, wg_ref, ca_ref, cg_ref, wd_ref,
                xo_ref, h_scr, ua_scr, ug_scr, act_scr, *, tm, fc):
    i = pl.program_id(1)
    mod = mod_ref[0]
    g = g_ref[...]
    hal = CONV_HALO

    def hidden(x):
        return _modulate(x, g, mod[3:4], mod[4:5])

    h_scr[0:tm] = hidden(xm_ref[0]).astype(BF16)
    nxt = jnp.where(i == pl.num_programs(1) - 1, 0.0, hidden(xn_ref[0]))
    prv = jnp.where(i == 0, 0.0, hidden(xp_ref[0]))
    h_scr[tm:] = jnp.concatenate([nxt, prv], axis=0).astype(BF16)

    def put(u_scr, slab, u):
        u_scr[slab, hal:hal + tm, :] = u[0:tm]
        u_scr[slab, hal + tm:, :] = u[tm:tm + hal]
        u_scr[slab, 0:hal, :] = u[tm + hal:]

    def conv(u_scr, slab, cw):
        return (u_scr[slab, hal - 1:hal - 1 + tm, :] * cw[0:1] + u_scr[slab, hal:hal + tm, :] * cw[1:2]
                + u_scr[slab, hal + 1:hal + 1 + tm, :] * cw[2:3] + cw[3:4])

    per = fc // LANES
    for ci in range(wa_ref.shape[1] // fc):
        cols = slice(ci * fc, (ci + 1) * fc)
        ua = _dot(h_scr[...], wa_ref[:, cols])
        ug = _dot(h_scr[...], wg_ref[:, cols])
        for s in range(per):
            put(ua_scr, ci * per + s, ua[:, s * LANES:(s + 1) * LANES])
            put(ug_scr, ci * per + s, ug[:, s * LANES:(s + 1) * LANES])
        for s in range(per):
            slab = ci * per + s
            lanes = slice(slab * LANES, (slab + 1) * LANES)
            a = conv(ua_scr, slab, ca_ref[:, lanes])
            gt = conv(ug_scr, slab, cg_ref[:, lanes])
            act_scr[:, lanes] = (gt * _sigmoid(gt) * a).astype(BF16)
    xo_ref[0] = xm_ref[0] + mod[5:6] * _dot(act_scr[...], wd_ref[...])


def _conv_ffn(x, mod, norm_g, wa, wg, ca, cg, wd, *, tm, fc):
    b, l, d = x.shape
    hal = CONV_HALO
    nt = l // tm
    per = tm // hal
    last = l // hal - 1
    d_ff = wa.shape[1]
    return pl.pallas_call(
        functools.partial(_ffn_kernel, tm=tm, fc=fc),
        grid=(b, nt),
        in_specs=[
            pl.BlockSpec((1, tm, d), lambda i, j: (i, j, 0)),
            pl.BlockSpec((1, hal, d), lambda i, j: (i, jnp.maximum(j * per - 1, 0), 0)),
            pl.BlockSpec((1, hal, d), lambda i, j: (i, jnp.minimum((j + 1) * per, last), 0)),
            pl.BlockSpec((1, 6, d), lambda i, j: (i, 0, 0)),
            _const_spec(norm_g.shape),
            _const_spec(wa.shape),
            _const_spec(wg.shape),
            _const_spec(ca.shape),
            _const_spec(cg.shape),
            _const_spec(wd.shape),
        ],
        out_specs=pl.BlockSpec((1, tm, d), lambda i, j: (i, j, 0)),
        out_shape=jax.ShapeDtypeStruct(x.shape, F32),
        scratch_shapes=[
            pltpu.VMEM((tm + 2 * hal, d), BF16),
            pltpu.VMEM((d_ff // LANES, tm + 2 * hal, LANES), F32),
            pltpu.VMEM((d_ff // LANES, tm + 2 * hal, LANES), F32),
            pltpu.VMEM((tm, d_ff), BF16),
        ],
        compiler_params=_cparams("parallel", "parallel"),
        name="conv_ffn",
    )(x, x, x, mod, norm_g, wa, wg, ca, cg, wd)


def _rope_tables(length):
    nf = MLA_ROPE // 4
    t = jnp.arange(length)
    pos = jnp.stack([(t // GRID_W).astype(F32), (t % GRID_W).astype(F32)], axis=1)
    inv = ROPE_THETA ** (-jnp.arange(nf, dtype=F32) / nf)
    ang = pos[:, :, None] * inv
    cos = jnp.cos(ang)[:, :, None, :] * jnp.ones((1, 1, 2, 1), F32)
    sin = jnp.sin(ang)[:, :, None, :] * jnp.array([-1.0, 1.0], F32)[None, None, :, None]
    pad = lambda a: jnp.pad(a.reshape(length, MLA_ROPE), ((0, 0), (0, 128 - MLA_ROPE)))
    return pad(cos), pad(sin)


def _rope_partner(a):
    nf = MLA_ROPE // 4
    r = a.reshape(a.shape[:-1] + (2, 2, nf))
    return jnp.flip(r, axis=-2).reshape(a.shape)


def _pad_last(a, width):
    return jnp.pad(a, [(0, 0)] * (a.ndim - 1) + [(0, width - a.shape[-1])])


def _mla_weights(w_down, q_lora_norm, kv_lora_norm, w_uq, w_ukv, q_norm, k_norm):
    q_rank = q_lora_norm.shape[0]
    kv_rank = kv_lora_norm.shape[0]
    qk = MLA_NOPE + MLA_ROPE
    w_pe = w_down[:, q_rank + kv_rank:]
    wuq = w_uq.reshape(q_rank, MLA_HEADS, qk)
    wuq_rope = wuq[..., MLA_NOPE:]
    wukv = w_ukv.reshape(kv_rank, MLA_HEADS, MLA_NOPE + MLA_V)

    def max_sq_norm(g):
        return MLA_NOPE * jnp.max(g[:MLA_NOPE] ** 2) + MLA_ROPE * jnp.max(g[MLA_NOPE:] ** 2)

    bound = jnp.sqrt(max_sq_norm(q_norm) * max_sq_norm(k_norm)) * (qk ** -0.5 * LOG2_E * 1.02)

    def gains(g):
        rope = g[MLA_NOPE:]
        return jnp.stack([g[:MLA_NOPE], _pad_last(rope, 128), _pad_last(_rope_partner(rope), 128)])

    return {
        "bounded": bound <= ATTN_MAX_SCORE,
        "wdq": w_down[:, :q_rank].astype(BF16),
        "wdkv": w_down[:, q_rank:q_rank + kv_rank].astype(BF16),
        "wdpe": jnp.concatenate([_pad_last(w_pe, 128), _pad_last(_rope_partner(w_pe), 128)],
                                axis=1).astype(BF16),
        "qln": q_lora_norm[None, :],
        "kvln": kv_lora_norm[None, :],
        "wuq": _pad_last(wuq, MLA_QPAD).reshape(q_rank, MLA_HEADS * MLA_QPAD).astype(BF16),
        "wuqs": _pad_last(_rope_partner(wuq_rope), 128).reshape(q_rank, MLA_HEADS * 128).astype(BF16),
        "wuk": wukv[..., :MLA_NOPE].reshape(kv_rank, MLA_HEADS * MLA_NOPE).astype(BF16),
        "wuvt": wukv[..., MLA_NOPE:].reshape(kv_rank, MLA_HEADS * MLA_V).T.astype(BF16),
        "qn": gains(q_norm),
        "kn": gains(k_norm),
    }


def kernel(x, c, ctx, c_ctx, w_ada, b_ada, norm_mix, norm_ffn, gla_w_in, gla_gate_w1, gla_gate_w2, gla_gate_b, gla_out_norm, gla_w_out, mla_w_down, mla_q_lora_norm, mla_kv_lora_norm, mla_w_uq, mla_w_ukv, mla_q_norm, mla_k_norm, mla_w_out, ffn_w_up, ffn_conv_w, ffn_conv_b, ffn_w_down):
    bsz, seq, d = x.shape
    lc = ctx.shape[1]
    depth = w_ada.shape[0]
    d_ff = ffn_w_down.shape[1]
    dqk = gla_gate_w2.shape[-1]
    rank = gla_gate_w1.shape[-1]
    dv = gla_out_norm.shape[-1]

    cond_rows = -(-(bsz + 1) // 8) * 8
    cond = jnp.zeros((cond_rows, d), F32).at[:bsz].set(c).at[bsz].set(c_ctx)
    mods = _ada_all(cond, w_ada, b_ada)

    cos_l, sin_l = _rope_tables(seq)
    cos_c = _pad_last(jnp.ones((lc, MLA_ROPE), F32), 128)
    sin_c = jnp.zeros((lc, 128), F32)

    tm_l = min(seq, 512)
    tm_c = min(lc, 512)
    xc = ctx
    for i in range(depth):
        last = i == depth - 1
        j = i // 2
        mod_l = mods[i, :bsz].reshape(bsz, 6, d)
        mod_c = jnp.broadcast_to(mods[i, bsz].reshape(1, 6, d), (bsz, 6, d))
        g_mix = norm_mix[i][None, :]
        if i % 2 == 0:
            w_in = gla_w_in[j].astype(BF16)
            w1 = _pad_last(jnp.concatenate([gla_gate_w1[j, 0], gla_gate_w1[j, 1]], axis=1),
                           GLA_RANK_PAD).astype(BF16)
            w2 = jnp.zeros((GLA_RANK_PAD, 2 * dqk), F32)
            w2 = w2.at[:rank, :dqk].set(gla_gate_w2[j, 0]).at[rank:2 * rank, dqk:].set(gla_gate_w2[j, 1])
            w2 = w2.astype(BF16)
            gb = gla_gate_b[j].reshape(1, 2 * dqk)
            gn = gla_out_norm[j][None, :]
            w_out = gla_w_out[j].astype(BF16)
            s0 = jnp.zeros((bsz, GLA_HEADS, dv, dqk // GLA_HEADS), F32)

            proj_c = _gla_proj(xc, mod_c, g_mix, w_in, w1, w2, gb, tm=tm_c)
            proj_l = _gla_proj(x, mod_l, g_mix, w_in, w1, w2, gb, tm=tm_l)
            vc, rc, fwd_c, bwd_c = proj_c[0], proj_c[1], proj_c[2:5], proj_c[5:8]
            v, r, fwd_l, bwd_l = proj_l[0], proj_l[1], proj_l[2:5], proj_l[5:8]
            if last:
                _, s_fwd = _gla_scan(*fwd_c[:2], vc, fwd_c[2], s0, direction=0, want_o=False)
                _, s_bwd = _gla_scan(*bwd_c[:2], vc, bwd_c[2], s0, direction=1, want_o=False)
            else:
                oc_b, s_bwd = _gla_scan(*bwd_c[:2], vc, bwd_c[2], s0, direction=1)
                xc, s_fwd = _gla_scan(*fwd_c[:2], vc, fwd_c[2], s0, direction=0,
                                      fused=(oc_b, rc, xc, mod_c, gn, w_out))
            o_b, _ = _gla_scan(*bwd_l[:2], v, bwd_l[2], s_bwd, direction=1)
            x, _ = _gla_scan(*fwd_l[:2], v, fwd_l[2], s_fwd, direction=0,
                             fused=(o_b, r, x, mod_l, gn, w_out))
        else:
            w = _mla_weights(mla_w_down[j], mla_q_lora_norm[j], mla_kv_lora_norm[j], mla_w_uq[j],
                             mla_w_ukv[j], mla_q_norm[j], mla_k_norm[j])
            w_out = mla_w_out[j].astype(BF16)
            if last:
                kc, vtc = _mla_proj(xc, mod_c, g_mix, cos_c, sin_c, w, want_q=False, tm=tm_c)
            else:
                qc, kc, vtc = _mla_proj(xc, mod_c, g_mix, cos_c, sin_c, w, want_q=True, tm=tm_c)
            q, k, vt = _mla_proj(x, mod_l, g_mix, cos_l, sin_l, w, want_q=True, tm=min(seq, 1024))
            ot = _latent_attention(w["bounded"], q, kc, vtc, k, vt)
            x = _out_proj(ot, x, mod_l, w_out, tm=min(seq, 1024))
            if not last:
                otc = _attention(qc, kc, vtc, tq=min(lc, 512))
                xc = _out_proj(otc, xc, mod_c, w_out, tm=tm_c)

        g_ffn = norm_ffn[i][None, :]
        wa = ffn_w_up[i, :, :d_ff].astype(BF16)
        wg = ffn_w_up[i, :, d_ff:].astype(BF16)
        conv = jnp.concatenate([ffn_conv_w[i], ffn_conv_b[i][None, :]], axis=0)
        ca, cg = conv[:, :d_ff], conv[:, d_ff:]
        wd = ffn_w_down[i].astype(BF16)
        x = _conv_ffn(x, mod_l, g_ffn, wa, wg, ca, cg, wd, tm=tm_l, fc=FFN_COLS)
        if not last:
            xc = _conv_ffn(xc, mod_c, g_ffn, wa, wg, ca, cg, wd, tm=tm_c, fc=FFN_COLS)
    return x
```

```python
import functools

import jax
import jax.numpy as jnp
from jax import lax
from jax.experimental import pallas as pl
from jax.experimental.pallas import tpu as pltpu

F32 = jnp.float32
BF16 = jnp.bfloat16

LANES = 128
SUBLANES = 8
BF16_ROWS = 16
EPS = 1e-6
GRID_W = 64
ROPE_THETA = 10000.0
LOG2_E = 1.4426950408889634

GLA_HEADS = 4
GLA_GATE_NORMALIZER = 16.0
GLA_CHUNK = 128
GLA_RANK_PAD = 128

MLA_HEADS = 8
MLA_NOPE = 128
MLA_ROPE = 64
MLA_V = 128
MLA_QPAD = 256
MLA_PROJ_ROWS = 256
ATTN_MAX_SCORE = 50.0

FFN_COLS = 256
CONV_HALO = SUBLANES
VMEM_LIMIT = 56 * 1024 * 1024


def _cparams(*sem):
    return pltpu.CompilerParams(dimension_semantics=sem, vmem_limit_bytes=VMEM_LIMIT)


def _const_spec(shape):
    nd = len(shape)
    return pl.BlockSpec(shape, lambda *_: (0,) * nd, pipeline_mode=pl.Buffered(1))


def _dot(a, b):
    return jnp.dot(a, b, preferred_element_type=F32)


def _dot_nt(a, b):
    return lax.dot_general(a, b, (((1,), (1,)), ((), ())), preferred_element_type=F32)


def _dot_tn(a, b):
    return lax.dot_general(a, b, (((0,), (0,)), ((), ())), preferred_element_type=F32)


def _sigmoid(x):
    return 1.0 / (1.0 + jnp.exp(-x))


def _rms_scale(x, width):
    ss = jnp.sum(x * x, axis=-1, keepdims=True)
    return lax.rsqrt(ss * (1.0 / width) + EPS)


def _modulate(x, g, shift, scale):
    return x * _rms_scale(x, x.shape[-1]) * (g * (1.0 + scale)) + shift


def _ada_kernel(cond_ref, w_ref, b_ref, o_ref):
    cond = cond_ref[...]
    s = cond * _sigmoid(cond)
    o_ref[0] = jnp.dot(s, w_ref[0], preferred_element_type=F32,
                       precision=lax.Precision.HIGHEST) + b_ref[0]


def _ada_all(cond, w_ada, b_ada):
    depth, d, d6 = w_ada.shape
    rows = cond.shape[0]
    return pl.pallas_call(
        _ada_kernel,
        grid=(depth, d6 // d),
        in_specs=[
            pl.BlockSpec((rows, d), lambda i, j: (0, 0)),
            pl.BlockSpec((1, d, d), lambda i, j: (i, 0, j)),
            pl.BlockSpec((1, 1, d), lambda i, j: (i, 0, j)),
        ],
        out_specs=pl.BlockSpec((1, rows, d), lambda i, j: (i, 0, j)),
        out_shape=jax.ShapeDtypeStruct((depth, rows, d6), F32),
        compiler_params=_cparams("parallel", "parallel"),
        name="ada_mod",
    )(cond, w_ada, b_ada.reshape(depth, 1, d6))


def _gla_proj_kernel(x_ref, mod_ref, g_ref, win_ref, w1_ref, w2_ref, gb_ref,
                     v_ref, r_ref, qf_ref, kf_ref, df_ref, qb_ref, kb_ref, db_ref, *, dqk):
    mod = mod_ref[0]
    h = _modulate(x_ref[0], g_ref[...], mod[0:1], mod[1:2]).astype(BF16)
    c = GLA_CHUNK
    dk = dqk // GLA_HEADS
    dv = v_ref.shape[2]
    low = _dot(h, w1_ref[...]).astype(BF16)
    z = _dot(low, w2_ref[...]) + gb_ref[...]
    pqk = _dot(h, win_ref[:, :2 * dqk])
    unit = LOG2_E / GLA_GATE_NORMALIZER
    gates = jnp.minimum(z, 0.0) * unit - jnp.log(1.0 + jnp.exp(-jnp.abs(z))) * unit
    row = lax.broadcasted_iota(jnp.int32, (c, c), 0)
    col = lax.broadcasted_iota(jnp.int32, (c, c), 1)
    for direction, (q_ref, k_ref, d_ref) in enumerate(((qf_ref, kf_ref, df_ref), (qb_ref, kb_ref, db_ref))):
        reverse = direction == 1
        tri = ((col >= row) if reverse else (col <= row)).astype(BF16)
        tri2 = jnp.concatenate([tri, tri], axis=1)
        i_mid = c // 2 if reverse else c // 2 - 1
        i_end = 0 if reverse else c - 1
        g = gates[:, direction * dqk:(direction + 1) * dqk]
        g_hi = g.astype(BF16)
        g_lo = (g - g_hi.astype(F32)).astype(BF16)
        for ci in range(x_ref.shape[1] // c):
            rows = slice(ci * c, (ci + 1) * c)
            cum = _dot(tri2, jnp.concatenate([g_hi[rows], g_lo[rows]], axis=0))
            c_mid = cum[i_mid:i_mid + 1, :]
            c_end = cum[i_end:i_end + 1, :]
            rel = cum - c_mid
            q_in = pqk[rows, :dqk] * (jnp.exp2(rel) * (dk ** -0.5))
            k_in = pqk[rows, dqk:] * jnp.exp2(-rel)
            q_ref[0, rows, :dqk] = q_in.astype(BF16)
            q_ref[0, rows, dqk:] = (q_in * jnp.exp2(c_mid)).astype(BF16)
            k_ref[0, rows, :dqk] = k_in.astype(BF16)
            k_ref[0, rows, dqk:] = (k_in * jnp.exp2(c_end - c_mid)).astype(BF16)
            d_ref[0, ci * SUBLANES:(ci + 1) * SUBLANES, :] = jnp.broadcast_to(jnp.exp2(c_end), (SUBLANES, dqk))
    pvr = _dot(h, win_ref[:, 2 * dqk:])
    v_ref[0] = pvr[:, :dv].astype(BF16)
    r_ref[0] = pvr[:, dv:].astype(BF16)


def _gla_proj(x, mod, norm_g, w_in, w1, w2, gb, *, tm):
    b, l, d = x.shape
    dqk = w2.shape[1] // 2
    dv = (w_in.shape[1] - 2 * dqk) // 2
    drows = tm // GLA_CHUNK * SUBLANES
    row = lambda i, j: (i, j, 0)
    tok = lambda width, dtype: (pl.BlockSpec((1, tm, width), row), jax.ShapeDtypeStruct((b, l, width), dtype))
    dec = (pl.BlockSpec((1, drows, dqk), row),
           jax.ShapeDtypeStruct((b, l // GLA_CHUNK * SUBLANES, dqk), F32))
    outs = [tok(dv, BF16), tok(dv, BF16)] + 2 * [tok(2 * dqk, BF16), tok(2 * dqk, BF16), dec]
    return pl.pallas_call(
        functools.partial(_gla_proj_kernel, dqk=dqk),
        grid=(b, l // tm),
        in_specs=[
            pl.BlockSpec((1, tm, d), row),
            pl.BlockSpec((1, 6, d), lambda i, j: (i, 0, 0)),
            _const_spec(norm_g.shape),
            _const_spec(w_in.shape),
            _const_spec(w1.shape),
            _const_spec(w2.shape),
            _const_spec(gb.shape),
        ],
        out_specs=[o[0] for o in outs],
        out_shape=[o[1] for o in outs],
        compiler_params=_cparams("parallel", "parallel"),
        name="gla_proj",
    )(x, mod, norm_g, w_in, w1, w2, gb)


def _gla_scan_kernel(*refs, reverse, want_o, fuse_out, tblk):
    if fuse_out:
        (q_ref, k_ref, v_ref, d_ref, s0_ref, ob_ref, r_ref, x_ref, mod_ref, gn_ref, wout_ref,
         xo_ref, st_ref, y_scr) = refs
    elif want_o:
        q_ref, k_ref, v_ref, d_ref, s0_ref, o_ref, st_ref = refs
    else:
        q_ref, k_ref, v_ref, d_ref, s0_ref, st_ref = refs

    @pl.when(pl.program_id(1) == 0)
    def _():
        st_ref[...] = s0_ref[...]

    c = GLA_CHUNK
    dqk = q_ref.shape[2] // 2
    dk = dqk // GLA_HEADS
    dv = v_ref.shape[2] // GLA_HEADS
    row = lax.broadcasted_iota(jnp.int32, (c, c), 0)
    col = lax.broadcasted_iota(jnp.int32, (c, c), 1)
    mask = (col >= row) if reverse else (col <= row)

    nch = tblk // c
    states = [st_ref[0, h] for h in range(GLA_HEADS)]
    for ci in (reversed(range(nch)) if reverse else range(nch)):
        rows = slice(ci * c, (ci + 1) * c)
        for h in range(GLA_HEADS):
            intra = slice(h * dk, (h + 1) * dk)
            state = slice(dqk + h * dk, dqk + (h + 1) * dk)
            vs = slice(h * dv, (h + 1) * dv)
            v = v_ref[0, rows, vs]
            st = states[h]
            if want_o:
                scores = jnp.where(mask, _dot_nt(q_ref[0, rows, intra], k_ref[0, rows, intra]), 0.0)
                o = _dot(scores.astype(BF16), v) + _dot_nt(q_ref[0, rows, state], st.astype(BF16))
                if fuse_out:
                    o = o + ob_ref[0, rows, vs].astype(F32)
                    r = r_ref[0, rows, vs].astype(F32)
                    y = o * _rms_scale(o, dv) * gn_ref[...] * (r * _sigmoid(r))
                    y_scr[rows, vs] = y.astype(BF16)
                else:
                    o_ref[0, rows, vs] = o.astype(BF16)
            decay_end = d_ref[0, ci * SUBLANES:ci * SUBLANES + 1, intra]
            states[h] = st * decay_end + _dot_tn(v, k_ref[0, rows, state])
    for h in range(GLA_HEADS):
        st_ref[0, h] = states[h]

    if fuse_out:
        gate = mod_ref[0][2:3]
        xo_ref[0] = x_ref[0] + gate * _dot(y_scr[...], wout_ref[...])


def _gla_scan(q, k, v, decay, s0, *, direction, want_o=True, fused=None):
    b, l, dq2 = q.shape
    dvt = v.shape[2]
    tblk = min(l, 1024)
    nblk = l // tblk
    reverse = direction == 1
    blk = (lambda i, j: (i, nblk - 1 - j, 0)) if reverse else (lambda i, j: (i, j, 0))
    st_spec = pl.BlockSpec((1,) + s0.shape[1:], lambda i, j: (i, 0, 0, 0))
    in_specs = [
        pl.BlockSpec((1, tblk, dq2), blk),
        pl.BlockSpec((1, tblk, dq2), blk),
        pl.BlockSpec((1, tblk, dvt), blk),
        pl.BlockSpec((1, tblk // GLA_CHUNK * SUBLANES, decay.shape[2]), blk),
        st_spec,
    ]
    args = [q, k, v, decay, s0]
    st_shape = jax.ShapeDtypeStruct(s0.shape, F32)
    scratch = []
    if fused is not None:
        o_other, r, x, mod, gn, w_out = fused
        d = x.shape[2]
        in_specs += [
            pl.BlockSpec((1, tblk, dvt), blk),
            pl.BlockSpec((1, tblk, dvt), blk),
            pl.BlockSpec((1, tblk, d), blk),
            pl.BlockSpec((1, 6, d), lambda i, j: (i, 0, 0)),
            _const_spec(gn.shape),
            _const_spec(w_out.shape),
        ]
        args += [o_other, r, x, mod, gn, w_out]
        out_specs = [pl.BlockSpec((1, tblk, d), blk), st_spec]
        out_shape = [jax.ShapeDtypeStruct(x.shape, F32), st_shape]
        scratch = [pltpu.VMEM((tblk, dvt), BF16)]
    elif want_o:
        out_specs = [pl.BlockSpec((1, tblk, dvt), blk), st_spec]
        out_shape = [jax.ShapeDtypeStruct((b, l, dvt), BF16), st_shape]
    else:
        out_specs = [st_spec]
        out_shape = [st_shape]
    outs = pl.pallas_call(
        functools.partial(_gla_scan_kernel, reverse=reverse, want_o=want_o,
                          fuse_out=fused is not None, tblk=tblk),
        grid=(b, nblk),
        in_specs=in_specs,
        out_specs=out_specs,
        out_shape=out_shape,
        scratch_shapes=scratch,
        compiler_params=_cparams("parallel", "arbitrary"),
        name="gla_scan_%s%s" % ("bwd" if reverse else "fwd", "_out" if fused is not None else ""),
    )(*args)
    if want_o:
        return outs[0], outs[1]
    return None, outs[0]


def _mla_proj_kernel(*refs, want_q):
    (x_ref, mod_ref, g_ref, cos_ref, sin_ref, wdq_ref, wdkv_ref, wdpe_ref, qln_ref, kvln_ref,
     wuq_ref, wuqs_ref, wuk_ref, wuvt_ref, qn_ref, kn_ref) = refs[:16]
    if want_q:
        q_ref, k_ref, vt_ref = refs[16:]
    else:
        k_ref, vt_ref = refs[16:]
    mod = mod_ref[0]
    kn = kn_ref[...]
    qn = qn_ref[...] * ((MLA_NOPE + MLA_ROPE) ** -0.5 * LOG2_E)
    sub = min(x_ref.shape[1], MLA_PROJ_ROWS)
    parts = [slice(i * sub, (i + 1) * sub) for i in range(x_ref.shape[1] // sub)]

    def rope_tables(gains, rows):
        return cos_ref[rows, :] * gains[1:2], sin_ref[rows, :] * gains[2:3]

    def rope_part(raw, raw_partner, tables):
        return (raw * tables[0] + raw_partner * tables[1]) * _rms_scale(raw, MLA_ROPE)

    def normed(c, gain_ref):
        return (c * _rms_scale(c, c.shape[-1]) * gain_ref[...]).astype(BF16)

    hs = [_modulate(x_ref[0, r, :], g_ref[...], mod[0:1], mod[1:2]).astype(BF16) for r in parts]
    c_kv = [_dot(h, wdkv_ref[...]) for h in hs]
    kpe = [_dot(h, wdpe_ref[...]) for h in hs]
    if want_q:
        c_q = [_dot(h, wdq_ref[...]) for h in hs]
    c_kv = [normed(c, kvln_ref) for c in c_kv]
    for c, r in zip(c_kv, parts):
        vt_ref[0, :, r] = _dot_nt(wuvt_ref[...], c).astype(BF16)
    kn_all = [_dot(c, wuk_ref[...]) for c in c_kv]
    k_rope = [rope_part(p[:, :128], p[:, 128:], rope_tables(kn, r)).astype(BF16) for p, r in zip(kpe, parts)]
    if want_q:
        c_q = [normed(c, qln_ref) for c in c_q]
        qm = [_dot(c, wuq_ref[...]) for c in c_q]
        qs = [_dot(c, wuqs_ref[...]) for c in c_q]
    for i, r in enumerate(parts):
        if want_q:
            q_tables = rope_tables(qn, r)
        for hd in range(MLA_HEADS):
            k_nope = kn_all[i][:, hd * MLA_NOPE:(hd + 1) * MLA_NOPE]
            k_ref[0, hd, r, :MLA_NOPE] = (k_nope * _rms_scale(k_nope, MLA_NOPE) * kn[0:1]).astype(BF16)
            k_ref[0, hd, r, MLA_NOPE:] = k_rope[i]
            if want_q:
                base = hd * MLA_QPAD
                q_nope = qm[i][:, base:base + MLA_NOPE]
                q_ref[0, hd, r, :MLA_NOPE] = (q_nope * _rms_scale(q_nope, MLA_NOPE) * qn[0:1]).astype(BF16)
                q_rope = rope_part(qm[i][:, base + MLA_NOPE:base + MLA_QPAD],
                                   qs[i][:, hd * 128:(hd + 1) * 128], q_tables)
                q_ref[0, hd, r, MLA_NOPE:] = q_rope.astype(BF16)


def _mla_proj(x, mod, norm_g, cos, sin, w, *, want_q, tm):
    b, l, d = x.shape
    consts = [w["wdq"], w["wdkv"], w["wdpe"], w["qln"], w["kvln"], w["wuq"], w["wuqs"], w["wuk"],
              w["wuvt"], w["qn"], w["kn"]]
    hspec = pl.BlockSpec((1, MLA_HEADS, tm, MLA_QPAD), lambda i, j: (i, 0, j, 0))
    hshape = jax.ShapeDtypeStruct((b, MLA_HEADS, l, MLA_QPAD), BF16)
    out_specs = [hspec, pl.BlockSpec((1, MLA_HEADS * MLA_V, tm), lambda i, j: (i, 0, j))]
    out_shape = [hshape, jax.ShapeDtypeStruct((b, MLA_HEADS * MLA_V, l), BF16)]
    if want_q:
        out_specs = [hspec] + out_specs
        out_shape = [hshape] + out_shape
    return pl.pallas_call(
        functools.partial(_mla_proj_kernel, want_q=want_q),
        grid=(b, l // tm),
        in_specs=[
            pl.BlockSpec((1, tm, d), lambda i, j: (i, j, 0)),
            pl.BlockSpec((1, 6, d), lambda i, j: (i, 0, 0)),
            _const_spec(norm_g.shape),
            pl.BlockSpec((tm, 128), lambda i, j: (j, 0)),
            pl.BlockSpec((tm, 128), lambda i, j: (j, 0)),
        ] + [_const_spec(a.shape) for a in consts],
        out_specs=out_specs,
        out_shape=out_shape,
        compiler_params=_cparams("parallel", "parallel"),
        name="mla_proj_q" if want_q else "mla_proj_kv",
    )(x, mod, norm_g, cos, sin, *consts)


def _attn_kernel(*refs, with_latent, tk):
    if with_latent:
        q_ref, kc_ref, vtc_ref, k_ref, vt_ref, o_ref = refs
    else:
        q_ref, kc_ref, vtc_ref, o_ref = refs
    q = q_ref[0, 0]
    chunks = [(kc_ref, vtc_ref, 0, kc_ref.shape[2])]
    if with_latent:
        chunks += [(k_ref, vt_ref, j * tk, tk) for j in range(k_ref.shape[2] // tk)]

    def scores(idx):
        kr, _, start, size = chunks[idx]
        return _dot_nt(kr[0, 0, start:start + size, :], q)

    s_next = scores(0)
    m = denom = acc = None
    for idx, (_, vr, start, size) in enumerate(chunks):
        s = s_next
        if idx + 1 < len(chunks):
            s_next = scores(idx + 1)
        vt = vr[0, :, start:start + size]
        cmax = jnp.max(s, axis=0, keepdims=True)
        if m is None:
            m = cmax
            p = jnp.exp2(s - m)
            denom = jnp.sum(p, axis=0, keepdims=True)
            acc = _dot(vt, p.astype(BF16))
        else:
            m_new = jnp.maximum(m, cmax)
            alpha = jnp.exp2(m - m_new)
            p = jnp.exp2(s - m_new)
            denom = denom * alpha + jnp.sum(p, axis=0, keepdims=True)
            acc = acc * alpha + _dot(vt, p.astype(BF16))
            m = m_new
    o_ref[0] = (acc * (1.0 / denom)).T.astype(BF16)


def _attn_bounded_kernel(q_ref, kc_ref, vtc_ref, k_ref, vt_ref, o_ref, *, tq, tk):
    chunks = [(kc_ref, vtc_ref, 0, kc_ref.shape[2])]
    chunks += [(k_ref, vt_ref, j * tk, tk) for j in range(k_ref.shape[2] // tk)]
    for qi in range(q_ref.shape[2] // tq):
        cols = slice(qi * tq, (qi + 1) * tq)
        q = q_ref[0, 0, cols, :]

        def scores(idx):
            kr, _, start, size = chunks[idx]
            return _dot_nt(kr[0, 0, start:start + size, :], q)

        s_next = scores(0)
        acc = denom8 = None
        for idx, (_, vr, start, size) in enumerate(chunks):
            s = s_next
            if idx + 1 < len(chunks):
                s_next = scores(idx + 1)
            p = jnp.exp2(s)
            part = jnp.sum(p.reshape(size // SUBLANES, SUBLANES, p.shape[1]), axis=0)
            y = _dot(vr[0, :, start:start + size], p.astype(BF16))
            acc = y if acc is None else acc + y
            denom8 = part if denom8 is None else denom8 + part
        denom = jnp.sum(denom8, axis=0, keepdims=True)
        o_ref[0, cols, :] = (acc * (1.0 / denom)).T.astype(BF16)


def _attention(q, kc, vtc, k=None, vt=None, *, tq, tk=512, bounded_tq=None):
    b, nh, lq, dq = q.shape
    lc = kc.shape[2]
    dv = vtc.shape[1] // nh
    with_latent = k is not None
    in_specs = [
        pl.BlockSpec((1, 1, tq, dq), lambda i, h, j: (i, h, j, 0)),
        pl.BlockSpec((1, 1, lc, dq), lambda i, h, j: (i, h, 0, 0)),
        pl.BlockSpec((1, dv, lc), lambda i, h, j: (i, h, 0)),
    ]
    args = [q, kc, vtc]
    if with_latent:
        lk = k.shape[2]
        in_specs += [pl.BlockSpec((1, 1, lk, dq), lambda i, h, j: (i, h, 0, 0)),
                     pl.BlockSpec((1, dv, lk), lambda i, h, j: (i, h, 0))]
        args += [k, vt]
    if bounded_tq is not None:
        body = functools.partial(_attn_bounded_kernel, tq=bounded_tq, tk=tk)
        name = "mla_attn_bounded"
    else:
        body = functools.partial(_attn_kernel, with_latent=with_latent, tk=tk)
        name = "mla_attn" if with_latent else "mla_attn_ctx"
    return pl.pallas_call(
        body,
        grid=(b, nh, lq // tq),
        in_specs=in_specs,
        out_specs=pl.BlockSpec((1, tq, dv), lambda i, h, j: (i, j, h)),
        out_shape=jax.ShapeDtypeStruct((b, lq, nh * dv), BF16),
        compiler_params=_cparams("parallel", "parallel", "parallel"),
        name=name,
    )(*args)


def _latent_attention(bounded, q, kc, vtc, k, vt):
    lq = q.shape[2]
    lk = k.shape[2]
    fast = functools.partial(_attention, tq=min(lq, 4096), tk=min(lk, 2048), bounded_tq=min(lq, 2048))
    exact = functools.partial(_attention, tq=min(lq, 512), tk=min(lk, 512))
    return lax.cond(bounded, fast, exact, q, kc, vtc, k, vt)


def _ffn_kernel(*refs, tm, fc, mixer_out):
    if mixer_out:
        (xm_ref, xp_ref, xn_ref, om_ref, op_ref, on_ref, wo_ref, mod_ref, g_ref, wa_ref, wg_ref, ca_ref,
         cg_ref, wd_ref, xo_ref, h_scr, ua_scr, ug_scr, act_scr, o_scr) = refs
    else:
        (xm_ref, xp_ref, xn_ref, mod_ref, g_ref, wa_ref, wg_ref, ca_ref, cg_ref, wd_ref,
         xo_ref, h_scr, ua_scr, ug_scr, act_scr) = refs
    i = pl.program_id(1)
    mod = mod_ref[0]
    g = g_ref[...]
    hal = CONV_HALO

    def hidden(x):
        return _modulate(x, g, mod[3:4], mod[4:5])

    x_main, x_nxt, x_prv = xm_ref[0], xn_ref[0], xp_ref[0]
    if mixer_out:
        o_scr[0:tm] = om_ref[0]
        o_nxt = on_ref[0].astype(F32)[:hal]
        o_prv = op_ref[0].astype(F32)[BF16_ROWS - hal:]
        o_scr[tm:] = jnp.concatenate([o_nxt, o_prv], axis=0).astype(BF16)
        y = _dot(o_scr[...], wo_ref[...])
        x_main = x_main + mod[2:3] * y[0:tm]
        x_nxt = x_nxt + mod[2:3] * y[tm:tm + hal]
        x_prv = x_prv + mod[2:3] * y[tm + hal:]

    h_scr[0:tm] = hidden(x_main).astype(BF16)
    nxt = jnp.where(i == pl.num_programs(1) - 1, 0.0, hidden(x_nxt))
    prv = jnp.where(i == 0, 0.0, hidden(x_prv))
    h_scr[tm:] = jnp.concatenate([nxt, prv], axis=0).astype(BF16)

    def put(u_scr, slab, u):
        u_scr[slab, hal:hal + tm, :] = u[0:tm]
        u_scr[slab, hal + tm:, :] = u[tm:tm + hal]
        u_scr[slab, 0:hal, :] = u[tm + hal:]

    def conv(u_scr, slab, cw):
        return (u_scr[slab, hal - 1:hal - 1 + tm, :] * cw[0:1] + u_scr[slab, hal:hal + tm, :] * cw[1:2]
                + u_scr[slab, hal + 1:hal + 1 + tm, :] * cw[2:3] + cw[3:4])

    per = fc // LANES
    for ci in range(wa_ref.shape[1] // fc):
        cols = slice(ci * fc, (ci + 1) * fc)
        ua = _dot(h_scr[...], wa_ref[:, cols])
        ug = _dot(h_scr[...], wg_ref[:, cols])
        for s in range(per):
            put(ua_scr, ci * per + s, ua[:, s * LANES:(s + 1) * LANES])
            put(ug_scr, ci * per + s, ug[:, s * LANES:(s + 1) * LANES])
        for s in range(per):
            slab = ci * per + s
            lanes = slice(slab * LANES, (slab + 1) * LANES)
            a = conv(ua_scr, slab, ca_ref[:, lanes])
            gt = conv(ug_scr, slab, cg_ref[:, lanes])
            act_scr[:, lanes] = (gt * _sigmoid(gt) * a).astype(BF16)
    xo_ref[0] = x_main + mod[5:6] * _dot(act_scr[...], wd_ref[...])


def _conv_ffn(x, mod, norm_g, wa, wg, ca, cg, wd, *, tm, fc, mixer_out=None):
    b, l, d = x.shape
    hal = CONV_HALO
    nt = l // tm
    d_ff = wa.shape[1]

    def halo_specs(rows, width):
        per, last = tm // rows, l // rows - 1
        return [pl.BlockSpec((1, rows, width), lambda i, j: (i, jnp.maximum(j * per - 1, 0), 0)),
                pl.BlockSpec((1, rows, width), lambda i, j: (i, jnp.minimum((j + 1) * per, last), 0))]

    in_specs = [pl.BlockSpec((1, tm, d), lambda i, j: (i, j, 0))] + halo_specs(hal, d)
    args = [x, x, x]
    scratch = [
        pltpu.VMEM((tm + 2 * hal, d), BF16),
        pltpu.VMEM((d_ff // LANES, tm + 2 * hal, LANES), F32),
        pltpu.VMEM((d_ff // LANES, tm + 2 * hal, LANES), F32),
        pltpu.VMEM((tm, d_ff), BF16),
    ]
    if mixer_out is not None:
        o, w_out = mixer_out
        do = o.shape[2]
        in_specs += [pl.BlockSpec((1, tm, do), lambda i, j: (i, j, 0))] + halo_specs(BF16_ROWS, do)
        in_specs += [_const_spec(w_out.shape)]
        args += [o, o, o, w_out]
        scratch += [pltpu.VMEM((tm + 2 * hal, do), BF16)]
    consts = [norm_g, wa, wg, ca, cg, wd]
    return pl.pallas_call(
        functools.partial(_ffn_kernel, tm=tm, fc=fc, mixer_out=mixer_out is not None),
        grid=(b, nt),
        in_specs=in_specs + [pl.BlockSpec((1, 6, d), lambda i, j: (i, 0, 0))] + [_const_spec(a.shape) for a in consts],
        out_specs=pl.BlockSpec((1, tm, d), lambda i, j: (i, j, 0)),
        out_shape=jax.ShapeDtypeStruct(x.shape, F32),
        scratch_shapes=scratch,
        compiler_params=_cparams("parallel", "parallel"),
        name="conv_ffn_mix" if mixer_out is not None else "conv_ffn",
    )(*args, mod, *consts)


def _rope_tables(length):
    nf = MLA_ROPE // 4
    t = jnp.arange(length)
    pos = jnp.stack([(t // GRID_W).astype(F32), (t % GRID_W).astype(F32)], axis=1)
    inv = ROPE_THETA ** (-jnp.arange(nf, dtype=F32) / nf)
    ang = pos[:, :, None] * inv
    cos = jnp.cos(ang)[:, :, None, :] * jnp.ones((1, 1, 2, 1), F32)
    sin = jnp.sin(ang)[:, :, None, :] * jnp.array([-1.0, 1.0], F32)[None, None, :, None]
    pad = lambda a: jnp.pad(a.reshape(length, MLA_ROPE), ((0, 0), (0, 128 - MLA_ROPE)))
    return pad(cos), pad(sin)


def _rope_partner(a):
    nf = MLA_ROPE // 4
    r = a.reshape(a.shape[:-1] + (2, 2, nf))
    return jnp.flip(r, axis=-2).reshape(a.shape)


def _pad_last(a, width):
    return jnp.pad(a, [(0, 0)] * (a.ndim - 1) + [(0, width - a.shape[-1])])


def _mla_weights(w_down, q_lora_norm, kv_lora_norm, w_uq, w_ukv, q_norm, k_norm):
    q_rank = q_lora_norm.shape[0]
    kv_rank = kv_lora_norm.shape[0]
    qk = MLA_NOPE + MLA_ROPE
    w_pe = w_down[:, q_rank + kv_rank:]
    wuq = w_uq.reshape(q_rank, MLA_HEADS, qk)
    wuq_rope = wuq[..., MLA_NOPE:]
    wukv = w_ukv.reshape(kv_rank, MLA_HEADS, MLA_NOPE + MLA_V)

    def max_sq_norm(g):
        return MLA_NOPE * jnp.max(g[:MLA_NOPE] ** 2) + MLA_ROPE * jnp.max(g[MLA_NOPE:] ** 2)

    bound = jnp.sqrt(max_sq_norm(q_norm) * max_sq_norm(k_norm)) * (qk ** -0.5 * LOG2_E * 1.02)

    def gains(g):
        rope = g[MLA_NOPE:]
        return jnp.stack([g[:MLA_NOPE], _pad_last(rope, 128), _pad_last(_rope_partner(rope), 128)])

    return {
        "bounded": bound <= ATTN_MAX_SCORE,
        "wdq": w_down[:, :q_rank].astype(BF16),
        "wdkv": w_down[:, q_rank:q_rank + kv_rank].astype(BF16),
        "wdpe": jnp.concatenate([_pad_last(w_pe, 128), _pad_last(_rope_partner(w_pe), 128)],
                                axis=1).astype(BF16),
        "qln": q_lora_norm[None, :],
        "kvln": kv_lora_norm[None, :],
        "wuq": _pad_last(wuq, MLA_QPAD).reshape(q_rank, MLA_HEADS * MLA_QPAD).astype(BF16),
        "wuqs": _pad_last(_rope_partner(wuq_rope), 128).reshape(q_rank, MLA_HEADS * 128).astype(BF16),
        "wuk": wukv[..., :MLA_NOPE].reshape(kv_rank, MLA_HEADS * MLA_NOPE).astype(BF16),
        "wuvt": wukv[..., MLA_NOPE:].reshape(kv_rank, MLA_HEADS * MLA_V).T.astype(BF16),
        "qn": gains(q_norm),
        "kn": gains(k_norm),
    }


def kernel(x, c, ctx, c_ctx, w_ada, b_ada, norm_mix, norm_ffn, gla_w_in, gla_gate_w1, gla_gate_w2, gla_gate_b, gla_out_norm, gla_w_out, mla_w_down, mla_q_lora_norm, mla_kv_lora_norm, mla_w_uq, mla_w_ukv, mla_q_norm, mla_k_norm, mla_w_out, ffn_w_up, ffn_conv_w, ffn_conv_b, ffn_w_down):
    bsz, seq, d = x.shape
    lc = ctx.shape[1]
    depth = w_ada.shape[0]
    d_ff = ffn_w_down.shape[1]
    dqk = gla_gate_w2.shape[-1]
    rank = gla_gate_w1.shape[-1]
    dv = gla_out_norm.shape[-1]

    cond_rows = -(-(bsz + 1) // 8) * 8
    cond = jnp.zeros((cond_rows, d), F32).at[:bsz].set(c).at[bsz].set(c_ctx)
    mods = _ada_all(cond, w_ada, b_ada)

    cos_l, sin_l = _rope_tables(seq)
    cos_c = _pad_last(jnp.ones((lc, MLA_ROPE), F32), 128)
    sin_c = jnp.zeros((lc, 128), F32)

    tm_l = min(seq, 512)
    tm_c = min(lc, 512)
    xc = ctx
    for i in range(depth):
        last = i == depth - 1
        j = i // 2
        mod_l = mods[i, :bsz].reshape(bsz, 6, d)
        mod_c = jnp.broadcast_to(mods[i, bsz].reshape(1, 6, d), (bsz, 6, d))
        g_mix = norm_mix[i][None, :]
        mix_l = mix_c = None
        if i % 2 == 0:
            w_in = gla_w_in[j].astype(BF16)
            w1 = _pad_last(jnp.concatenate([gla_gate_w1[j, 0], gla_gate_w1[j, 1]], axis=1),
                           GLA_RANK_PAD).astype(BF16)
            w2 = jnp.zeros((GLA_RANK_PAD, 2 * dqk), F32)
            w2 = w2.at[:rank, :dqk].set(gla_gate_w2[j, 0]).at[rank:2 * rank, dqk:].set(gla_gate_w2[j, 1])
            w2 = w2.astype(BF16)
            gb = gla_gate_b[j].reshape(1, 2 * dqk)
            gn = gla_out_norm[j][None, :]
            w_out = gla_w_out[j].astype(BF16)
            s0 = jnp.zeros((bsz, GLA_HEADS, dv, dqk // GLA_HEADS), F32)

            proj_c = _gla_proj(xc, mod_c, g_mix, w_in, w1, w2, gb, tm=tm_c)
            proj_l = _gla_proj(x, mod_l, g_mix, w_in, w1, w2, gb, tm=tm_l)
            vc, rc, fwd_c, bwd_c = proj_c[0], proj_c[1], proj_c[2:5], proj_c[5:8]
            v, r, fwd_l, bwd_l = proj_l[0], proj_l[1], proj_l[2:5], proj_l[5:8]
            if last:
                _, s_fwd = _gla_scan(*fwd_c[:2], vc, fwd_c[2], s0, direction=0, want_o=False)
                _, s_bwd = _gla_scan(*bwd_c[:2], vc, bwd_c[2], s0, direction=1, want_o=False)
            else:
                oc_b, s_bwd = _gla_scan(*bwd_c[:2], vc, bwd_c[2], s0, direction=1)
                xc, s_fwd = _gla_scan(*fwd_c[:2], vc, fwd_c[2], s0, direction=0,
                                      fused=(oc_b, rc, xc, mod_c, gn, w_out))
            o_b, _ = _gla_scan(*bwd_l[:2], v, bwd_l[2], s_bwd, direction=1)
            x, _ = _gla_scan(*fwd_l[:2], v, fwd_l[2], s_fwd, direction=0,
                             fused=(o_b, r, x, mod_l, gn, w_out))
        else:
            w = _mla_weights(mla_w_down[j], mla_q_lora_norm[j], mla_kv_lora_norm[j], mla_w_uq[j],
                             mla_w_ukv[j], mla_q_norm[j], mla_k_norm[j])
            w_out = mla_w_out[j].astype(BF16)
            if last:
                kc, vtc = _mla_proj(xc, mod_c, g_mix, cos_c, sin_c, w, want_q=False, tm=tm_c)
            else:
                qc, kc, vtc = _mla_proj(xc, mod_c, g_mix, cos_c, sin_c, w, want_q=True, tm=tm_c)
            q, k, vt = _mla_proj(x, mod_l, g_mix, cos_l, sin_l, w, want_q=True, tm=min(seq, 1024))
            mix_l = (_latent_attention(w["bounded"], q, kc, vtc, k, vt), w_out)
            if not last:
                mix_c = (_attention(qc, kc, vtc, tq=min(lc, 512)), w_out)

        g_ffn = norm_ffn[i][None, :]
        wa = ffn_w_up[i, :, :d_ff].astype(BF16)
        wg = ffn_w_up[i, :, d_ff:].astype(BF16)
        conv = jnp.concatenate([ffn_conv_w[i], ffn_conv_b[i][None, :]], axis=0)
        ca, cg = conv[:, :d_ff], conv[:, d_ff:]
        wd = ffn_w_down[i].astype(BF16)
        x = _conv_ffn(x, mod_l, g_ffn, wa, wg, ca, cg, wd, tm=tm_l, fc=FFN_COLS, mixer_out=mix_l)
        if not last:
            xc = _conv_ffn(xc, mod_c, g_ffn, wa, wg, ca, cg, wd, tm=tm_c, fc=FFN_COLS, mixer_out=mix_c)
    return x
```

```python
import functools

import jax
import jax.numpy as jnp
from jax import lax
from jax.experimental import pallas as pl
from jax.experimental.pallas import tpu as pltpu

F32 = jnp.float32
BF16 = jnp.bfloat16

LANES = 128
SUBLANES = 8
BF16_ROWS = 16
EPS = 1e-6
GRID_W = 64
ROPE_THETA = 10000.0
LOG2_E = 1.4426950408889634

GLA_HEADS = 4
GLA_GATE_NORMALIZER = 16.0
GLA_CHUNK = 128
GLA_RANK_PAD = 128

MLA_HEADS = 8
MLA_NOPE = 128
MLA_ROPE = 64
MLA_V = 128
MLA_QPAD = 256
MLA_PROJ_ROWS = 256
ATTN_MAX_SCORE = 50.0

FFN_COLS = 256
CONV_HALO = SUBLANES
VMEM_LIMIT = 56 * 1024 * 1024


def _cparams(*sem):
    return pltpu.CompilerParams(dimension_semantics=sem, vmem_limit_bytes=VMEM_LIMIT)


def _const_spec(shape):
    nd = len(shape)
    return pl.BlockSpec(shape, lambda *_: (0,) * nd, pipeline_mode=pl.Buffered(1))


def _dot(a, b):
    return jnp.dot(a, b, preferred_element_type=F32)


def _dot_nt(a, b):
    return lax.dot_general(a, b, (((1,), (1,)), ((), ())), preferred_element_type=F32)


def _dot_tn(a, b):
    return lax.dot_general(a, b, (((0,), (0,)), ((), ())), preferred_element_type=F32)


def _sigmoid(x):
    return 1.0 / (1.0 + jnp.exp(-x))


def _rms_scale(x, width):
    ss = jnp.sum(x * x, axis=-1, keepdims=True)
    return lax.rsqrt(ss * (1.0 / width) + EPS)


def _modulate(x, g, shift, scale):
    return x * _rms_scale(x, x.shape[-1]) * (g * (1.0 + scale)) + shift


def _ada_kernel(cond_ref, w_ref, b_ref, o_ref):
    cond = cond_ref[...]
    s = cond * _sigmoid(cond)
    o_ref[0] = jnp.dot(s, w_ref[0], preferred_element_type=F32,
                       precision=lax.Precision.HIGHEST) + b_ref[0]


def _ada_all(cond, w_ada, b_ada):
    depth, d, d6 = w_ada.shape
    rows = cond.shape[0]
    return pl.pallas_call(
        _ada_kernel,
        grid=(depth, d6 // d),
        in_specs=[
            pl.BlockSpec((rows, d), lambda i, j: (0, 0)),
            pl.BlockSpec((1, d, d), lambda i, j: (i, 0, j)),
            pl.BlockSpec((1, 1, d), lambda i, j: (i, 0, j)),
        ],
        out_specs=pl.BlockSpec((1, rows, d), lambda i, j: (i, 0, j)),
        out_shape=jax.ShapeDtypeStruct((depth, rows, d6), F32),
        compiler_params=_cparams("parallel", "parallel"),
        name="ada_mod",
    )(cond, w_ada, b_ada.reshape(depth, 1, d6))


def _gla_proj_kernel(x_ref, mod_ref, g_ref, win_ref, w1_ref, w2_ref, gb_ref,
                     v_ref, r_ref, qkf_ref, ff_ref, qkb_ref, fb_ref, *, dqk):
    mod = mod_ref[0]
    h = _modulate(x_ref[0], g_ref[...], mod[0:1], mod[1:2]).astype(BF16)
    c = GLA_CHUNK
    dk = dqk // GLA_HEADS
    dv = v_ref.shape[2]
    low = _dot(h, w1_ref[...]).astype(BF16)
    z = _dot(low, w2_ref[...]) + gb_ref[...]
    pqk = _dot(h, win_ref[:, :2 * dqk])
    unit = LOG2_E / GLA_GATE_NORMALIZER
    gates = jnp.minimum(z, 0.0) * unit - jnp.log(1.0 + jnp.exp(-jnp.abs(z))) * unit
    row = lax.broadcasted_iota(jnp.int32, (c, c), 0)
    col = lax.broadcasted_iota(jnp.int32, (c, c), 1)
    for direction, (qk_ref, f_ref) in enumerate(((qkf_ref, ff_ref), (qkb_ref, fb_ref))):
        reverse = direction == 1
        tri = ((col >= row) if reverse else (col <= row)).astype(BF16)
        tri2 = jnp.concatenate([tri, tri], axis=1)
        i_mid = c // 2 if reverse else c // 2 - 1
        i_end = 0 if reverse else c - 1
        g = gates[:, direction * dqk:(direction + 1) * dqk]
        g_hi = g.astype(BF16)
        g_lo = (g - g_hi.astype(F32)).astype(BF16)
        for ci in range(x_ref.shape[1] // c):
            rows = slice(ci * c, (ci + 1) * c)
            cum = _dot(tri2, jnp.concatenate([g_hi[rows], g_lo[rows]], axis=0))
            c_mid = cum[i_mid:i_mid + 1, :]
            c_end = cum[i_end:i_end + 1, :]
            rel = cum - c_mid
            qk_ref[0, rows, :dqk] = (pqk[rows, :dqk] * (jnp.exp2(rel) * (dk ** -0.5))).astype(BF16)
            qk_ref[0, rows, dqk:] = (pqk[rows, dqk:] * jnp.exp2(-rel)).astype(BF16)
            factors = jnp.concatenate([jnp.exp2(c_mid), jnp.exp2(c_end - c_mid), jnp.exp2(c_end)], axis=1)
            f_ref[0, ci * SUBLANES:(ci + 1) * SUBLANES, :] = jnp.broadcast_to(factors, (SUBLANES, 3 * dqk))
    pvr = _dot(h, win_ref[:, 2 * dqk:])
    v_ref[0] = pvr[:, :dv].astype(BF16)
    r_ref[0] = pvr[:, dv:].astype(BF16)


def _gla_proj(x, mod, norm_g, w_in, w1, w2, gb, *, tm):
    b, l, d = x.shape
    dqk = w2.shape[1] // 2
    dv = (w_in.shape[1] - 2 * dqk) // 2
    drows = tm // GLA_CHUNK * SUBLANES
    row = lambda i, j: (i, j, 0)
    tok = lambda width, dtype: (pl.BlockSpec((1, tm, width), row), jax.ShapeDtypeStruct((b, l, width), dtype))
    fac = (pl.BlockSpec((1, drows, 3 * dqk), row),
           jax.ShapeDtypeStruct((b, l // GLA_CHUNK * SUBLANES, 3 * dqk), F32))
    outs = [tok(dv, BF16), tok(dv, BF16)] + 2 * [tok(2 * dqk, BF16), fac]
    return pl.pallas_call(
        functools.partial(_gla_proj_kernel, dqk=dqk),
        grid=(b, l // tm),
        in_specs=[
            pl.BlockSpec((1, tm, d), row),
            pl.BlockSpec((1, 6, d), lambda i, j: (i, 0, 0)),
            _const_spec(norm_g.shape),
            _const_spec(w_in.shape),
            _const_spec(w1.shape),
            _const_spec(w2.shape),
            _const_spec(gb.shape),
        ],
        out_specs=[o[0] for o in outs],
        out_shape=[o[1] for o in outs],
        compiler_params=_cparams("parallel", "parallel"),
        name="gla_proj",
    )(x, mod, norm_g, w_in, w1, w2, gb)


def _gla_scan_kernel(*refs, reverse, want_o, fuse_out, tblk):
    if fuse_out:
        (qk_ref, v_ref, f_ref, s0_ref, ob_ref, r_ref, x_ref, mod_ref, gn_ref, wout_ref,
         xo_ref, st_ref, y_scr) = refs
    elif want_o:
        qk_ref, v_ref, f_ref, s0_ref, o_ref, st_ref = refs
    else:
        qk_ref, v_ref, f_ref, s0_ref, st_ref = refs

    @pl.when(pl.program_id(1) == 0)
    def _():
        st_ref[...] = s0_ref[...]

    c = GLA_CHUNK
    dqk = qk_ref.shape[2] // 2
    dk = dqk // GLA_HEADS
    dv = v_ref.shape[2] // GLA_HEADS
    row = lax.broadcasted_iota(jnp.int32, (c, c), 0)
    col = lax.broadcasted_iota(jnp.int32, (c, c), 1)
    mask = (col >= row) if reverse else (col <= row)

    nch = tblk // c
    states = [st_ref[0, h] for h in range(GLA_HEADS)]
    for ci in (reversed(range(nch)) if reverse else range(nch)):
        rows = slice(ci * c, (ci + 1) * c)
        frow = slice(ci * SUBLANES, ci * SUBLANES + 1)
        for h in range(GLA_HEADS):
            qs = slice(h * dk, (h + 1) * dk)
            ks = slice(dqk + h * dk, dqk + (h + 1) * dk)
            vs = slice(h * dv, (h + 1) * dv)
            v = v_ref[0, rows, vs]
            st = states[h]
            k_in = qk_ref[0, rows, ks]
            k_st = (k_in.astype(F32) * f_ref[0, frow, ks]).astype(BF16)
            if want_o:
                q_in = qk_ref[0, rows, qs]
                q_st = (q_in.astype(F32) * f_ref[0, frow, qs]).astype(BF16)
                scores = jnp.where(mask, _dot_nt(q_in, k_in), 0.0)
                o = _dot(scores.astype(BF16), v) + _dot_nt(q_st, st.astype(BF16))
                if fuse_out:
                    o = o + ob_ref[0, rows, vs].astype(F32)
                    r = r_ref[0, rows, vs].astype(F32)
                    y = o * _rms_scale(o, dv) * gn_ref[...] * (r * _sigmoid(r))
                    y_scr[rows, vs] = y.astype(BF16)
                else:
                    o_ref[0, rows, vs] = o.astype(BF16)
            decay_end = f_ref[0, frow, 2 * dqk + h * dk:2 * dqk + (h + 1) * dk]
            states[h] = st * decay_end + _dot_tn(v, k_st)
    for h in range(GLA_HEADS):
        st_ref[0, h] = states[h]

    if fuse_out:
        gate = mod_ref[0][2:3]
        xo_ref[0] = x_ref[0] + gate * _dot(y_scr[...], wout_ref[...])


def _gla_scan(qk, factors, v, s0, *, direction, want_o=True, fused=None):
    b, l, dq2 = qk.shape
    dvt = v.shape[2]
    tblk = min(l, 1024)
    nblk = l // tblk
    reverse = direction == 1
    blk = (lambda i, j: (i, nblk - 1 - j, 0)) if reverse else (lambda i, j: (i, j, 0))
    st_spec = pl.BlockSpec((1,) + s0.shape[1:], lambda i, j: (i, 0, 0, 0))
    in_specs = [
        pl.BlockSpec((1, tblk, dq2), blk),
        pl.BlockSpec((1, tblk, dvt), blk),
        pl.BlockSpec((1, tblk // GLA_CHUNK * SUBLANES, factors.shape[2]), blk),
        st_spec,
    ]
    args = [qk, v, factors, s0]
    st_shape = jax.ShapeDtypeStruct(s0.shape, F32)
    scratch = []
    if fused is not None:
        o_other, r, x, mod, gn, w_out = fused
        d = x.shape[2]
        in_specs += [
            pl.BlockSpec((1, tblk, dvt), blk),
            pl.BlockSpec((1, tblk, dvt), blk),
            pl.BlockSpec((1, tblk, d), blk),
            pl.BlockSpec((1, 6, d), lambda i, j: (i, 0, 0)),
            _const_spec(gn.shape),
            _const_spec(w_out.shape),
        ]
        args += [o_other, r, x, mod, gn, w_out]
        out_specs = [pl.BlockSpec((1, tblk, d), blk), st_spec]
        out_shape = [jax.ShapeDtypeStruct(x.shape, F32), st_shape]
        scratch = [pltpu.VMEM((tblk, dvt), BF16)]
    elif want_o:
        out_specs = [pl.BlockSpec((1, tblk, dvt), blk), st_spec]
        out_shape = [jax.ShapeDtypeStruct((b, l, dvt), BF16), st_shape]
    else:
        out_specs = [st_spec]
        out_shape = [st_shape]
    outs = pl.pallas_call(
        functools.partial(_gla_scan_kernel, reverse=reverse, want_o=want_o,
                          fuse_out=fused is not None, tblk=tblk),
        grid=(b, nblk),
        in_specs=in_specs,
        out_specs=out_specs,
        out_shape=out_shape,
        scratch_shapes=scratch,
        compiler_params=_cparams("parallel", "arbitrary"),
        name="gla_scan_%s%s" % ("bwd" if reverse else "fwd", "_out" if fused is not None else ""),
    )(*args)
    if want_o:
        return outs[0], outs[1]
    return None, outs[0]


def _mla_proj_kernel(*refs, want_q):
    (x_ref, mod_ref, g_ref, cos_ref, sin_ref, wdq_ref, wdkv_ref, wdpe_ref, qln_ref, kvln_ref,
     wuq_ref, wuqs_ref, wuk_ref, wuvt_ref, qn_ref, kn_ref) = refs[:16]
    if want_q:
        q_ref, k_ref, vt_ref = refs[16:]
    else:
        k_ref, vt_ref = refs[16:]
    mod = mod_ref[0]
    kn = kn_ref[...]
    qn = qn_ref[...] * ((MLA_NOPE + MLA_ROPE) ** -0.5 * LOG2_E)
    sub = min(x_ref.shape[1], MLA_PROJ_ROWS)
    parts = [slice(i * sub, (i + 1) * sub) for i in range(x_ref.shape[1] // sub)]

    def rope_tables(gains, rows):
        return cos_ref[rows, :] * gains[1:2], sin_ref[rows, :] * gains[2:3]

    def rope_part(raw, raw_partner, tables):
        return (raw * tables[0] + raw_partner * tables[1]) * _rms_scale(raw, MLA_ROPE)

    def normed(c, gain_ref):
        return (c * _rms_scale(c, c.shape[-1]) * gain_ref[...]).astype(BF16)

    hs = [_modulate(x_ref[0, r, :], g_ref[...], mod[0:1], mod[1:2]).astype(BF16) for r in parts]
    c_kv = [_dot(h, wdkv_ref[...]) for h in hs]
    kpe = [_dot(h, wdpe_ref[...]) for h in hs]
    if want_q:
        c_q = [_dot(h, wdq_ref[...]) for h in hs]
    c_kv = [normed(c, kvln_ref) for c in c_kv]
    for c, r in zip(c_kv, parts):
        vt_ref[0, :, r] = _dot_nt(wuvt_ref[...], c).astype(BF16)
    kn_all = [_dot(c, wuk_ref[...]) for c in c_kv]
    k_rope = [rope_part(p[:, :128], p[:, 128:], rope_tables(kn, r)).astype(BF16) for p, r in zip(kpe, parts)]
    if want_q:
        c_q = [normed(c, qln_ref) for c in c_q]
        qm = [_dot(c, wuq_ref[...]) for c in c_q]
        qs = [_dot(c, wuqs_ref[...]) for c in c_q]
    for i, r in enumerate(parts):
        if want_q:
            q_tables = rope_tables(qn, r)
        for hd in range(MLA_HEADS):
            k_nope = kn_all[i][:, hd * MLA_NOPE:(hd + 1) * MLA_NOPE]
            k_ref[0, hd, r, :MLA_NOPE] = (k_nope * _rms_scale(k_nope, MLA_NOPE) * kn[0:1]).astype(BF16)
            k_ref[0, hd, r, MLA_NOPE:] = k_rope[i]
            if want_q:
                base = hd * MLA_QPAD
                q_nope = qm[i][:, base:base + MLA_NOPE]
                q_ref[0, hd, r, :MLA_NOPE] = (q_nope * _rms_scale(q_nope, MLA_NOPE) * qn[0:1]).astype(BF16)
                q_rope = rope_part(qm[i][:, base + MLA_NOPE:base + MLA_QPAD],
                                   qs[i][:, hd * 128:(hd + 1) * 128], q_tables)
                q_ref[0, hd, r, MLA_NOPE:] = q_rope.astype(BF16)


def _mla_proj(x, mod, norm_g, cos, sin, w, *, want_q, tm):
    b, l, d = x.shape
    consts = [w["wdq"], w["wdkv"], w["wdpe"], w["qln"], w["kvln"], w["wuq"], w["wuqs"], w["wuk"],
              w["wuvt"], w["qn"], w["kn"]]
    hspec = pl.BlockSpec((1, MLA_HEADS, tm, MLA_QPAD), lambda i, j: (i, 0, j, 0))
    hshape = jax.ShapeDtypeStruct((b, MLA_HEADS, l, MLA_QPAD), BF16)
    out_specs = [hspec, pl.BlockSpec((1, MLA_HEADS * MLA_V, tm), lambda i, j: (i, 0, j))]
    out_shape = [hshape, jax.ShapeDtypeStruct((b, MLA_HEADS * MLA_V, l), BF16)]
    if want_q:
        out_specs = [hspec] + out_specs
        out_shape = [hshape] + out_shape
    return pl.pallas_call(
        functools.partial(_mla_proj_kernel, want_q=want_q),
        grid=(b, l // tm),
        in_specs=[
            pl.BlockSpec((1, tm, d), lambda i, j: (i, j, 0)),
            pl.BlockSpec((1, 6, d), lambda i, j: (i, 0, 0)),
            _const_spec(norm_g.shape),
            pl.BlockSpec((tm, 128), lambda i, j: (j, 0)),
            pl.BlockSpec((tm, 128), lambda i, j: (j, 0)),
        ] + [_const_spec(a.shape) for a in consts],
        out_specs=out_specs,
        out_shape=out_shape,
        compiler_params=_cparams("parallel", "parallel"),
        name="mla_proj_q" if want_q else "mla_proj_kv",
    )(x, mod, norm_g, cos, sin, *consts)


def _attn_kernel(*refs, with_latent, tk, heads):
    if with_latent:
        q_ref, kc_ref, vtc_ref, k_ref, vt_ref, o_ref = refs
    else:
        q_ref, kc_ref, vtc_ref, o_ref = refs
    dv = vtc_ref.shape[1] // heads
    for hd in range(heads):
        q = q_ref[0, hd]
        vrows = slice(hd * dv, (hd + 1) * dv)
        chunks = [(kc_ref, vtc_ref, 0, kc_ref.shape[2])]
        if with_latent:
            chunks += [(k_ref, vt_ref, j * tk, tk) for j in range(k_ref.shape[2] // tk)]

        def scores(idx):
            kr, _, start, size = chunks[idx]
            return _dot_nt(kr[0, hd, start:start + size, :], q)

        s_next = scores(0)
        m = denom = acc = None
        for idx, (_, vr, start, size) in enumerate(chunks):
            s = s_next
            if idx + 1 < len(chunks):
                s_next = scores(idx + 1)
            vt = vr[0, vrows, start:start + size]
            cmax = jnp.max(s, axis=0, keepdims=True)
            if m is None:
                m = cmax
                p = jnp.exp2(s - m)
                denom = jnp.sum(p, axis=0, keepdims=True)
                acc = _dot(vt, p.astype(BF16))
            else:
                m_new = jnp.maximum(m, cmax)
                alpha = jnp.exp2(m - m_new)
                p = jnp.exp2(s - m_new)
                denom = denom * alpha + jnp.sum(p, axis=0, keepdims=True)
                acc = acc * alpha + _dot(vt, p.astype(BF16))
                m = m_new
        o_ref[0, :, vrows] = (acc * (1.0 / denom)).T.astype(BF16)


def _attn_bounded_kernel(q_ref, kc_ref, vtc_ref, k_ref, vt_ref, o_ref, *, tq, tk):
    chunks = [(kc_ref, vtc_ref, 0, kc_ref.shape[2])]
    chunks += [(k_ref, vt_ref, j * tk, tk) for j in range(k_ref.shape[2] // tk)]
    for qi in range(q_ref.shape[2] // tq):
        cols = slice(qi * tq, (qi + 1) * tq)
        q = q_ref[0, 0, cols, :]

        def scores(idx):
            kr, _, start, size = chunks[idx]
            return _dot_nt(kr[0, 0, start:start + size, :], q)

        s_next = scores(0)
        acc = denom8 = None
        for idx, (_, vr, start, size) in enumerate(chunks):
            s = s_next
            if idx + 1 < len(chunks):
                s_next = scores(idx + 1)
            p = jnp.exp2(s)
            part = jnp.sum(p.reshape(size // SUBLANES, SUBLANES, p.shape[1]), axis=0)
            y = _dot(vr[0, :, start:start + size], p.astype(BF16))
            acc = y if acc is None else acc + y
            denom8 = part if denom8 is None else denom8 + part
        denom = jnp.sum(denom8, axis=0, keepdims=True)
        o_ref[0, cols, :] = (acc * (1.0 / denom)).T.astype(BF16)


def _attention(q, kc, vtc, k=None, vt=None, *, tq, tk=512, bounded_tq=None, heads=1):
    b, nh, lq, dq = q.shape
    lc = kc.shape[2]
    dv = vtc.shape[1] // nh
    with_latent = k is not None
    in_specs = [
        pl.BlockSpec((1, heads, tq, dq), lambda i, h, j: (i, h, j, 0)),
        pl.BlockSpec((1, heads, lc, dq), lambda i, h, j: (i, h, 0, 0)),
        pl.BlockSpec((1, heads * dv, lc), lambda i, h, j: (i, h, 0)),
    ]
    args = [q, kc, vtc]
    if with_latent:
        lk = k.shape[2]
        in_specs += [pl.BlockSpec((1, heads, lk, dq), lambda i, h, j: (i, h, 0, 0)),
                     pl.BlockSpec((1, heads * dv, lk), lambda i, h, j: (i, h, 0))]
        args += [k, vt]
    if bounded_tq is not None:
        assert heads == 1
        body = functools.partial(_attn_bounded_kernel, tq=bounded_tq, tk=tk)
        name = "mla_attn_bounded"
    else:
        body = functools.partial(_attn_kernel, with_latent=with_latent, tk=tk, heads=heads)
        name = "mla_attn" if with_latent else "mla_attn_ctx"
    return pl.pallas_call(
        body,
        grid=(b, nh // heads, lq // tq),
        in_specs=in_specs,
        out_specs=pl.BlockSpec((1, tq, heads * dv), lambda i, h, j: (i, j, h)),
        out_shape=jax.ShapeDtypeStruct((b, lq, nh * dv), BF16),
        compiler_params=_cparams("parallel", "parallel", "parallel"),
        name=name,
    )(*args)


def _latent_attention(bounded, q, kc, vtc, k, vt):
    lq = q.shape[2]
    lk = k.shape[2]
    fast = functools.partial(_attention, tq=min(lq, 4096), tk=min(lk, 2048), bounded_tq=min(lq, 2048))
    exact = functools.partial(_attention, tq=min(lq, 512), tk=min(lk, 512))
    return lax.cond(bounded, fast, exact, q, kc, vtc, k, vt)


def _ffn_kernel(*refs, tm, fc, mixer_out):
    if mixer_out:
        (xm_ref, xp_ref, xn_ref, om_ref, op_ref, on_ref, wo_ref, mod_ref, g_ref, wa_ref, wg_ref, ca_ref,
         cg_ref, wd_ref, xo_ref, h_scr, ua_scr, ug_scr, act_scr, o_scr) = refs
    else:
        (xm_ref, xp_ref, xn_ref, mod_ref, g_ref, wa_ref, wg_ref, ca_ref, cg_ref, wd_ref,
         xo_ref, h_scr, ua_scr, ug_scr, act_scr) = refs
    i = pl.program_id(1)
    mod = mod_ref[0]
    g = g_ref[...]
    hal = CONV_HALO

    def hidden(x):
        return _modulate(x, g, mod[3:4], mod[4:5])

    x_main, x_nxt, x_prv = xm_ref[0], xn_ref[0], xp_ref[0]
    if mixer_out:
        o_scr[0:tm] = om_ref[0]
        o_nxt = on_ref[0].astype(F32)[:hal]
        o_prv = op_ref[0].astype(F32)[BF16_ROWS - hal:]
        o_scr[tm:] = jnp.concatenate([o_nxt, o_prv], axis=0).astype(BF16)
        y = _dot(o_scr[...], wo_ref[...])
        x_main = x_main + mod[2:3] * y[0:tm]
        x_nxt = x_nxt + mod[2:3] * y[tm:tm + hal]
        x_prv = x_prv + mod[2:3] * y[tm + hal:]

    h_scr[0:tm] = hidden(x_main).astype(BF16)
    nxt = jnp.where(i == pl.num_programs(1) - 1, 0.0, hidden(x_nxt))
    prv = jnp.where(i == 0, 0.0, hidden(x_prv))
    h_scr[tm:] = jnp.concatenate([nxt, prv], axis=0).astype(BF16)

    def put(u_scr, slab, u):
        u_scr[slab, hal:hal + tm, :] = u[0:tm]
        u_scr[slab, hal + tm:, :] = u[tm:tm + hal]
        u_scr[slab, 0:hal, :] = u[tm + hal:]

    def conv(u_scr, slab, cw):
        return (u_scr[slab, hal - 1:hal - 1 + tm, :] * cw[0:1] + u_scr[slab, hal:hal + tm, :] * cw[1:2]
                + u_scr[slab, hal + 1:hal + 1 + tm, :] * cw[2:3] + cw[3:4])

    per = fc // LANES
    for ci in range(wa_ref.shape[1] // fc):
        cols = slice(ci * fc, (ci + 1) * fc)
        ua = _dot(h_scr[...], wa_ref[:, cols])
        ug = _dot(h_scr[...], wg_ref[:, cols])
        for s in range(per):
            put(ua_scr, ci * per + s, ua[:, s * LANES:(s + 1) * LANES])
            put(ug_scr, ci * per + s, ug[:, s * LANES:(s + 1) * LANES])
        for s in range(per):
            slab = ci * per + s
            lanes = slice(slab * LANES, (slab + 1) * LANES)
            a = conv(ua_scr, slab, ca_ref[:, lanes])
            gt = conv(ug_scr, slab, cg_ref[:, lanes])
            act_scr[:, lanes] = (gt * _sigmoid(gt) * a).astype(BF16)
    xo_ref[0] = x_main + mod[5:6] * _dot(act_scr[...], wd_ref[...])


def _conv_ffn(x, mod, norm_g, wa, wg, ca, cg, wd, *, tm, fc, mixer_out=None):
    b, l, d = x.shape
    hal = CONV_HALO
    nt = l // tm
    d_ff = wa.shape[1]

    def halo_specs(rows, width):
        per, last = tm // rows, l // rows - 1
        return [pl.BlockSpec((1, rows, width), lambda i, j: (i, jnp.maximum(j * per - 1, 0), 0)),
                pl.BlockSpec((1, rows, width), lambda i, j: (i, jnp.minimum((j + 1) * per, last), 0))]

    in_specs = [pl.BlockSpec((1, tm, d), lambda i, j: (i, j, 0))] + halo_specs(hal, d)
    args = [x, x, x]
    scratch = [
        pltpu.VMEM((tm + 2 * hal, d), BF16),
        pltpu.VMEM((d_ff // LANES, tm + 2 * hal, LANES), F32),
        pltpu.VMEM((d_ff // LANES, tm + 2 * hal, LANES), F32),
        pltpu.VMEM((tm, d_ff), BF16),
    ]
    if mixer_out is not None:
        o, w_out = mixer_out
        do = o.shape[2]
        in_specs += [pl.BlockSpec((1, tm, do), lambda i, j: (i, j, 0))] + halo_specs(BF16_ROWS, do)
        in_specs += [_const_spec(w_out.shape)]
        args += [o, o, o, w_out]
        scratch += [pltpu.VMEM((tm + 2 * hal, do), BF16)]
    consts = [norm_g, wa, wg, ca, cg, wd]
    return pl.pallas_call(
        functools.partial(_ffn_kernel, tm=tm, fc=fc, mixer_out=mixer_out is not None),
        grid=(b, nt),
        in_specs=in_specs + [pl.BlockSpec((1, 6, d), lambda i, j: (i, 0, 0))] + [_const_spec(a.shape) for a in consts],
        out_specs=pl.BlockSpec((1, tm, d), lambda i, j: (i, j, 0)),
        out_shape=jax.ShapeDtypeStruct(x.shape, F32),
        scratch_shapes=scratch,
        compiler_params=_cparams("parallel", "parallel"),
        name="conv_ffn_mix" if mixer_out is not None else "conv_ffn",
    )(*args, mod, *consts)


def _rope_tables(length):
    nf = MLA_ROPE // 4
    t = jnp.arange(length)
    pos = jnp.stack([(t // GRID_W).astype(F32), (t % GRID_W).astype(F32)], axis=1)
    inv = ROPE_THETA ** (-jnp.arange(nf, dtype=F32) / nf)
    ang = pos[:, :, None] * inv
    cos = jnp.cos(ang)[:, :, None, :] * jnp.ones((1, 1, 2, 1), F32)
    sin = jnp.sin(ang)[:, :, None, :] * jnp.array([-1.0, 1.0], F32)[None, None, :, None]
    pad = lambda a: jnp.pad(a.reshape(length, MLA_ROPE), ((0, 0), (0, 128 - MLA_ROPE)))
    return pad(cos), pad(sin)


def _rope_partner(a):
    nf = MLA_ROPE // 4
    r = a.reshape(a.shape[:-1] + (2, 2, nf))
    return jnp.flip(r, axis=-2).reshape(a.shape)


def _pad_last(a, width):
    return jnp.pad(a, [(0, 0)] * (a.ndim - 1) + [(0, width - a.shape[-1])])


def _mla_weights(w_down, q_lora_norm, kv_lora_norm, w_uq, w_ukv, q_norm, k_norm):
    q_rank = q_lora_norm.shape[0]
    kv_rank = kv_lora_norm.shape[0]
    qk = MLA_NOPE + MLA_ROPE
    w_pe = w_down[:, q_rank + kv_rank:]
    wuq = w_uq.reshape(q_rank, MLA_HEADS, qk)
    wuq_rope = wuq[..., MLA_NOPE:]
    wukv = w_ukv.reshape(kv_rank, MLA_HEADS, MLA_NOPE + MLA_V)

    def max_sq_norm(g):
        return MLA_NOPE * jnp.max(g[:MLA_NOPE] ** 2) + MLA_ROPE * jnp.max(g[MLA_NOPE:] ** 2)

    bound = jnp.sqrt(max_sq_norm(q_norm) * max_sq_norm(k_norm)) * (qk ** -0.5 * LOG2_E * 1.02)

    def gains(g):
        rope = g[MLA_NOPE:]
        return jnp.stack([g[:MLA_NOPE], _pad_last(rope, 128), _pad_last(_rope_partner(rope), 128)])

    return {
        "bounded": bound <= ATTN_MAX_SCORE,
        "wdq": w_down[:, :q_rank].astype(BF16),
        "wdkv": w_down[:, q_rank:q_rank + kv_rank].astype(BF16),
        "wdpe": jnp.concatenate([_pad_last(w_pe, 128), _pad_last(_rope_partner(w_pe), 128)],
                                axis=1).astype(BF16),
        "qln": q_lora_norm[None, :],
        "kvln": kv_lora_norm[None, :],
        "wuq": _pad_last(wuq, MLA_QPAD).reshape(q_rank, MLA_HEADS * MLA_QPAD).astype(BF16),
        "wuqs": _pad_last(_rope_partner(wuq_rope), 128).reshape(q_rank, MLA_HEADS * 128).astype(BF16),
        "wuk": wukv[..., :MLA_NOPE].reshape(kv_rank, MLA_HEADS * MLA_NOPE).astype(BF16),
        "wuvt": wukv[..., MLA_NOPE:].reshape(kv_rank, MLA_HEADS * MLA_V).T.astype(BF16),
        "qn": gains(q_norm),
        "kn": gains(k_norm),
    }


def kernel(x, c, ctx, c_ctx, w_ada, b_ada, norm_mix, norm_ffn, gla_w_in, gla_gate_w1, gla_gate_w2, gla_gate_b, gla_out_norm, gla_w_out, mla_w_down, mla_q_lora_norm, mla_kv_lora_norm, mla_w_uq, mla_w_ukv, mla_q_norm, mla_k_norm, mla_w_out, ffn_w_up, ffn_conv_w, ffn_conv_b, ffn_w_down):
    bsz, seq, d = x.shape
    lc = ctx.shape[1]
    depth = w_ada.shape[0]
    d_ff = ffn_w_down.shape[1]
    dqk = gla_gate_w2.shape[-1]
    rank = gla_gate_w1.shape[-1]
    dv = gla_out_norm.shape[-1]

    cond_rows = -(-(bsz + 1) // 8) * 8
    cond = jnp.zeros((cond_rows, d), F32).at[:bsz].set(c).at[bsz].set(c_ctx)
    mods = _ada_all(cond, w_ada, b_ada)

    cos_l, sin_l = _rope_tables(seq)
    cos_c = _pad_last(jnp.ones((lc, MLA_ROPE), F32), 128)
    sin_c = jnp.zeros((lc, 128), F32)

    tm_l = min(seq, 512)
    tm_c = min(lc, 512)
    xc = ctx
    for i in range(depth):
        last = i == depth - 1
        j = i // 2
        mod_l = mods[i, :bsz].reshape(bsz, 6, d)
        mod_c = jnp.broadcast_to(mods[i, bsz].reshape(1, 6, d), (bsz, 6, d))
        g_mix = norm_mix[i][None, :]
        mix_l = mix_c = None
        if i % 2 == 0:
            w_in = gla_w_in[j].astype(BF16)
            w1 = _pad_last(jnp.concatenate([gla_gate_w1[j, 0], gla_gate_w1[j, 1]], axis=1),
                           GLA_RANK_PAD).astype(BF16)
            w2 = jnp.zeros((GLA_RANK_PAD, 2 * dqk), F32)
            w2 = w2.at[:rank, :dqk].set(gla_gate_w2[j, 0]).at[rank:2 * rank, dqk:].set(gla_gate_w2[j, 1])
            w2 = w2.astype(BF16)
            gb = gla_gate_b[j].reshape(1, 2 * dqk)
            gn = gla_out_norm[j][None, :]
            w_out = gla_w_out[j].astype(BF16)
            s0 = jnp.zeros((bsz, GLA_HEADS, dv, dqk // GLA_HEADS), F32)

            vc, rc, *dirs_c = _gla_proj(xc, mod_c, g_mix, w_in, w1, w2, gb, tm=tm_c)
            v, r, *dirs_l = _gla_proj(x, mod_l, g_mix, w_in, w1, w2, gb, tm=tm_l)
            fwd_c, bwd_c, fwd_l, bwd_l = dirs_c[:2], dirs_c[2:], dirs_l[:2], dirs_l[2:]
            if last:
                _, s_fwd = _gla_scan(*fwd_c, vc, s0, direction=0, want_o=False)
                _, s_bwd = _gla_scan(*bwd_c, vc, s0, direction=1, want_o=False)
            else:
                oc_b, s_bwd = _gla_scan(*bwd_c, vc, s0, direction=1)
                xc, s_fwd = _gla_scan(*fwd_c, vc, s0, direction=0, fused=(oc_b, rc, xc, mod_c, gn, w_out))
            o_b, _ = _gla_scan(*bwd_l, v, s_bwd, direction=1)
            x, _ = _gla_scan(*fwd_l, v, s_fwd, direction=0, fused=(o_b, r, x, mod_l, gn, w_out))
        else:
            w = _mla_weights(mla_w_down[j], mla_q_lora_norm[j], mla_kv_lora_norm[j], mla_w_uq[j],
                             mla_w_ukv[j], mla_q_norm[j], mla_k_norm[j])
            w_out = mla_w_out[j].astype(BF16)
            if last:
                kc, vtc = _mla_proj(xc, mod_c, g_mix, cos_c, sin_c, w, want_q=False, tm=tm_c)
            else:
                qc, kc, vtc = _mla_proj(xc, mod_c, g_mix, cos_c, sin_c, w, want_q=True, tm=tm_c)
            q, k, vt = _mla_proj(x, mod_l, g_mix, cos_l, sin_l, w, want_q=True, tm=min(seq, 1024))
            mix_l = (_latent_attention(w["bounded"], q, kc, vtc, k, vt), w_out)
            if not last:
                mix_c = (_attention(qc, kc, vtc, tq=min(lc, 512), heads=MLA_HEADS), w_out)

        g_ffn = norm_ffn[i][None, :]
        wa = ffn_w_up[i, :, :d_ff].astype(BF16)
        wg = ffn_w_up[i, :, d_ff:].astype(BF16)
        conv = jnp.concatenate([ffn_conv_w[i], ffn_conv_b[i][None, :]], axis=0)
        ca, cg = conv[:, :d_ff], conv[:, d_ff:]
        wd = ffn_w_down[i].astype(BF16)
        x = _conv_ffn(x, mod_l, g_ffn, wa, wg, ca, cg, wd, tm=tm_l, fc=FFN_COLS, mixer_out=mix_l)
        if not last:
            xc = _conv_ffn(xc, mod_c, g_ffn, wa, wg, ca, cg, wd, tm=tm_c, fc=FFN_COLS, mixer_out=mix_c)
    return x
```

```python
import functools

import jax
import jax.numpy as jnp
from jax import lax
from jax.experimental import pallas as pl
from jax.experimental.pallas import tpu as pltpu

F32 = jnp.float32
BF16 = jnp.bfloat16

LANES = 128
SUBLANES = 8
BF16_ROWS = 16
EPS = 1e-6
GRID_W = 64
ROPE_THETA = 10000.0
LOG2_E = 1.4426950408889634

GLA_HEADS = 4
GLA_GATE_NORMALIZER = 16.0
GLA_CHUNK = 128
GLA_RANK_PAD = 128

MLA_HEADS = 8
MLA_NOPE = 128
MLA_ROPE = 64
MLA_V = 128
MLA_QPAD = 256
ROPE_PAD = MLA_QPAD - MLA_NOPE
MLA_PROJ_ROWS = 256
ATTN_MAX_SCORE = 50.0

ROW_TILE = 512
MLA_PROJ_TILE = 1024
SCAN_BLOCK = 1024
ATTN_Q_STEP = 4096
ATTN_Q_TILE = 2048
ATTN_K_CHUNK = 2048
ATTN_EXACT_TILE = 512
FFN_COLS = 256
CONV_HALO = SUBLANES
VMEM_LIMIT = 56 * 1024 * 1024


def _cparams(*sem):
    return pltpu.CompilerParams(dimension_semantics=sem, vmem_limit_bytes=VMEM_LIMIT)


def _const_spec(shape):
    nd = len(shape)
    return pl.BlockSpec(shape, lambda *_: (0,) * nd, pipeline_mode=pl.Buffered(1))


def _dot(a, b):
    return jnp.dot(a, b, preferred_element_type=F32)


def _dot_nt(a, b):
    return lax.dot_general(a, b, (((1,), (1,)), ((), ())), preferred_element_type=F32)


def _dot_tn(a, b):
    return lax.dot_general(a, b, (((0,), (0,)), ((), ())), preferred_element_type=F32)


def _sigmoid(x):
    return 1.0 / (1.0 + jnp.exp(-x))


def _rms_scale(x, width):
    ss = jnp.sum(x * x, axis=-1, keepdims=True)
    return lax.rsqrt(ss * (1.0 / width) + EPS)


def _modulate(x, g, shift, scale):
    return x * _rms_scale(x, x.shape[-1]) * (g * (1.0 + scale)) + shift


def _ada_kernel(cond_ref, w_ref, b_ref, o_ref):
    cond = cond_ref[...]
    s = cond * _sigmoid(cond)
    o_ref[0] = jnp.dot(s, w_ref[0], preferred_element_type=F32,
                       precision=lax.Precision.HIGHEST) + b_ref[0]


def _ada_all(cond, w_ada, b_ada):
    depth, d, d6 = w_ada.shape
    rows = cond.shape[0]
    cols = 2 * d
    return pl.pallas_call(
        _ada_kernel,
        grid=(depth, d6 // cols),
        in_specs=[
            pl.BlockSpec((rows, d), lambda i, j: (0, 0)),
            pl.BlockSpec((1, d, cols), lambda i, j: (i, 0, j)),
            pl.BlockSpec((1, 1, cols), lambda i, j: (i, 0, j)),
        ],
        out_specs=pl.BlockSpec((1, rows, cols), lambda i, j: (i, 0, j)),
        out_shape=jax.ShapeDtypeStruct((depth, rows, d6), F32),
        compiler_params=_cparams("parallel", "parallel"),
        name="ada_mod",
    )(cond, w_ada, b_ada.reshape(depth, 1, d6))


def _gla_proj_kernel(x_ref, mod_ref, g_ref, win_ref, w1_ref, w2_ref, gb_ref,
                     v_ref, r_ref, qkf_ref, ff_ref, qkb_ref, fb_ref, *, dqk):
    mod = mod_ref[0]
    h = _modulate(x_ref[0], g_ref[...], mod[0:1], mod[1:2]).astype(BF16)
    c = GLA_CHUNK
    dk = dqk // GLA_HEADS
    dv = v_ref.shape[2]
    low = _dot(h, w1_ref[...]).astype(BF16)
    z = _dot(low, w2_ref[...]) + gb_ref[...]
    pqk = _dot(h, win_ref[:, :2 * dqk])
    unit = LOG2_E / GLA_GATE_NORMALIZER
    gates = jnp.minimum(z, 0.0) * unit - jnp.log(1.0 + jnp.exp(-jnp.abs(z))) * unit
    row = lax.broadcasted_iota(jnp.int32, (c, c), 0)
    col = lax.broadcasted_iota(jnp.int32, (c, c), 1)
    for direction, (qk_ref, f_ref) in enumerate(((qkf_ref, ff_ref), (qkb_ref, fb_ref))):
        reverse = direction == 1
        tri = ((col >= row) if reverse else (col <= row)).astype(BF16)
        tri2 = jnp.concatenate([tri, tri], axis=1)
        i_mid = c // 2 if reverse else c // 2 - 1
        i_end = 0 if reverse else c - 1
        g = gates[:, direction * dqk:(direction + 1) * dqk]
        g_hi = g.astype(BF16)
        g_lo = (g - g_hi.astype(F32)).astype(BF16)
        for ci in range(x_ref.shape[1] // c):
            rows = slice(ci * c, (ci + 1) * c)
            cum = _dot(tri2, jnp.concatenate([g_hi[rows], g_lo[rows]], axis=0))
            c_mid = cum[i_mid:i_mid + 1, :]
            c_end = cum[i_end:i_end + 1, :]
            rel = cum - c_mid
            qk_ref[0, rows, :dqk] = (pqk[rows, :dqk] * (jnp.exp2(rel) * (dk ** -0.5))).astype(BF16)
            qk_ref[0, rows, dqk:] = (pqk[rows, dqk:] * jnp.exp2(-rel)).astype(BF16)
            factors = jnp.concatenate([jnp.exp2(c_mid), jnp.exp2(c_end - c_mid), jnp.exp2(c_end)], axis=1)
            f_ref[0, ci * SUBLANES:(ci + 1) * SUBLANES, :] = jnp.broadcast_to(factors, (SUBLANES, 3 * dqk))
    pvr = _dot(h, win_ref[:, 2 * dqk:])
    v_ref[0] = pvr[:, :dv].astype(BF16)
    r_ref[0] = pvr[:, dv:].astype(BF16)


def _gla_proj(x, mod, norm_g, w_in, w1, w2, gb, *, tm):
    b, l, d = x.shape
    dqk = w2.shape[1] // 2
    dv = (w_in.shape[1] - 2 * dqk) // 2
    drows = tm // GLA_CHUNK * SUBLANES
    row = lambda i, j: (i, j, 0)
    tok = lambda width, dtype: (pl.BlockSpec((1, tm, width), row), jax.ShapeDtypeStruct((b, l, width), dtype))
    fac = (pl.BlockSpec((1, drows, 3 * dqk), row),
           jax.ShapeDtypeStruct((b, l // GLA_CHUNK * SUBLANES, 3 * dqk), F32))
    outs = [tok(dv, BF16), tok(dv, BF16)] + 2 * [tok(2 * dqk, BF16), fac]
    return pl.pallas_call(
        functools.partial(_gla_proj_kernel, dqk=dqk),
        grid=(b, l // tm),
        in_specs=[
            pl.BlockSpec((1, tm, d), row),
            pl.BlockSpec((1, 6, d), lambda i, j: (i, 0, 0)),
            _const_spec(norm_g.shape),
            _const_spec(w_in.shape),
            _const_spec(w1.shape),
            _const_spec(w2.shape),
            _const_spec(gb.shape),
        ],
        out_specs=[o[0] for o in outs],
        out_shape=[o[1] for o in outs],
        compiler_params=_cparams("parallel", "parallel"),
        name="gla_proj",
    )(x, mod, norm_g, w_in, w1, w2, gb)


def _gla_scan_kernel(*refs, reverse, want_o, fuse_out, tblk):
    if fuse_out:
        (qk_ref, v_ref, f_ref, s0_ref, ob_ref, r_ref, x_ref, mod_ref, gn_ref, wout_ref,
         xo_ref, st_ref, y_scr) = refs
    elif want_o:
        qk_ref, v_ref, f_ref, s0_ref, o_ref, st_ref = refs
    else:
        qk_ref, v_ref, f_ref, s0_ref, st_ref = refs

    @pl.when(pl.program_id(1) == 0)
    def _():
        st_ref[...] = s0_ref[...]

    c = GLA_CHUNK
    dqk = qk_ref.shape[2] // 2
    dk = dqk // GLA_HEADS
    dv = v_ref.shape[2] // GLA_HEADS
    row = lax.broadcasted_iota(jnp.int32, (c, c), 0)
    col = lax.broadcasted_iota(jnp.int32, (c, c), 1)
    mask = (col >= row) if reverse else (col <= row)

    nch = tblk // c
    states = [st_ref[0, h] for h in range(GLA_HEADS)]
    for ci in (reversed(range(nch)) if reverse else range(nch)):
        rows = slice(ci * c, (ci + 1) * c)
        frow = slice(ci * SUBLANES, ci * SUBLANES + 1)
        for h in range(GLA_HEADS):
            qs = slice(h * dk, (h + 1) * dk)
            ks = slice(dqk + h * dk, dqk + (h + 1) * dk)
            vs = slice(h * dv, (h + 1) * dv)
            v = v_ref[0, rows, vs]
            st = states[h]
            k_in = qk_ref[0, rows, ks]
            k_st = (k_in.astype(F32) * f_ref[0, frow, ks]).astype(BF16)
            if want_o:
                q_in = qk_ref[0, rows, qs]
                q_st = (q_in.astype(F32) * f_ref[0, frow, qs]).astype(BF16)
                scores = jnp.where(mask, _dot_nt(q_in, k_in), 0.0)
                o = _dot(scores.astype(BF16), v) + _dot_nt(q_st, st.astype(BF16))
                if fuse_out:
                    o = o + ob_ref[0, rows, vs].astype(F32)
                    r = r_ref[0, rows, vs].astype(F32)
                    y = o * _rms_scale(o, dv) * gn_ref[...] * (r * _sigmoid(r))
                    y_scr[rows, vs] = y.astype(BF16)
                else:
                    o_ref[0, rows, vs] = o.astype(BF16)
            decay_end = f_ref[0, frow, 2 * dqk + h * dk:2 * dqk + (h + 1) * dk]
            states[h] = st * decay_end + _dot_tn(v, k_st)
    for h in range(GLA_HEADS):
        st_ref[0, h] = states[h]

    if fuse_out:
        gate = mod_ref[0][2:3]
        xo_ref[0] = x_ref[0] + gate * _dot(y_scr[...], wout_ref[...])


def _gla_scan(qk, factors, v, s0, *, direction, want_o=True, fused=None):
    b, l, dq2 = qk.shape
    dvt = v.shape[2]
    tblk = min(l, SCAN_BLOCK)
    nblk = l // tblk
    reverse = direction == 1
    blk = (lambda i, j: (i, nblk - 1 - j, 0)) if reverse else (lambda i, j: (i, j, 0))
    st_spec = pl.BlockSpec((1,) + s0.shape[1:], lambda i, j: (i, 0, 0, 0))
    in_specs = [
        pl.BlockSpec((1, tblk, dq2), blk),
        pl.BlockSpec((1, tblk, dvt), blk),
        pl.BlockSpec((1, tblk // GLA_CHUNK * SUBLANES, factors.shape[2]), blk),
        st_spec,
    ]
    args = [qk, v, factors, s0]
    st_shape = jax.ShapeDtypeStruct(s0.shape, F32)
    scratch = []
    if fused is not None:
        o_other, r, x, mod, gn, w_out = fused
        d = x.shape[2]
        in_specs += [
            pl.BlockSpec((1, tblk, dvt), blk),
            pl.BlockSpec((1, tblk, dvt), blk),
            pl.BlockSpec((1, tblk, d), blk),
            pl.BlockSpec((1, 6, d), lambda i, j: (i, 0, 0)),
            _const_spec(gn.shape),
            _const_spec(w_out.shape),
        ]
        args += [o_other, r, x, mod, gn, w_out]
        out_specs = [pl.BlockSpec((1, tblk, d), blk), st_spec]
        out_shape = [jax.ShapeDtypeStruct(x.shape, F32), st_shape]
        scratch = [pltpu.VMEM((tblk, dvt), BF16)]
    elif want_o:
        out_specs = [pl.BlockSpec((1, tblk, dvt), blk), st_spec]
        out_shape = [jax.ShapeDtypeStruct((b, l, dvt), BF16), st_shape]
    else:
        out_specs = [st_spec]
        out_shape = [st_shape]
    outs = pl.pallas_call(
        functools.partial(_gla_scan_kernel, reverse=reverse, want_o=want_o,
                          fuse_out=fused is not None, tblk=tblk),
        grid=(b, nblk),
        in_specs=in_specs,
        out_specs=out_specs,
        out_shape=out_shape,
        scratch_shapes=scratch,
        compiler_params=_cparams("parallel", "arbitrary"),
        name="gla_scan_%s%s" % ("bwd" if reverse else "fwd", "_out" if fused is not None else ""),
    )(*args)
    if want_o:
        return outs[0], outs[1]
    return None, outs[0]


def _mla_proj_kernel(*refs, want_q):
    (x_ref, mod_ref, g_ref, cos_ref, sin_ref, wdq_ref, wdkv_ref, wdpe_ref, qln_ref, kvln_ref,
     wuq_ref, wuqs_ref, wuk_ref, wuvt_ref, qn_ref, kn_ref) = refs[:16]
    if want_q:
        q_ref, k_ref, vt_ref = refs[16:]
    else:
        k_ref, vt_ref = refs[16:]
    mod = mod_ref[0]
    kn = kn_ref[...]
    qn = qn_ref[...] * ((MLA_NOPE + MLA_ROPE) ** -0.5 * LOG2_E)
    sub = min(x_ref.shape[1], MLA_PROJ_ROWS)
    parts = [slice(i * sub, (i + 1) * sub) for i in range(x_ref.shape[1] // sub)]

    def rope_tables(gains, rows):
        return cos_ref[rows, :] * gains[1:2], sin_ref[rows, :] * gains[2:3]

    def rope_part(raw, raw_partner, tables):
        return (raw * tables[0] + raw_partner * tables[1]) * _rms_scale(raw, MLA_ROPE)

    def normed(c, gain_ref):
        return (c * _rms_scale(c, c.shape[-1]) * gain_ref[...]).astype(BF16)

    hs = [_modulate(x_ref[0, r, :], g_ref[...], mod[0:1], mod[1:2]).astype(BF16) for r in parts]
    c_kv = [_dot(h, wdkv_ref[...]) for h in hs]
    kpe = [_dot(h, wdpe_ref[...]) for h in hs]
    if want_q:
        c_q = [_dot(h, wdq_ref[...]) for h in hs]
    c_kv = [normed(c, kvln_ref) for c in c_kv]
    for c, r in zip(c_kv, parts):
        vt_ref[0, :, r] = _dot_nt(wuvt_ref[...], c).astype(BF16)
    kn_all = [_dot(c, wuk_ref[...]) for c in c_kv]
    k_rope = [rope_part(p[:, :ROPE_PAD], p[:, ROPE_PAD:], rope_tables(kn, r)).astype(BF16) for p, r in zip(kpe, parts)]
    if want_q:
        c_q = [normed(c, qln_ref) for c in c_q]
        qm = [_dot(c, wuq_ref[...]) for c in c_q]
        qs = [_dot(c, wuqs_ref[...]) for c in c_q]
    for i, r in enumerate(parts):
        if want_q:
            q_tables = rope_tables(qn, r)
        for hd in range(MLA_HEADS):
            k_nope = kn_all[i][:, hd * MLA_NOPE:(hd + 1) * MLA_NOPE]
            k_ref[0, hd, r, :MLA_NOPE] = (k_nope * _rms_scale(k_nope, MLA_NOPE) * kn[0:1]).astype(BF16)
            k_ref[0, hd, r, MLA_NOPE:] = k_rope[i]
            if want_q:
                base = hd * MLA_QPAD
                q_nope = qm[i][:, base:base + MLA_NOPE]
                q_ref[0, hd, r, :MLA_NOPE] = (q_nope * _rms_scale(q_nope, MLA_NOPE) * qn[0:1]).astype(BF16)
                q_rope = rope_part(qm[i][:, base + MLA_NOPE:base + MLA_QPAD],
                                   qs[i][:, hd * ROPE_PAD:(hd + 1) * ROPE_PAD], q_tables)
                q_ref[0, hd, r, MLA_NOPE:] = q_rope.astype(BF16)


def _mla_proj(x, mod, norm_g, cos, sin, w, *, want_q, tm):
    b, l, d = x.shape
    consts = [w["wdq"], w["wdkv"], w["wdpe"], w["qln"], w["kvln"], w["wuq"], w["wuqs"], w["wuk"],
              w["wuvt"], w["qn"], w["kn"]]
    hspec = pl.BlockSpec((1, MLA_HEADS, tm, MLA_QPAD), lambda i, j: (i, 0, j, 0))
    hshape = jax.ShapeDtypeStruct((b, MLA_HEADS, l, MLA_QPAD), BF16)
    out_specs = [hspec, pl.BlockSpec((1, MLA_HEADS * MLA_V, tm), lambda i, j: (i, 0, j))]
    out_shape = [hshape, jax.ShapeDtypeStruct((b, MLA_HEADS * MLA_V, l), BF16)]
    if want_q:
        out_specs = [hspec] + out_specs
        out_shape = [hshape] + out_shape
    return pl.pallas_call(
        functools.partial(_mla_proj_kernel, want_q=want_q),
        grid=(b, l // tm),
        in_specs=[
            pl.BlockSpec((1, tm, d), lambda i, j: (i, j, 0)),
            pl.BlockSpec((1, 6, d), lambda i, j: (i, 0, 0)),
            _const_spec(norm_g.shape),
            pl.BlockSpec((tm, ROPE_PAD), lambda i, j: (j, 0)),
            pl.BlockSpec((tm, ROPE_PAD), lambda i, j: (j, 0)),
        ] + [_const_spec(a.shape) for a in consts],
        out_specs=out_specs,
        out_shape=out_shape,
        compiler_params=_cparams("parallel", "parallel"),
        name="mla_proj_q" if want_q else "mla_proj_kv",
    )(x, mod, norm_g, cos, sin, *consts)


def _attn_kernel(*refs, with_latent, tk, heads):
    if with_latent:
        q_ref, kc_ref, vtc_ref, k_ref, vt_ref, o_ref = refs
    else:
        q_ref, kc_ref, vtc_ref, o_ref = refs
    dv = vtc_ref.shape[1] // heads
    for hd in range(heads):
        q = q_ref[0, hd]
        vrows = slice(hd * dv, (hd + 1) * dv)
        chunks = [(kc_ref, vtc_ref, 0, kc_ref.shape[2])]
        if with_latent:
            chunks += [(k_ref, vt_ref, j * tk, tk) for j in range(k_ref.shape[2] // tk)]

        def scores(idx):
            kr, _, start, size = chunks[idx]
            return _dot_nt(kr[0, hd, start:start + size, :], q)

        s_next = scores(0)
        m = denom = acc = None
        for idx, (_, vr, start, size) in enumerate(chunks):
            s = s_next
            if idx + 1 < len(chunks):
                s_next = scores(idx + 1)
            vt = vr[0, vrows, start:start + size]
            cmax = jnp.max(s, axis=0, keepdims=True)
            if m is None:
                m = cmax
                p = jnp.exp2(s - m)
                denom = jnp.sum(p, axis=0, keepdims=True)
                acc = _dot(vt, p.astype(BF16))
            else:
                m_new = jnp.maximum(m, cmax)
                alpha = jnp.exp2(m - m_new)
                p = jnp.exp2(s - m_new)
                denom = denom * alpha + jnp.sum(p, axis=0, keepdims=True)
                acc = acc * alpha + _dot(vt, p.astype(BF16))
                m = m_new
        o_ref[0, :, vrows] = (acc * (1.0 / denom)).T.astype(BF16)


def _attn_bounded_kernel(q_ref, kc_ref, vtc_ref, k_ref, vt_ref, o_ref, *, tq, tk):
    chunks = [(kc_ref, vtc_ref, 0, kc_ref.shape[2])]
    chunks += [(k_ref, vt_ref, j * tk, tk) for j in range(k_ref.shape[2] // tk)]
    for qi in range(q_ref.shape[2] // tq):
        cols = slice(qi * tq, (qi + 1) * tq)
        q = q_ref[0, 0, cols, :]

        def scores(idx):
            kr, _, start, size = chunks[idx]
            return _dot_nt(kr[0, 0, start:start + size, :], q)

        s_next = scores(0)
        acc = denom8 = None
        for idx, (_, vr, start, size) in enumerate(chunks):
            s = s_next
            if idx + 1 < len(chunks):
                s_next = scores(idx + 1)
            p = jnp.exp2(s)
            part = jnp.sum(p.reshape(size // SUBLANES, SUBLANES, p.shape[1]), axis=0)
            y = _dot(vr[0, :, start:start + size], p.astype(BF16))
            acc = y if acc is None else acc + y
            denom8 = part if denom8 is None else denom8 + part
        denom = jnp.sum(denom8, axis=0, keepdims=True)
        o_ref[0, cols, :] = (acc * (1.0 / denom)).T.astype(BF16)


def _attention(q, kc, vtc, k=None, vt=None, *, tq, tk=ATTN_EXACT_TILE, bounded_tq=None, heads=1):
    b, nh, lq, dq = q.shape
    lc = kc.shape[2]
    dv = vtc.shape[1] // nh
    with_latent = k is not None
    in_specs = [
        pl.BlockSpec((1, heads, tq, dq), lambda i, h, j: (i, h, j, 0)),
        pl.BlockSpec((1, heads, lc, dq), lambda i, h, j: (i, h, 0, 0)),
        pl.BlockSpec((1, heads * dv, lc), lambda i, h, j: (i, h, 0)),
    ]
    args = [q, kc, vtc]
    if with_latent:
        lk = k.shape[2]
        in_specs += [pl.BlockSpec((1, heads, lk, dq), lambda i, h, j: (i, h, 0, 0)),
                     pl.BlockSpec((1, heads * dv, lk), lambda i, h, j: (i, h, 0))]
        args += [k, vt]
    if bounded_tq is not None:
        assert heads == 1
        body = functools.partial(_attn_bounded_kernel, tq=bounded_tq, tk=tk)
        name = "mla_attn_bounded"
    else:
        body = functools.partial(_attn_kernel, with_latent=with_latent, tk=tk, heads=heads)
        name = "mla_attn" if with_latent else "mla_attn_ctx"
    return pl.pallas_call(
        body,
        grid=(b, nh // heads, lq // tq),
        in_specs=in_specs,
        out_specs=pl.BlockSpec((1, tq, heads * dv), lambda i, h, j: (i, j, h)),
        out_shape=jax.ShapeDtypeStruct((b, lq, nh * dv), BF16),
        compiler_params=_cparams("parallel", "parallel", "parallel"),
        name=name,
    )(*args)


def _latent_attention(bounded, q, kc, vtc, k, vt):
    lq = q.shape[2]
    lk = k.shape[2]
    fast = functools.partial(_attention, tq=min(lq, ATTN_Q_STEP), tk=min(lk, ATTN_K_CHUNK),
                             bounded_tq=min(lq, ATTN_Q_TILE))
    exact = functools.partial(_attention, tq=min(lq, ATTN_EXACT_TILE), tk=min(lk, ATTN_EXACT_TILE))
    return lax.cond(bounded, fast, exact, q, kc, vtc, k, vt)


def _ffn_kernel(*refs, tm, fc, mixer_out):
    if mixer_out:
        (xm_ref, xp_ref, xn_ref, om_ref, op_ref, on_ref, wo_ref, mod_ref, g_ref, wa_ref, wg_ref, ca_ref,
         cg_ref, wd_ref, xo_ref, h_scr, ua_scr, ug_scr, act_scr, o_scr) = refs
    else:
        (xm_ref, xp_ref, xn_ref, mod_ref, g_ref, wa_ref, wg_ref, ca_ref, cg_ref, wd_ref,
         xo_ref, h_scr, ua_scr, ug_scr, act_scr) = refs
    i = pl.program_id(1)
    mod = mod_ref[0]
    g = g_ref[...]
    hal = CONV_HALO

    def hidden(x):
        return _modulate(x, g, mod[3:4], mod[4:5])

    x_main, x_nxt, x_prv = xm_ref[0], xn_ref[0], xp_ref[0]
    if mixer_out:
        o_scr[0:tm] = om_ref[0]
        o_nxt = on_ref[0].astype(F32)[:hal]
        o_prv = op_ref[0].astype(F32)[BF16_ROWS - hal:]
        o_scr[tm:] = jnp.concatenate([o_nxt, o_prv], axis=0).astype(BF16)
        y = _dot(o_scr[...], wo_ref[...])
        x_main = x_main + mod[2:3] * y[0:tm]
        x_nxt = x_nxt + mod[2:3] * y[tm:tm + hal]
        x_prv = x_prv + mod[2:3] * y[tm + hal:]

    h_scr[0:tm] = hidden(x_main).astype(BF16)
    nxt = jnp.where(i == pl.num_programs(1) - 1, 0.0, hidden(x_nxt))
    prv = jnp.where(i == 0, 0.0, hidden(x_prv))
    h_scr[tm:] = jnp.concatenate([nxt, prv], axis=0).astype(BF16)

    def put(u_scr, slab, u):
        u_scr[slab, hal:hal + tm, :] = u[0:tm]
        u_scr[slab, hal + tm:, :] = u[tm:tm + hal]
        u_scr[slab, 0:hal, :] = u[tm + hal:]

    def conv(u_scr, slab, cw):
        return (u_scr[slab, hal - 1:hal - 1 + tm, :] * cw[0:1] + u_scr[slab, hal:hal + tm, :] * cw[1:2]
                + u_scr[slab, hal + 1:hal + 1 + tm, :] * cw[2:3] + cw[3:4])

    per = fc // LANES
    for ci in range(wa_ref.shape[1] // fc):
        cols = slice(ci * fc, (ci + 1) * fc)
        ua = _dot(h_scr[...], wa_ref[:, cols])
        ug = _dot(h_scr[...], wg_ref[:, cols])
        for s in range(per):
            put(ua_scr, ci * per + s, ua[:, s * LANES:(s + 1) * LANES])
            put(ug_scr, ci * per + s, ug[:, s * LANES:(s + 1) * LANES])
        for s in range(per):
            slab = ci * per + s
            lanes = slice(slab * LANES, (slab + 1) * LANES)
            a = conv(ua_scr, slab, ca_ref[:, lanes])
            gt = conv(ug_scr, slab, cg_ref[:, lanes])
            act_scr[:, lanes] = (gt * _sigmoid(gt) * a).astype(BF16)
    xo_ref[0] = x_main + mod[5:6] * _dot(act_scr[...], wd_ref[...])


def _conv_ffn(x, mod, norm_g, wa, wg, ca, cg, wd, *, tm, fc, mixer_out=None):
    b, l, d = x.shape
    hal = CONV_HALO
    nt = l // tm
    d_ff = wa.shape[1]

    def halo_specs(rows, width):
        per, last = tm // rows, l // rows - 1
        return [pl.BlockSpec((1, rows, width), lambda i, j: (i, jnp.maximum(j * per - 1, 0), 0)),
                pl.BlockSpec((1, rows, width), lambda i, j: (i, jnp.minimum((j + 1) * per, last), 0))]

    in_specs = [pl.BlockSpec((1, tm, d), lambda i, j: (i, j, 0))] + halo_specs(hal, d)
    args = [x, x, x]
    scratch = [
        pltpu.VMEM((tm + 2 * hal, d), BF16),
        pltpu.VMEM((d_ff // LANES, tm + 2 * hal, LANES), F32),
        pltpu.VMEM((d_ff // LANES, tm + 2 * hal, LANES), F32),
        pltpu.VMEM((tm, d_ff), BF16),
    ]
    if mixer_out is not None:
        o, w_out = mixer_out
        do = o.shape[2]
        in_specs += [pl.BlockSpec((1, tm, do), lambda i, j: (i, j, 0))] + halo_specs(BF16_ROWS, do)
        in_specs += [_const_spec(w_out.shape)]
        args += [o, o, o, w_out]
        scratch += [pltpu.VMEM((tm + 2 * hal, do), BF16)]
    consts = [norm_g, wa, wg, ca, cg, wd]
    return pl.pallas_call(
        functools.partial(_ffn_kernel, tm=tm, fc=fc, mixer_out=mixer_out is not None),
        grid=(b, nt),
        in_specs=in_specs + [pl.BlockSpec((1, 6, d), lambda i, j: (i, 0, 0))] + [_const_spec(a.shape) for a in consts],
        out_specs=pl.BlockSpec((1, tm, d), lambda i, j: (i, j, 0)),
        out_shape=jax.ShapeDtypeStruct(x.shape, F32),
        scratch_shapes=scratch,
        compiler_params=_cparams("parallel", "parallel"),
        name="conv_ffn_mix" if mixer_out is not None else "conv_ffn",
    )(*args, mod, *consts)


def _rope_tables(length):
    nf = MLA_ROPE // 4
    t = jnp.arange(length)
    pos = jnp.stack([(t // GRID_W).astype(F32), (t % GRID_W).astype(F32)], axis=1)
    inv = ROPE_THETA ** (-jnp.arange(nf, dtype=F32) / nf)
    ang = pos[:, :, None] * inv
    cos = jnp.cos(ang)[:, :, None, :] * jnp.ones((1, 1, 2, 1), F32)
    sin = jnp.sin(ang)[:, :, None, :] * jnp.array([-1.0, 1.0], F32)[None, None, :, None]
    pad = lambda a: jnp.pad(a.reshape(length, MLA_ROPE), ((0, 0), (0, ROPE_PAD - MLA_ROPE)))
    return pad(cos), pad(sin)


def _rope_partner(a):
    nf = MLA_ROPE // 4
    r = a.reshape(a.shape[:-1] + (2, 2, nf))
    return jnp.flip(r, axis=-2).reshape(a.shape)


def _pad_last(a, width):
    return jnp.pad(a, [(0, 0)] * (a.ndim - 1) + [(0, width - a.shape[-1])])


def _mla_weights(w_down, q_lora_norm, kv_lora_norm, w_uq, w_ukv, q_norm, k_norm):
    q_rank = q_lora_norm.shape[0]
    kv_rank = kv_lora_norm.shape[0]
    qk = MLA_NOPE + MLA_ROPE
    w_pe = w_down[:, q_rank + kv_rank:]
    wuq = w_uq.reshape(q_rank, MLA_HEADS, qk)
    wuq_rope = wuq[..., MLA_NOPE:]
    wukv = w_ukv.reshape(kv_rank, MLA_HEADS, MLA_NOPE + MLA_V)

    def max_sq_norm(g):
        return MLA_NOPE * jnp.max(g[:MLA_NOPE] ** 2) + MLA_ROPE * jnp.max(g[MLA_NOPE:] ** 2)

    bound = jnp.sqrt(max_sq_norm(q_norm) * max_sq_norm(k_norm)) * (qk ** -0.5 * LOG2_E * 1.02)

    def gains(g):
        rope = g[MLA_NOPE:]
        return jnp.stack([g[:MLA_NOPE], _pad_last(rope, ROPE_PAD), _pad_last(_rope_partner(rope), ROPE_PAD)])

    return {
        "bounded": bound <= ATTN_MAX_SCORE,
        "wdq": w_down[:, :q_rank].astype(BF16),
        "wdkv": w_down[:, q_rank:q_rank + kv_rank].astype(BF16),
        "wdpe": jnp.concatenate([_pad_last(w_pe, ROPE_PAD), _pad_last(_rope_partner(w_pe), ROPE_PAD)],
                                axis=1).astype(BF16),
        "qln": q_lora_norm[None, :],
        "kvln": kv_lora_norm[None, :],
        "wuq": _pad_last(wuq, MLA_QPAD).reshape(q_rank, MLA_HEADS * MLA_QPAD).astype(BF16),
        "wuqs": _pad_last(_rope_partner(wuq_rope), ROPE_PAD).reshape(q_rank, MLA_HEADS * ROPE_PAD).astype(BF16),
        "wuk": wukv[..., :MLA_NOPE].reshape(kv_rank, MLA_HEADS * MLA_NOPE).astype(BF16),
        "wuvt": wukv[..., MLA_NOPE:].reshape(kv_rank, MLA_HEADS * MLA_V).T.astype(BF16),
        "qn": gains(q_norm),
        "kn": gains(k_norm),
    }


def kernel(x, c, ctx, c_ctx, w_ada, b_ada, norm_mix, norm_ffn, gla_w_in, gla_gate_w1, gla_gate_w2, gla_gate_b, gla_out_norm, gla_w_out, mla_w_down, mla_q_lora_norm, mla_kv_lora_norm, mla_w_uq, mla_w_ukv, mla_q_norm, mla_k_norm, mla_w_out, ffn_w_up, ffn_conv_w, ffn_conv_b, ffn_w_down):
    bsz, seq, d = x.shape
    lc = ctx.shape[1]
    depth = w_ada.shape[0]
    d_ff = ffn_w_down.shape[1]
    dqk = gla_gate_w2.shape[-1]
    rank = gla_gate_w1.shape[-1]
    dv = gla_out_norm.shape[-1]

    cond_rows = -(-(bsz + 1) // 8) * 8
    cond = jnp.zeros((cond_rows, d), F32).at[:bsz].set(c).at[bsz].set(c_ctx)
    mods = _ada_all(cond, w_ada, b_ada)

    cos_l, sin_l = _rope_tables(seq)
    cos_c = _pad_last(jnp.ones((lc, MLA_ROPE), F32), ROPE_PAD)
    sin_c = jnp.zeros((lc, ROPE_PAD), F32)

    tm_l = min(seq, ROW_TILE)
    tm_c = min(lc, ROW_TILE)
    xc = ctx
    for i in range(depth):
        last = i == depth - 1
        j = i // 2
        mod_l = mods[i, :bsz].reshape(bsz, 6, d)
        mod_c = jnp.broadcast_to(mods[i, bsz].reshape(1, 6, d), (bsz, 6, d))
        g_mix = norm_mix[i][None, :]
        mix_l = mix_c = None
        if i % 2 == 0:
            w_in = gla_w_in[j].astype(BF16)
            w1 = _pad_last(jnp.concatenate([gla_gate_w1[j, 0], gla_gate_w1[j, 1]], axis=1),
                           GLA_RANK_PAD).astype(BF16)
            w2 = jnp.zeros((GLA_RANK_PAD, 2 * dqk), F32)
            w2 = w2.at[:rank, :dqk].set(gla_gate_w2[j, 0]).at[rank:2 * rank, dqk:].set(gla_gate_w2[j, 1])
            w2 = w2.astype(BF16)
            gb = gla_gate_b[j].reshape(1, 2 * dqk)
            gn = gla_out_norm[j][None, :]
            w_out = gla_w_out[j].astype(BF16)
            s0 = jnp.zeros((bsz, GLA_HEADS, dv, dqk // GLA_HEADS), F32)

            vc, rc, *dirs_c = _gla_proj(xc, mod_c, g_mix, w_in, w1, w2, gb, tm=tm_c)
            v, r, *dirs_l = _gla_proj(x, mod_l, g_mix, w_in, w1, w2, gb, tm=tm_l)
            fwd_c, bwd_c, fwd_l, bwd_l = dirs_c[:2], dirs_c[2:], dirs_l[:2], dirs_l[2:]
            if last:
                _, s_fwd = _gla_scan(*fwd_c, vc, s0, direction=0, want_o=False)
                _, s_bwd = _gla_scan(*bwd_c, vc, s0, direction=1, want_o=False)
            else:
                oc_b, s_bwd = _gla_scan(*bwd_c, vc, s0, direction=1)
                xc, s_fwd = _gla_scan(*fwd_c, vc, s0, direction=0, fused=(oc_b, rc, xc, mod_c, gn, w_out))
            o_b, _ = _gla_scan(*bwd_l, v, s_bwd, direction=1)
            x, _ = _gla_scan(*fwd_l, v, s_fwd, direction=0, fused=(o_b, r, x, mod_l, gn, w_out))
        else:
            w = _mla_weights(mla_w_down[j], mla_q_lora_norm[j], mla_kv_lora_norm[j], mla_w_uq[j],
                             mla_w_ukv[j], mla_q_norm[j], mla_k_norm[j])
            w_out = mla_w_out[j].astype(BF16)
            if last:
                kc, vtc = _mla_proj(xc, mod_c, g_mix, cos_c, sin_c, w, want_q=False, tm=tm_c)
            else:
                qc, kc, vtc = _mla_proj(xc, mod_c, g_mix, cos_c, sin_c, w, want_q=True, tm=tm_c)
            q, k, vt = _mla_proj(x, mod_l, g_mix, cos_l, sin_l, w, want_q=True, tm=min(seq, MLA_PROJ_TILE))
            mix_l = (_latent_attention(w["bounded"], q, kc, vtc, k, vt), w_out)
            if not last:
                mix_c = (_attention(qc, kc, vtc, tq=min(lc, ATTN_EXACT_TILE), heads=MLA_HEADS), w_out)

        g_ffn = norm_ffn[i][None, :]
        wa = ffn_w_up[i, :, :d_ff].astype(BF16)
        wg = ffn_w_up[i, :, d_ff:].astype(BF16)
        conv = jnp.concatenate([ffn_conv_w[i], ffn_conv_b[i][None, :]], axis=0)
        ca, cg = conv[:, :d_ff], conv[:, d_ff:]
        wd = ffn_w_down[i].astype(BF16)
        x = _conv_ffn(x, mod_l, g_ffn, wa, wg, ca, cg, wd, tm=tm_l, fc=FFN_COLS, mixer_out=mix_l)
        if not last:
            xc = _conv_ffn(xc, mod_c, g_ffn, wa, wg, ca, cg, wd, tm=tm_c, fc=FFN_COLS, mixer_out=mix_c)
    return x
```

```python
import functools

import jax
import jax.numpy as jnp
from jax import lax
from jax.experimental import pallas as pl
from jax.experimental.pallas import tpu as pltpu

F32 = jnp.float32
BF16 = jnp.bfloat16

LANES = 128
SUBLANES = 8
BF16_ROWS = 16
EPS = 1e-6
GRID_W = 64
ROPE_THETA = 10000.0
LOG2_E = 1.4426950408889634

GLA_HEADS = 4
GLA_GATE_NORMALIZER = 16.0
GLA_CHUNK = 128
GLA_RANK_PAD = 128

MLA_HEADS = 8
MLA_NOPE = 128
MLA_ROPE = 64
MLA_V = 128
MLA_QPAD = 256
ROPE_PAD = MLA_QPAD - MLA_NOPE
MLA_PROJ_ROWS = 256
ATTN_MAX_SCORE = 50.0

ROW_TILE = 512
PROJ_TILE = 1024
SCAN_BLOCK = 1024
ATTN_Q_STEP = 4096
ATTN_Q_TILE = 2048
ATTN_K_CHUNK = 2048
ATTN_EXACT_TILE = 512
FFN_COLS = 256
CONV_HALO = SUBLANES
VMEM_LIMIT = 56 * 1024 * 1024


def _cparams(*sem):
    return pltpu.CompilerParams(dimension_semantics=sem, vmem_limit_bytes=VMEM_LIMIT)


def _const_spec(shape):
    nd = len(shape)
    return pl.BlockSpec(shape, lambda *_: (0,) * nd, pipeline_mode=pl.Buffered(1))


def _dot(a, b):
    return jnp.dot(a, b, preferred_element_type=F32)


def _dot_nt(a, b):
    return lax.dot_general(a, b, (((1,), (1,)), ((), ())), preferred_element_type=F32)


def _dot_tn(a, b):
    return lax.dot_general(a, b, (((0,), (0,)), ((), ())), preferred_element_type=F32)


def _sigmoid(x):
    return 1.0 / (1.0 + jnp.exp(-x))


def _rms_scale(x, width):
    ss = jnp.sum(x * x, axis=-1, keepdims=True)
    return lax.rsqrt(ss * (1.0 / width) + EPS)


def _modulate(x, g, shift, scale):
    return x * _rms_scale(x, x.shape[-1]) * (g * (1.0 + scale)) + shift


def _ada_kernel(cond_ref, w_ref, b_ref, o_ref):
    cond = cond_ref[...]
    s = cond * _sigmoid(cond)
    o_ref[0] = jnp.dot(s, w_ref[0], preferred_element_type=F32,
                       precision=lax.Precision.HIGHEST) + b_ref[0]


def _ada_all(cond, w_ada, b_ada):
    depth, d, d6 = w_ada.shape
    rows = cond.shape[0]
    cols = 2 * d
    return pl.pallas_call(
        _ada_kernel,
        grid=(depth, d6 // cols),
        in_specs=[
            pl.BlockSpec((rows, d), lambda i, j: (0, 0)),
            pl.BlockSpec((1, d, cols), lambda i, j: (i, 0, j)),
            pl.BlockSpec((1, 1, cols), lambda i, j: (i, 0, j)),
        ],
        out_specs=pl.BlockSpec((1, rows, cols), lambda i, j: (i, 0, j)),
        out_shape=jax.ShapeDtypeStruct((depth, rows, d6), F32),
        compiler_params=_cparams("parallel", "parallel"),
        name="ada_mod",
    )(cond, w_ada, b_ada.reshape(depth, 1, d6))


def _gla_proj_kernel(x_ref, mod_ref, g_ref, win_ref, w1_ref, w2_ref, gb_ref,
                     v_ref, r_ref, qkf_ref, ff_ref, qkb_ref, fb_ref, *, dqk):
    mod = mod_ref[0]
    h = _modulate(x_ref[0], g_ref[...], mod[0:1], mod[1:2]).astype(BF16)
    c = GLA_CHUNK
    dk = dqk // GLA_HEADS
    dv = v_ref.shape[2]
    low = _dot(h, w1_ref[...]).astype(BF16)
    z = _dot(low, w2_ref[...]) + gb_ref[...]
    pqk = _dot(h, win_ref[:, :2 * dqk])
    unit = LOG2_E / GLA_GATE_NORMALIZER
    gates = jnp.minimum(z, 0.0) * unit - jnp.log(1.0 + jnp.exp(-jnp.abs(z))) * unit
    row = lax.broadcasted_iota(jnp.int32, (c, c), 0)
    col = lax.broadcasted_iota(jnp.int32, (c, c), 1)
    for direction, (qk_ref, f_ref) in enumerate(((qkf_ref, ff_ref), (qkb_ref, fb_ref))):
        reverse = direction == 1
        tri = ((col >= row) if reverse else (col <= row)).astype(BF16)
        tri2 = jnp.concatenate([tri, tri], axis=1)
        i_mid = c // 2 if reverse else c // 2 - 1
        i_end = 0 if reverse else c - 1
        g = gates[:, direction * dqk:(direction + 1) * dqk]
        g_hi = g.astype(BF16)
        g_lo = (g - g_hi.astype(F32)).astype(BF16)
        for ci in range(x_ref.shape[1] // c):
            rows = slice(ci * c, (ci + 1) * c)
            cum = _dot(tri2, jnp.concatenate([g_hi[rows], g_lo[rows]], axis=0))
            c_mid = cum[i_mid:i_mid + 1, :]
            c_end = cum[i_end:i_end + 1, :]
            rel = cum - c_mid
            qk_ref[0, rows, :dqk] = (pqk[rows, :dqk] * (jnp.exp2(rel) * (dk ** -0.5))).astype(BF16)
            qk_ref[0, rows, dqk:] = (pqk[rows, dqk:] * jnp.exp2(-rel)).astype(BF16)
            factors = jnp.concatenate([jnp.exp2(c_mid), jnp.exp2(c_end - c_mid), jnp.exp2(c_end)], axis=1)
            f_ref[0, ci * SUBLANES:(ci + 1) * SUBLANES, :] = jnp.broadcast_to(factors, (SUBLANES, 3 * dqk))
    pvr = _dot(h, win_ref[:, 2 * dqk:])
    v_ref[0] = pvr[:, :dv].astype(BF16)
    r_ref[0] = pvr[:, dv:].astype(BF16)


def _gla_proj(x, mod, norm_g, w_in, w1, w2, gb, *, tm):
    b, l, d = x.shape
    dqk = w2.shape[1] // 2
    dv = (w_in.shape[1] - 2 * dqk) // 2
    drows = tm // GLA_CHUNK * SUBLANES
    row = lambda i, j: (i, j, 0)
    tok = lambda width, dtype: (pl.BlockSpec((1, tm, width), row), jax.ShapeDtypeStruct((b, l, width), dtype))
    fac = (pl.BlockSpec((1, drows, 3 * dqk), row),
           jax.ShapeDtypeStruct((b, l // GLA_CHUNK * SUBLANES, 3 * dqk), F32))
    outs = [tok(dv, BF16), tok(dv, BF16)] + 2 * [tok(2 * dqk, BF16), fac]
    return pl.pallas_call(
        functools.partial(_gla_proj_kernel, dqk=dqk),
        grid=(b, l // tm),
        in_specs=[
            pl.BlockSpec((1, tm, d), row),
            pl.BlockSpec((1, 6, d), lambda i, j: (i, 0, 0)),
            _const_spec(norm_g.shape),
            _const_spec(w_in.shape),
            _const_spec(w1.shape),
            _const_spec(w2.shape),
            _const_spec(gb.shape),
        ],
        out_specs=[o[0] for o in outs],
        out_shape=[o[1] for o in outs],
        compiler_params=_cparams("parallel", "parallel"),
        name="gla_proj",
    )(x, mod, norm_g, w_in, w1, w2, gb)


def _gla_scan_kernel(*refs, reverse, want_o, fuse_out, tblk):
    if fuse_out:
        (qk_ref, v_ref, f_ref, s0_ref, ob_ref, r_ref, x_ref, mod_ref, gn_ref, wout_ref,
         xo_ref, st_ref, y_scr) = refs
    elif want_o:
        qk_ref, v_ref, f_ref, s0_ref, o_ref, st_ref = refs
    else:
        qk_ref, v_ref, f_ref, s0_ref, st_ref = refs

    @pl.when(pl.program_id(1) == 0)
    def _():
        st_ref[...] = s0_ref[...]

    c = GLA_CHUNK
    dqk = qk_ref.shape[2] // 2
    dk = dqk // GLA_HEADS
    dv = v_ref.shape[2] // GLA_HEADS
    row = lax.broadcasted_iota(jnp.int32, (c, c), 0)
    col = lax.broadcasted_iota(jnp.int32, (c, c), 1)
    mask = (col >= row) if reverse else (col <= row)

    nch = tblk // c
    states = [st_ref[0, h] for h in range(GLA_HEADS)]
    for ci in (reversed(range(nch)) if reverse else range(nch)):
        rows = slice(ci * c, (ci + 1) * c)
        frow = slice(ci * SUBLANES, ci * SUBLANES + 1)
        for h in range(GLA_HEADS):
            qs = slice(h * dk, (h + 1) * dk)
            ks = slice(dqk + h * dk, dqk + (h + 1) * dk)
            vs = slice(h * dv, (h + 1) * dv)
            v = v_ref[0, rows, vs]
            st = states[h]
            k_in = qk_ref[0, rows, ks]
            k_st = (k_in.astype(F32) * f_ref[0, frow, ks]).astype(BF16)
            if want_o:
                q_in = qk_ref[0, rows, qs]
                q_st = (q_in.astype(F32) * f_ref[0, frow, qs]).astype(BF16)
                scores = jnp.where(mask, _dot_nt(q_in, k_in), 0.0)
                o = _dot(scores.astype(BF16), v) + _dot_nt(q_st, st.astype(BF16))
                if fuse_out:
                    o = o + ob_ref[0, rows, vs].astype(F32)
                    r = r_ref[0, rows, vs].astype(F32)
                    y = o * _rms_scale(o, dv) * gn_ref[...] * (r * _sigmoid(r))
                    y_scr[rows, vs] = y.astype(BF16)
                else:
                    o_ref[0, rows, vs] = o.astype(BF16)
            decay_end = f_ref[0, frow, 2 * dqk + h * dk:2 * dqk + (h + 1) * dk]
            states[h] = st * decay_end + _dot_tn(v, k_st)
    for h in range(GLA_HEADS):
        st_ref[0, h] = states[h]

    if fuse_out:
        gate = mod_ref[0][2:3]
        xo_ref[0] = x_ref[0] + gate * _dot(y_scr[...], wout_ref[...])


def _gla_scan(qk, factors, v, s0, *, direction, want_o=True, fused=None):
    b, l, dq2 = qk.shape
    dvt = v.shape[2]
    tblk = min(l, SCAN_BLOCK)
    nblk = l // tblk
    reverse = direction == 1
    blk = (lambda i, j: (i, nblk - 1 - j, 0)) if reverse else (lambda i, j: (i, j, 0))
    st_spec = pl.BlockSpec((1,) + s0.shape[1:], lambda i, j: (i, 0, 0, 0))
    in_specs = [
        pl.BlockSpec((1, tblk, dq2), blk),
        pl.BlockSpec((1, tblk, dvt), blk),
        pl.BlockSpec((1, tblk // GLA_CHUNK * SUBLANES, factors.shape[2]), blk),
        st_spec,
    ]
    args = [qk, v, factors, s0]
    st_shape = jax.ShapeDtypeStruct(s0.shape, F32)
    scratch = []
    if fused is not None:
        o_other, r, x, mod, gn, w_out = fused
        d = x.shape[2]
        in_specs += [
            pl.BlockSpec((1, tblk, dvt), blk),
            pl.BlockSpec((1, tblk, dvt), blk),
            pl.BlockSpec((1, tblk, d), blk),
            pl.BlockSpec((1, 6, d), lambda i, j: (i, 0, 0)),
            _const_spec(gn.shape),
            _const_spec(w_out.shape),
        ]
        args += [o_other, r, x, mod, gn, w_out]
        out_specs = [pl.BlockSpec((1, tblk, d), blk), st_spec]
        out_shape = [jax.ShapeDtypeStruct(x.shape, F32), st_shape]
        scratch = [pltpu.VMEM((tblk, dvt), BF16)]
    elif want_o:
        out_specs = [pl.BlockSpec((1, tblk, dvt), blk), st_spec]
        out_shape = [jax.ShapeDtypeStruct((b, l, dvt), BF16), st_shape]
    else:
        out_specs = [st_spec]
        out_shape = [st_shape]
    outs = pl.pallas_call(
        functools.partial(_gla_scan_kernel, reverse=reverse, want_o=want_o,
                          fuse_out=fused is not None, tblk=tblk),
        grid=(b, nblk),
        in_specs=in_specs,
        out_specs=out_specs,
        out_shape=out_shape,
        scratch_shapes=scratch,
        compiler_params=_cparams("parallel", "arbitrary"),
        name="gla_scan_%s%s" % ("bwd" if reverse else "fwd", "_out" if fused is not None else ""),
    )(*args)
    if want_o:
        return outs[0], outs[1]
    return None, outs[0]


def _mla_proj_kernel(*refs, want_q):
    (x_ref, mod_ref, g_ref, cos_ref, sin_ref, wdq_ref, wdkv_ref, wdpe_ref, qln_ref, kvln_ref,
     wuq_ref, wuqs_ref, wuk_ref, wuvt_ref, qn_ref, kn_ref) = refs[:16]
    if want_q:
        q_ref, k_ref, vt_ref = refs[16:]
    else:
        k_ref, vt_ref = refs[16:]
    mod = mod_ref[0]
    kn = kn_ref[...]
    qn = qn_ref[...] * ((MLA_NOPE + MLA_ROPE) ** -0.5 * LOG2_E)
    sub = min(x_ref.shape[1], MLA_PROJ_ROWS)
    parts = [slice(i * sub, (i + 1) * sub) for i in range(x_ref.shape[1] // sub)]

    def rope_tables(gains, rows):
        return cos_ref[rows, :] * gains[1:2], sin_ref[rows, :] * gains[2:3]

    def rope_part(raw, raw_partner, tables):
        return (raw * tables[0] + raw_partner * tables[1]) * _rms_scale(raw, MLA_ROPE)

    def normed(c, gain_ref):
        return (c * _rms_scale(c, c.shape[-1]) * gain_ref[...]).astype(BF16)

    hs = [_modulate(x_ref[0, r, :], g_ref[...], mod[0:1], mod[1:2]).astype(BF16) for r in parts]
    c_kv = [_dot(h, wdkv_ref[...]) for h in hs]
    kpe = [_dot(h, wdpe_ref[...]) for h in hs]
    if want_q:
        c_q = [_dot(h, wdq_ref[...]) for h in hs]
    c_kv = [normed(c, kvln_ref) for c in c_kv]
    for c, r in zip(c_kv, parts):
        vt_ref[0, :, r] = _dot_nt(wuvt_ref[...], c).astype(BF16)
    kn_all = [_dot(c, wuk_ref[...]) for c in c_kv]
    k_rope = [rope_part(p[:, :ROPE_PAD], p[:, ROPE_PAD:], rope_tables(kn, r)).astype(BF16) for p, r in zip(kpe, parts)]
    if want_q:
        c_q = [normed(c, qln_ref) for c in c_q]
        qm = [_dot(c, wuq_ref[...]) for c in c_q]
        qs = [_dot(c, wuqs_ref[...]) for c in c_q]
    for i, r in enumerate(parts):
        if want_q:
            q_tables = rope_tables(qn, r)
        for hd in range(MLA_HEADS):
            k_nope = kn_all[i][:, hd * MLA_NOPE:(hd + 1) * MLA_NOPE]
            k_ref[0, hd, r, :MLA_NOPE] = (k_nope * _rms_scale(k_nope, MLA_NOPE) * kn[0:1]).astype(BF16)
            k_ref[0, hd, r, MLA_NOPE:] = k_rope[i]
            if want_q:
                base = hd * MLA_QPAD
                q_nope = qm[i][:, base:base + MLA_NOPE]
                q_ref[0, hd, r, :MLA_NOPE] = (q_nope * _rms_scale(q_nope, MLA_NOPE) * qn[0:1]).astype(BF16)
                q_rope = rope_part(qm[i][:, base + MLA_NOPE:base + MLA_QPAD],
                                   qs[i][:, hd * ROPE_PAD:(hd + 1) * ROPE_PAD], q_tables)
                q_ref[0, hd, r, MLA_NOPE:] = q_rope.astype(BF16)


def _mla_proj(x, mod, norm_g, cos, sin, w, *, want_q, tm):
    b, l, d = x.shape
    consts = [w["wdq"], w["wdkv"], w["wdpe"], w["qln"], w["kvln"], w["wuq"], w["wuqs"], w["wuk"],
              w["wuvt"], w["qn"], w["kn"]]
    hspec = pl.BlockSpec((1, MLA_HEADS, tm, MLA_QPAD), lambda i, j: (i, 0, j, 0))
    hshape = jax.ShapeDtypeStruct((b, MLA_HEADS, l, MLA_QPAD), BF16)
    out_specs = [hspec, pl.BlockSpec((1, MLA_HEADS * MLA_V, tm), lambda i, j: (i, 0, j))]
    out_shape = [hshape, jax.ShapeDtypeStruct((b, MLA_HEADS * MLA_V, l), BF16)]
    if want_q:
        out_specs = [hspec] + out_specs
        out_shape = [hshape] + out_shape
    return pl.pallas_call(
        functools.partial(_mla_proj_kernel, want_q=want_q),
        grid=(b, l // tm),
        in_specs=[
            pl.BlockSpec((1, tm, d), lambda i, j: (i, j, 0)),
            pl.BlockSpec((1, 6, d), lambda i, j: (i, 0, 0)),
            _const_spec(norm_g.shape),
            pl.BlockSpec((tm, ROPE_PAD), lambda i, j: (j, 0)),
            pl.BlockSpec((tm, ROPE_PAD), lambda i, j: (j, 0)),
        ] + [_const_spec(a.shape) for a in consts],
        out_specs=out_specs,
        out_shape=out_shape,
        compiler_params=_cparams("parallel", "parallel"),
        name="mla_proj_q" if want_q else "mla_proj_kv",
    )(x, mod, norm_g, cos, sin, *consts)


def _attn_kernel(*refs, with_latent, tk, heads):
    if with_latent:
        q_ref, kc_ref, vtc_ref, k_ref, vt_ref, o_ref = refs
    else:
        q_ref, kc_ref, vtc_ref, o_ref = refs
    dv = vtc_ref.shape[1] // heads
    for hd in range(heads):
        q = q_ref[0, hd]
        vrows = slice(hd * dv, (hd + 1) * dv)
        chunks = [(kc_ref, vtc_ref, 0, kc_ref.shape[2])]
        if with_latent:
            chunks += [(k_ref, vt_ref, j * tk, tk) for j in range(k_ref.shape[2] // tk)]

        def scores(idx):
            kr, _, start, size = chunks[idx]
            return _dot_nt(kr[0, hd, start:start + size, :], q)

        s_next = scores(0)
        m = denom = acc = None
        for idx, (_, vr, start, size) in enumerate(chunks):
            s = s_next
            if idx + 1 < len(chunks):
                s_next = scores(idx + 1)
            vt = vr[0, vrows, start:start + size]
            cmax = jnp.max(s, axis=0, keepdims=True)
            if m is None:
                m = cmax
                p = jnp.exp2(s - m)
                denom = jnp.sum(p, axis=0, keepdims=True)
                acc = _dot(vt, p.astype(BF16))
            else:
                m_new = jnp.maximum(m, cmax)
                alpha = jnp.exp2(m - m_new)
                p = jnp.exp2(s - m_new)
                denom = denom * alpha + jnp.sum(p, axis=0, keepdims=True)
                acc = acc * alpha + _dot(vt, p.astype(BF16))
                m = m_new
        o_ref[0, :, vrows] = (acc * (1.0 / denom)).T.astype(BF16)


def _attn_bounded_kernel(q_ref, kc_ref, vtc_ref, k_ref, vt_ref, o_ref, *, tq, tk):
    chunks = [(kc_ref, vtc_ref, 0, kc_ref.shape[2])]
    chunks += [(k_ref, vt_ref, j * tk, tk) for j in range(k_ref.shape[2] // tk)]
    for qi in range(q_ref.shape[2] // tq):
        cols = slice(qi * tq, (qi + 1) * tq)
        q = q_ref[0, 0, cols, :]

        def scores(idx):
            kr, _, start, size = chunks[idx]
            return _dot_nt(kr[0, 0, start:start + size, :], q)

        s_next = scores(0)
        acc = denom8 = None
        for idx, (_, vr, start, size) in enumerate(chunks):
            s = s_next
            if idx + 1 < len(chunks):
                s_next = scores(idx + 1)
            p = jnp.exp2(s)
            part = jnp.sum(p.reshape(size // SUBLANES, SUBLANES, p.shape[1]), axis=0)
            y = _dot(vr[0, :, start:start + size], p.astype(BF16))
            acc = y if acc is None else acc + y
            denom8 = part if denom8 is None else denom8 + part
        denom = jnp.sum(denom8, axis=0, keepdims=True)
        o_ref[0, cols, :] = (acc * (1.0 / denom)).T.astype(BF16)


def _attention(q, kc, vtc, k=None, vt=None, *, tq, tk=ATTN_EXACT_TILE, bounded_tq=None, heads=1):
    b, nh, lq, dq = q.shape
    lc = kc.shape[2]
    dv = vtc.shape[1] // nh
    with_latent = k is not None
    in_specs = [
        pl.BlockSpec((1, heads, tq, dq), lambda i, h, j: (i, h, j, 0)),
        pl.BlockSpec((1, heads, lc, dq), lambda i, h, j: (i, h, 0, 0)),
        pl.BlockSpec((1, heads * dv, lc), lambda i, h, j: (i, h, 0)),
    ]
    args = [q, kc, vtc]
    if with_latent:
        lk = k.shape[2]
        in_specs += [pl.BlockSpec((1, heads, lk, dq), lambda i, h, j: (i, h, 0, 0)),
                     pl.BlockSpec((1, heads * dv, lk), lambda i, h, j: (i, h, 0))]
        args += [k, vt]
    if bounded_tq is not None:
        assert heads == 1
        body = functools.partial(_attn_bounded_kernel, tq=bounded_tq, tk=tk)
        name = "mla_attn_bounded"
    else:
        body = functools.partial(_attn_kernel, with_latent=with_latent, tk=tk, heads=heads)
        name = "mla_attn" if with_latent else "mla_attn_ctx"
    return pl.pallas_call(
        body,
        grid=(b, nh // heads, lq // tq),
        in_specs=in_specs,
        out_specs=pl.BlockSpec((1, tq, heads * dv), lambda i, h, j: (i, j, h)),
        out_shape=jax.ShapeDtypeStruct((b, lq, nh * dv), BF16),
        compiler_params=_cparams("parallel", "parallel", "parallel"),
        name=name,
    )(*args)


def _latent_attention(bounded, q, kc, vtc, k, vt):
    lq = q.shape[2]
    lk = k.shape[2]
    fast = functools.partial(_attention, tq=min(lq, ATTN_Q_STEP), tk=min(lk, ATTN_K_CHUNK),
                             bounded_tq=min(lq, ATTN_Q_TILE))
    exact = functools.partial(_attention, tq=min(lq, ATTN_EXACT_TILE), tk=min(lk, ATTN_EXACT_TILE))
    return lax.cond(bounded, fast, exact, q, kc, vtc, k, vt)


def _ffn_kernel(*refs, tm, fc, mixer_out):
    if mixer_out:
        (xm_ref, xp_ref, xn_ref, om_ref, op_ref, on_ref, wo_ref, mod_ref, g_ref, wa_ref, wg_ref, ca_ref,
         cg_ref, wd_ref, xo_ref, h_scr, ua_scr, ug_scr, act_scr, o_scr) = refs
    else:
        (xm_ref, xp_ref, xn_ref, mod_ref, g_ref, wa_ref, wg_ref, ca_ref, cg_ref, wd_ref,
         xo_ref, h_scr, ua_scr, ug_scr, act_scr) = refs
    i = pl.program_id(1)
    mod = mod_ref[0]
    g = g_ref[...]
    hal = CONV_HALO

    def hidden(x):
        return _modulate(x, g, mod[3:4], mod[4:5])

    x_main, x_nxt, x_prv = xm_ref[0], xn_ref[0], xp_ref[0]
    if mixer_out:
        o_scr[0:tm] = om_ref[0]
        o_nxt = on_ref[0].astype(F32)[:hal]
        o_prv = op_ref[0].astype(F32)[BF16_ROWS - hal:]
        o_scr[tm:] = jnp.concatenate([o_nxt, o_prv], axis=0).astype(BF16)
        y = _dot(o_scr[...], wo_ref[...])
        x_main = x_main + mod[2:3] * y[0:tm]
        x_nxt = x_nxt + mod[2:3] * y[tm:tm + hal]
        x_prv = x_prv + mod[2:3] * y[tm + hal:]

    h_scr[0:tm] = hidden(x_main).astype(BF16)
    nxt = jnp.where(i == pl.num_programs(1) - 1, 0.0, hidden(x_nxt))
    prv = jnp.where(i == 0, 0.0, hidden(x_prv))
    h_scr[tm:] = jnp.concatenate([nxt, prv], axis=0).astype(BF16)

    def put(u_scr, slab, u):
        u_scr[slab, hal:hal + tm, :] = u[0:tm]
        u_scr[slab, hal + tm:, :] = u[tm:tm + hal]
        u_scr[slab, 0:hal, :] = u[tm + hal:]

    def conv(u_scr, slab, cw):
        return (u_scr[slab, hal - 1:hal - 1 + tm, :] * cw[0:1] + u_scr[slab, hal:hal + tm, :] * cw[1:2]
                + u_scr[slab, hal + 1:hal + 1 + tm, :] * cw[2:3] + cw[3:4])

    per = fc // LANES
    for ci in range(wa_ref.shape[1] // fc):
        cols = slice(ci * fc, (ci + 1) * fc)
        ua = _dot(h_scr[...], wa_ref[:, cols])
        ug = _dot(h_scr[...], wg_ref[:, cols])
        for s in range(per):
            put(ua_scr, ci * per + s, ua[:, s * LANES:(s + 1) * LANES])
            put(ug_scr, ci * per + s, ug[:, s * LANES:(s + 1) * LANES])
        for s in range(per):
            slab = ci * per + s
            lanes = slice(slab * LANES, (slab + 1) * LANES)
            a = conv(ua_scr, slab, ca_ref[:, lanes])
            gt = conv(ug_scr, slab, cg_ref[:, lanes])
            act_scr[:, lanes] = (gt * _sigmoid(gt) * a).astype(BF16)
    xo_ref[0] = x_main + mod[5:6] * _dot(act_scr[...], wd_ref[...])


def _conv_ffn(x, mod, norm_g, wa, wg, ca, cg, wd, *, tm, fc, mixer_out=None):
    b, l, d = x.shape
    hal = CONV_HALO
    nt = l // tm
    d_ff = wa.shape[1]

    def halo_specs(rows, width):
        per, last = tm // rows, l // rows - 1
        return [pl.BlockSpec((1, rows, width), lambda i, j: (i, jnp.maximum(j * per - 1, 0), 0)),
                pl.BlockSpec((1, rows, width), lambda i, j: (i, jnp.minimum((j + 1) * per, last), 0))]

    in_specs = [pl.BlockSpec((1, tm, d), lambda i, j: (i, j, 0))] + halo_specs(hal, d)
    args = [x, x, x]
    scratch = [
        pltpu.VMEM((tm + 2 * hal, d), BF16),
        pltpu.VMEM((d_ff // LANES, tm + 2 * hal, LANES), F32),
        pltpu.VMEM((d_ff // LANES, tm + 2 * hal, LANES), F32),
        pltpu.VMEM((tm, d_ff), BF16),
    ]
    if mixer_out is not None:
        o, w_out = mixer_out
        do = o.shape[2]
        in_specs += [pl.BlockSpec((1, tm, do), lambda i, j: (i, j, 0))] + halo_specs(BF16_ROWS, do)
        in_specs += [_const_spec(w_out.shape)]
        args += [o, o, o, w_out]
        scratch += [pltpu.VMEM((tm + 2 * hal, do), BF16)]
    consts = [norm_g, wa, wg, ca, cg, wd]
    return pl.pallas_call(
        functools.partial(_ffn_kernel, tm=tm, fc=fc, mixer_out=mixer_out is not None),
        grid=(b, nt),
        in_specs=in_specs + [pl.BlockSpec((1, 6, d), lambda i, j: (i, 0, 0))] + [_const_spec(a.shape) for a in consts],
        out_specs=pl.BlockSpec((1, tm, d), lambda i, j: (i, j, 0)),
        out_shape=jax.ShapeDtypeStruct(x.shape, F32),
        scratch_shapes=scratch,
        compiler_params=_cparams("parallel", "parallel"),
        name="conv_ffn_mix" if mixer_out is not None else "conv_ffn",
    )(*args, mod, *consts)


def _rope_tables(length):
    nf = MLA_ROPE // 4
    t = jnp.arange(length)
    pos = jnp.stack([(t // GRID_W).astype(F32), (t % GRID_W).astype(F32)], axis=1)
    inv = ROPE_THETA ** (-jnp.arange(nf, dtype=F32) / nf)
    ang = pos[:, :, None] * inv
    cos = jnp.cos(ang)[:, :, None, :] * jnp.ones((1, 1, 2, 1), F32)
    sin = jnp.sin(ang)[:, :, None, :] * jnp.array([-1.0, 1.0], F32)[None, None, :, None]
    pad = lambda a: jnp.pad(a.reshape(length, MLA_ROPE), ((0, 0), (0, ROPE_PAD - MLA_ROPE)))
    return pad(cos), pad(sin)


def _rope_partner(a):
    nf = MLA_ROPE // 4
    r = a.reshape(a.shape[:-1] + (2, 2, nf))
    return jnp.flip(r, axis=-2).reshape(a.shape)


def _pad_last(a, width):
    return jnp.pad(a, [(0, 0)] * (a.ndim - 1) + [(0, width - a.shape[-1])])


def _mla_weights(w_down, q_lora_norm, kv_lora_norm, w_uq, w_ukv, q_norm, k_norm):
    q_rank = q_lora_norm.shape[0]
    kv_rank = kv_lora_norm.shape[0]
    qk = MLA_NOPE + MLA_ROPE
    w_pe = w_down[:, q_rank + kv_rank:]
    wuq = w_uq.reshape(q_rank, MLA_HEADS, qk)
    wuq_rope = wuq[..., MLA_NOPE:]
    wukv = w_ukv.reshape(kv_rank, MLA_HEADS, MLA_NOPE + MLA_V)

    def max_sq_norm(g):
        return MLA_NOPE * jnp.max(g[:MLA_NOPE] ** 2) + MLA_ROPE * jnp.max(g[MLA_NOPE:] ** 2)

    bound = jnp.sqrt(max_sq_norm(q_norm) * max_sq_norm(k_norm)) * (qk ** -0.5 * LOG2_E * 1.02)

    def gains(g):
        rope = g[MLA_NOPE:]
        return jnp.stack([g[:MLA_NOPE], _pad_last(rope, ROPE_PAD), _pad_last(_rope_partner(rope), ROPE_PAD)])

    return {
        "bounded": bound <= ATTN_MAX_SCORE,
        "wdq": w_down[:, :q_rank].astype(BF16),
        "wdkv": w_down[:, q_rank:q_rank + kv_rank].astype(BF16),
        "wdpe": jnp.concatenate([_pad_last(w_pe, ROPE_PAD), _pad_last(_rope_partner(w_pe), ROPE_PAD)],
                                axis=1).astype(BF16),
        "qln": q_lora_norm[None, :],
        "kvln": kv_lora_norm[None, :],
        "wuq": _pad_last(wuq, MLA_QPAD).reshape(q_rank, MLA_HEADS * MLA_QPAD).astype(BF16),
        "wuqs": _pad_last(_rope_partner(wuq_rope), ROPE_PAD).reshape(q_rank, MLA_HEADS * ROPE_PAD).astype(BF16),
        "wuk": wukv[..., :MLA_NOPE].reshape(kv_rank, MLA_HEADS * MLA_NOPE).astype(BF16),
        "wuvt": wukv[..., MLA_NOPE:].reshape(kv_rank, MLA_HEADS * MLA_V).T.astype(BF16),
        "qn": gains(q_norm),
        "kn": gains(k_norm),
    }


def kernel(x, c, ctx, c_ctx, w_ada, b_ada, norm_mix, norm_ffn, gla_w_in, gla_gate_w1, gla_gate_w2, gla_gate_b, gla_out_norm, gla_w_out, mla_w_down, mla_q_lora_norm, mla_kv_lora_norm, mla_w_uq, mla_w_ukv, mla_q_norm, mla_k_norm, mla_w_out, ffn_w_up, ffn_conv_w, ffn_conv_b, ffn_w_down):
    bsz, seq, d = x.shape
    lc = ctx.shape[1]
    depth = w_ada.shape[0]
    d_ff = ffn_w_down.shape[1]
    dqk = gla_gate_w2.shape[-1]
    rank = gla_gate_w1.shape[-1]
    dv = gla_out_norm.shape[-1]

    cond_rows = -(-(bsz + 1) // 8) * 8
    cond = jnp.zeros((cond_rows, d), F32).at[:bsz].set(c).at[bsz].set(c_ctx)
    mods = _ada_all(cond, w_ada, b_ada)

    cos_l, sin_l = _rope_tables(seq)
    cos_c = _pad_last(jnp.ones((lc, MLA_ROPE), F32), ROPE_PAD)
    sin_c = jnp.zeros((lc, ROPE_PAD), F32)

    tm_l, tm_c = min(seq, ROW_TILE), min(lc, ROW_TILE)
    tp_l, tp_c = min(seq, PROJ_TILE), min(lc, PROJ_TILE)
    xc = ctx
    for i in range(depth):
        last = i == depth - 1
        j = i // 2
        mod_l = mods[i, :bsz].reshape(bsz, 6, d)
        mod_c = jnp.broadcast_to(mods[i, bsz].reshape(1, 6, d), (bsz, 6, d))
        g_mix = norm_mix[i][None, :]
        mix_l = mix_c = None
        if i % 2 == 0:
            w_in = gla_w_in[j].astype(BF16)
            w1 = _pad_last(jnp.concatenate([gla_gate_w1[j, 0], gla_gate_w1[j, 1]], axis=1),
                           GLA_RANK_PAD).astype(BF16)
            w2 = jnp.zeros((GLA_RANK_PAD, 2 * dqk), F32)
            w2 = w2.at[:rank, :dqk].set(gla_gate_w2[j, 0]).at[rank:2 * rank, dqk:].set(gla_gate_w2[j, 1])
            w2 = w2.astype(BF16)
            gb = gla_gate_b[j].reshape(1, 2 * dqk)
            gn = gla_out_norm[j][None, :]
            w_out = gla_w_out[j].astype(BF16)
            s0 = jnp.zeros((bsz, GLA_HEADS, dv, dqk // GLA_HEADS), F32)

            vc, rc, *dirs_c = _gla_proj(xc, mod_c, g_mix, w_in, w1, w2, gb, tm=tp_c)
            v, r, *dirs_l = _gla_proj(x, mod_l, g_mix, w_in, w1, w2, gb, tm=tp_l)
            fwd_c, bwd_c, fwd_l, bwd_l = dirs_c[:2], dirs_c[2:], dirs_l[:2], dirs_l[2:]
            if last:
                _, s_fwd = _gla_scan(*fwd_c, vc, s0, direction=0, want_o=False)
                _, s_bwd = _gla_scan(*bwd_c, vc, s0, direction=1, want_o=False)
            else:
                oc_b, s_bwd = _gla_scan(*bwd_c, vc, s0, direction=1)
                xc, s_fwd = _gla_scan(*fwd_c, vc, s0, direction=0, fused=(oc_b, rc, xc, mod_c, gn, w_out))
            o_b, _ = _gla_scan(*bwd_l, v, s_bwd, direction=1)
            x, _ = _gla_scan(*fwd_l, v, s_fwd, direction=0, fused=(o_b, r, x, mod_l, gn, w_out))
        else:
            w = _mla_weights(mla_w_down[j], mla_q_lora_norm[j], mla_kv_lora_norm[j], mla_w_uq[j],
                             mla_w_ukv[j], mla_q_norm[j], mla_k_norm[j])
            w_out = mla_w_out[j].astype(BF16)
            if last:
                kc, vtc = _mla_proj(xc, mod_c, g_mix, cos_c, sin_c, w, want_q=False, tm=tp_c)
            else:
                qc, kc, vtc = _mla_proj(xc, mod_c, g_mix, cos_c, sin_c, w, want_q=True, tm=tp_c)
            q, k, vt = _mla_proj(x, mod_l, g_mix, cos_l, sin_l, w, want_q=True, tm=tp_l)
            mix_l = (_latent_attention(w["bounded"], q, kc, vtc, k, vt), w_out)
            if not last:
                mix_c = (_attention(qc, kc, vtc, tq=min(lc, ATTN_EXACT_TILE), heads=MLA_HEADS), w_out)

        g_ffn = norm_ffn[i][None, :]
        wa = ffn_w_up[i, :, :d_ff].astype(BF16)
        wg = ffn_w_up[i, :, d_ff:].astype(BF16)
        conv = jnp.concatenate([ffn_conv_w[i], ffn_conv_b[i][None, :]], axis=0)
        ca, cg = conv[:, :d_ff], conv[:, d_ff:]
        wd = ffn_w_down[i].astype(BF16)
        x = _conv_ffn(x, mod_l, g_ffn, wa, wg, ca, cg, wd, tm=tm_l, fc=FFN_COLS, mixer_out=mix_l)
        if not last:
            xc = _conv_ffn(xc, mod_c, g_ffn, wa, wg, ca, cg, wd, tm=tm_c, fc=FFN_COLS, mixer_out=mix_c)
    return x
```

```python
import functools

import jax
import jax.numpy as jnp
from jax import lax
from jax.experimental import pallas as pl
from jax.experimental.pallas import tpu as pltpu

F32 = jnp.float32
BF16 = jnp.bfloat16

LANES = 128
SUBLANES = 8
BF16_ROWS = 16
EPS = 1e-6
GRID_W = 64
ROPE_THETA = 10000.0
LOG2_E = 1.4426950408889634

GLA_HEADS = 4
GLA_GATE_NORMALIZER = 16.0
GLA_CHUNK = 128
GLA_RANK_PAD = 128

MLA_HEADS = 8
MLA_NOPE = 128
MLA_ROPE = 64
MLA_V = 128
MLA_QPAD = 256
ROPE_PAD = MLA_QPAD - MLA_NOPE
MLA_PROJ_ROWS = 256
ATTN_MAX_SCORE = 50.0

ROW_TILE = 512
MLA_PROJ_TILE = 1024
SCAN_BLOCK = 1024
ATTN_Q_STEP = 4096
ATTN_Q_TILE = 2048
ATTN_K_CHUNK = 2048
ATTN_EXACT_TILE = 512
FFN_COLS = 256
CONV_HALO = SUBLANES
VMEM_LIMIT = 56 * 1024 * 1024


def _cparams(*sem):
    return pltpu.CompilerParams(dimension_semantics=sem, vmem_limit_bytes=VMEM_LIMIT)


def _const_spec(shape):
    nd = len(shape)
    return pl.BlockSpec(shape, lambda *_: (0,) * nd, pipeline_mode=pl.Buffered(1))


def _dot(a, b):
    return jnp.dot(a, b, preferred_element_type=F32)


def _dot_nt(a, b):
    return lax.dot_general(a, b, (((1,), (1,)), ((), ())), preferred_element_type=F32)


def _dot_tn(a, b):
    return lax.dot_general(a, b, (((0,), (0,)), ((), ())), preferred_element_type=F32)


def _sigmoid(x):
    return 1.0 / (1.0 + jnp.exp(-x))


def _rms_scale(x, width):
    ss = jnp.sum(x * x, axis=-1, keepdims=True)
    return lax.rsqrt(ss * (1.0 / width) + EPS)


def _modulate(x, g, shift, scale):
    return x * _rms_scale(x, x.shape[-1]) * (g * (1.0 + scale)) + shift


def _ada_kernel(cond_ref, w_ref, b_ref, o_ref):
    cond = cond_ref[...]
    s = cond * _sigmoid(cond)
    o_ref[0] = jnp.dot(s, w_ref[0], preferred_element_type=F32,
                       precision=lax.Precision.HIGHEST) + b_ref[0]


def _ada_all(cond, w_ada, b_ada):
    depth, d, d6 = w_ada.shape
    rows = cond.shape[0]
    cols = 2 * d
    return pl.pallas_call(
        _ada_kernel,
        grid=(depth, d6 // cols),
        in_specs=[
            pl.BlockSpec((rows, d), lambda i, j: (0, 0)),
            pl.BlockSpec((1, d, cols), lambda i, j: (i, 0, j)),
            pl.BlockSpec((1, 1, cols), lambda i, j: (i, 0, j)),
        ],
        out_specs=pl.BlockSpec((1, rows, cols), lambda i, j: (i, 0, j)),
        out_shape=jax.ShapeDtypeStruct((depth, rows, d6), F32),
        compiler_params=_cparams("parallel", "parallel"),
        name="ada_mod",
    )(cond, w_ada, b_ada.reshape(depth, 1, d6))


def _gla_proj_kernel(x_ref, mod_ref, g_ref, win_ref, w1_ref, w2_ref, gb_ref,
                     v_ref, r_ref, qkf_ref, ff_ref, qkb_ref, fb_ref, *, dqk):
    mod = mod_ref[0]
    h = _modulate(x_ref[0], g_ref[...], mod[0:1], mod[1:2]).astype(BF16)
    c = GLA_CHUNK
    dk = dqk // GLA_HEADS
    dv = v_ref.shape[2]
    low = _dot(h, w1_ref[...]).astype(BF16)
    z = _dot(low, w2_ref[...]) + gb_ref[...]
    pqk = _dot(h, win_ref[:, :2 * dqk])
    unit = LOG2_E / GLA_GATE_NORMALIZER
    gates = jnp.minimum(z, 0.0) * unit - jnp.log(1.0 + jnp.exp(-jnp.abs(z))) * unit
    row = lax.broadcasted_iota(jnp.int32, (c, c), 0)
    col = lax.broadcasted_iota(jnp.int32, (c, c), 1)
    for direction, (qk_ref, f_ref) in enumerate(((qkf_ref, ff_ref), (qkb_ref, fb_ref))):
        reverse = direction == 1
        tri = ((col >= row) if reverse else (col <= row)).astype(BF16)
        tri2 = jnp.concatenate([tri, tri], axis=1)
        i_mid = c // 2 if reverse else c // 2 - 1
        i_end = 0 if reverse else c - 1
        g = gates[:, direction * dqk:(direction + 1) * dqk]
        g_hi = g.astype(BF16)
        g_lo = (g - g_hi.astype(F32)).astype(BF16)
        for ci in range(x_ref.shape[1] // c):
            rows = slice(ci * c, (ci + 1) * c)
            cum = _dot(tri2, jnp.concatenate([g_hi[rows], g_lo[rows]], axis=0))
            c_mid = cum[i_mid:i_mid + 1, :]
            c_end = cum[i_end:i_end + 1, :]
            rel = cum - c_mid
            qk_ref[0, rows, :dqk] = (pqk[rows, :dqk] * (jnp.exp2(rel) * (dk ** -0.5))).astype(BF16)
            qk_ref[0, rows, dqk:] = (pqk[rows, dqk:] * jnp.exp2(-rel)).astype(BF16)
            factors = jnp.concatenate([jnp.exp2(c_mid), jnp.exp2(c_end - c_mid), jnp.exp2(c_end)], axis=1)
            f_ref[0, ci * SUBLANES:(ci + 1) * SUBLANES, :] = jnp.broadcast_to(factors, (SUBLANES, 3 * dqk))
    pvr = _dot(h, win_ref[:, 2 * dqk:])
    v_ref[0] = pvr[:, :dv].astype(BF16)
    r_ref[0] = pvr[:, dv:].astype(BF16)


def _gla_proj(x, mod, norm_g, w_in, w1, w2, gb, *, tm):
    b, l, d = x.shape
    dqk = w2.shape[1] // 2
    dv = (w_in.shape[1] - 2 * dqk) // 2
    drows = tm // GLA_CHUNK * SUBLANES
    row = lambda i, j: (i, j, 0)
    tok = lambda width, dtype: (pl.BlockSpec((1, tm, width), row), jax.ShapeDtypeStruct((b, l, width), dtype))
    fac = (pl.BlockSpec((1, drows, 3 * dqk), row),
           jax.ShapeDtypeStruct((b, l // GLA_CHUNK * SUBLANES, 3 * dqk), F32))
    outs = [tok(dv, BF16), tok(dv, BF16)] + 2 * [tok(2 * dqk, BF16), fac]
    return pl.pallas_call(
        functools.partial(_gla_proj_kernel, dqk=dqk),
        grid=(b, l // tm),
        in_specs=[
            pl.BlockSpec((1, tm, d), row),
            pl.BlockSpec((1, 6, d), lambda i, j: (i, 0, 0)),
            _const_spec(norm_g.shape),
            _const_spec(w_in.shape),
            _const_spec(w1.shape),
            _const_spec(w2.shape),
            _const_spec(gb.shape),
        ],
        out_specs=[o[0] for o in outs],
        out_shape=[o[1] for o in outs],
        compiler_params=_cparams("parallel", "parallel"),
        name="gla_proj",
    )(x, mod, norm_g, w_in, w1, w2, gb)


def _gla_scan_kernel(*refs, reverse, want_o, fuse_out, tblk):
    if fuse_out:
        (qk_ref, v_ref, f_ref, s0_ref, ob_ref, r_ref, x_ref, mod_ref, gn_ref, wout_ref,
         xo_ref, st_ref, y_scr) = refs
    elif want_o:
        qk_ref, v_ref, f_ref, s0_ref, o_ref, st_ref = refs
    else:
        qk_ref, v_ref, f_ref, s0_ref, st_ref = refs

    @pl.when(pl.program_id(1) == 0)
    def _():
        st_ref[...] = s0_ref[...]

    c = GLA_CHUNK
    dqk = qk_ref.shape[2] // 2
    dk = dqk // GLA_HEADS
    dv = v_ref.shape[2] // GLA_HEADS
    row = lax.broadcasted_iota(jnp.int32, (c, c), 0)
    col = lax.broadcasted_iota(jnp.int32, (c, c), 1)
    mask = (col >= row) if reverse else (col <= row)

    nch = tblk // c
    states = [st_ref[0, h] for h in range(GLA_HEADS)]
    for ci in (reversed(range(nch)) if reverse else range(nch)):
        rows = slice(ci * c, (ci + 1) * c)
        frow = slice(ci * SUBLANES, ci * SUBLANES + 1)
        for h in range(GLA_HEADS):
            qs = slice(h * dk, (h + 1) * dk)
            ks = slice(dqk + h * dk, dqk + (h + 1) * dk)
            vs = slice(h * dv, (h + 1) * dv)
            v = v_ref[0, rows, vs]
            st = states[h]
            k_in = qk_ref[0, rows, ks]
            k_st = (k_in.astype(F32) * f_ref[0, frow, ks]).astype(BF16)
            if want_o:
                q_in = qk_ref[0, rows, qs]
                q_st = (q_in.astype(F32) * f_ref[0, frow, qs]).astype(BF16)
                scores = jnp.where(mask, _dot_nt(q_in, k_in), 0.0)
                o = _dot(scores.astype(BF16), v) + _dot_nt(q_st, st.astype(BF16))
                if fuse_out:
                    o = o + ob_ref[0, rows, vs].astype(F32)
                    r = r_ref[0, rows, vs].astype(F32)
                    y = o * _rms_scale(o, dv) * gn_ref[...] * (r * _sigmoid(r))
                    y_scr[rows, vs] = y.astype(BF16)
                else:
                    o_ref[0, rows, vs] = o.astype(BF16)
            decay_end = f_ref[0, frow, 2 * dqk + h * dk:2 * dqk + (h + 1) * dk]
            states[h] = st * decay_end + _dot_tn(v, k_st)
    for h in range(GLA_HEADS):
        st_ref[0, h] = states[h]

    if fuse_out:
        gate = mod_ref[0][2:3]
        xo_ref[0] = x_ref[0] + gate * _dot(y_scr[...], wout_ref[...])


def _gla_scan(qk, factors, v, s0, *, direction, want_o=True, fused=None):
    b, l, dq2 = qk.shape
    dvt = v.shape[2]
    tblk = min(l, SCAN_BLOCK)
    nblk = l // tblk
    reverse = direction == 1
    blk = (lambda i, j: (i, nblk - 1 - j, 0)) if reverse else (lambda i, j: (i, j, 0))
    st_spec = pl.BlockSpec((1,) + s0.shape[1:], lambda i, j: (i, 0, 0, 0))
    in_specs = [
        pl.BlockSpec((1, tblk, dq2), blk),
        pl.BlockSpec((1, tblk, dvt), blk),
        pl.BlockSpec((1, tblk // GLA_CHUNK * SUBLANES, factors.shape[2]), blk),
        st_spec,
    ]
    args = [qk, v, factors, s0]
    st_shape = jax.ShapeDtypeStruct(s0.shape, F32)
    scratch = []
    if fused is not None:
        o_other, r, x, mod, gn, w_out = fused
        d = x.shape[2]
        in_specs += [
            pl.BlockSpec((1, tblk, dvt), blk),
            pl.BlockSpec((1, tblk, dvt), blk),
            pl.BlockSpec((1, tblk, d), blk),
            pl.BlockSpec((1, 6, d), lambda i, j: (i, 0, 0)),
            _const_spec(gn.shape),
            _const_spec(w_out.shape),
        ]
        args += [o_other, r, x, mod, gn, w_out]
        out_specs = [pl.BlockSpec((1, tblk, d), blk), st_spec]
        out_shape = [jax.ShapeDtypeStruct(x.shape, F32), st_shape]
        scratch = [pltpu.VMEM((tblk, dvt), BF16)]
    elif want_o:
        out_specs = [pl.BlockSpec((1, tblk, dvt), blk), st_spec]
        out_shape = [jax.ShapeDtypeStruct((b, l, dvt), BF16), st_shape]
    else:
        out_specs = [st_spec]
        out_shape = [st_shape]
    outs = pl.pallas_call(
        functools.partial(_gla_scan_kernel, reverse=reverse, want_o=want_o,
                          fuse_out=fused is not None, tblk=tblk),
        grid=(b, nblk),
        in_specs=in_specs,
        out_specs=out_specs,
        out_shape=out_shape,
        scratch_shapes=scratch,
        compiler_params=_cparams("parallel", "arbitrary"),
        name="gla_scan_%s%s" % ("bwd" if reverse else "fwd", "_out" if fused is not None else ""),
    )(*args)
    if want_o:
        return outs[0], outs[1]
    return None, outs[0]


def _mla_proj_kernel(*refs, want_q):
    (x_ref, mod_ref, g_ref, cos_ref, sin_ref, wdq_ref, wdkv_ref, wdpe_ref, qln_ref, kvln_ref,
     wuq_ref, wuqs_ref, wuk_ref, wuvt_ref, qn_ref, kn_ref) = refs[:16]
    if want_q:
        q_ref, k_ref, vt_ref = refs[16:]
    else:
        k_ref, vt_ref = refs[16:]
    mod = mod_ref[0]
    kn = kn_ref[...]
    qn = qn_ref[...] * ((MLA_NOPE + MLA_ROPE) ** -0.5 * LOG2_E)
    sub = min(x_ref.shape[1], MLA_PROJ_ROWS)
    parts = [slice(i * sub, (i + 1) * sub) for i in range(x_ref.shape[1] // sub)]

    def rope_tables(gains, rows):
        return cos_ref[rows, :] * gains[1:2], sin_ref[rows, :] * gains[2:3]

    def rope_part(raw, raw_partner, tables):
        return (raw * tables[0] + raw_partner * tables[1]) * _rms_scale(raw, MLA_ROPE)

    def normed(c, gain_ref):
        return (c * _rms_scale(c, c.shape[-1]) * gain_ref[...]).astype(BF16)

    hs = [_modulate(x_ref[0, r, :], g_ref[...], mod[0:1], mod[1:2]).astype(BF16) for r in parts]
    c_kv = [_dot(h, wdkv_ref[...]) for h in hs]
    kpe = [_dot(h, wdpe_ref[...]) for h in hs]
    if want_q:
        c_q = [_dot(h, wdq_ref[...]) for h in hs]
    c_kv = [normed(c, kvln_ref) for c in c_kv]
    for c, r in zip(c_kv, parts):
        vt_ref[0, :, r] = _dot_nt(wuvt_ref[...], c).astype(BF16)
    kn_all = [_dot(c, wuk_ref[...]) for c in c_kv]
    k_rope = [rope_part(p[:, :ROPE_PAD], p[:, ROPE_PAD:], rope_tables(kn, r)).astype(BF16) for p, r in zip(kpe, parts)]
    if want_q:
        c_q = [normed(c, qln_ref) for c in c_q]
        qm = [_dot(c, wuq_ref[...]) for c in c_q]
        qs = [_dot(c, wuqs_ref[...]) for c in c_q]
    for i, r in enumerate(parts):
        if want_q:
            q_tables = rope_tables(qn, r)
        for hd in range(MLA_HEADS):
            k_nope = kn_all[i][:, hd * MLA_NOPE:(hd + 1) * MLA_NOPE]
            k_ref[0, hd, r, :MLA_NOPE] = (k_nope * _rms_scale(k_nope, MLA_NOPE) * kn[0:1]).astype(BF16)
            k_ref[0, hd, r, MLA_NOPE:] = k_rope[i]
            if want_q:
                base = hd * MLA_QPAD
                q_nope = qm[i][:, base:base + MLA_NOPE]
                q_ref[0, hd, r, :MLA_NOPE] = (q_nope * _rms_scale(q_nope, MLA_NOPE) * qn[0:1]).astype(BF16)
                q_rope = rope_part(qm[i][:, base + MLA_NOPE:base + MLA_QPAD],
                                   qs[i][:, hd * ROPE_PAD:(hd + 1) * ROPE_PAD], q_tables)
                q_ref[0, hd, r, MLA_NOPE:] = q_rope.astype(BF16)


def _mla_proj(x, mod, norm_g, cos, sin, w, *, want_q, tm):
    b, l, d = x.shape
    consts = [w["wdq"], w["wdkv"], w["wdpe"], w["qln"], w["kvln"], w["wuq"], w["wuqs"], w["wuk"],
              w["wuvt"], w["qn"], w["kn"]]
    hspec = pl.BlockSpec((1, MLA_HEADS, tm, MLA_QPAD), lambda i, j: (i, 0, j, 0))
    hshape = jax.ShapeDtypeStruct((b, MLA_HEADS, l, MLA_QPAD), BF16)
    out_specs = [hspec, pl.BlockSpec((1, MLA_HEADS * MLA_V, tm), lambda i, j: (i, 0, j))]
    out_shape = [hshape, jax.ShapeDtypeStruct((b, MLA_HEADS * MLA_V, l), BF16)]
    if want_q:
        out_specs = [hspec] + out_specs
        out_shape = [hshape] + out_shape
    return pl.pallas_call(
        functools.partial(_mla_proj_kernel, want_q=want_q),
        grid=(b, l // tm),
        in_specs=[
            pl.BlockSpec((1, tm, d), lambda i, j: (i, j, 0)),
            pl.BlockSpec((1, 6, d), lambda i, j: (i, 0, 0)),
            _const_spec(norm_g.shape),
            pl.BlockSpec((tm, ROPE_PAD), lambda i, j: (j, 0)),
            pl.BlockSpec((tm, ROPE_PAD), lambda i, j: (j, 0)),
        ] + [_const_spec(a.shape) for a in consts],
        out_specs=out_specs,
        out_shape=out_shape,
        compiler_params=_cparams("parallel", "parallel"),
        name="mla_proj_q" if want_q else "mla_proj_kv",
    )(x, mod, norm_g, cos, sin, *consts)


def _attn_kernel(*refs, with_latent, tk, heads):
    if with_latent:
        q_ref, kc_ref, vtc_ref, k_ref, vt_ref, o_ref = refs
    else:
        q_ref, kc_ref, vtc_ref, o_ref = refs
    dv = vtc_ref.shape[1] // heads
    for hd in range(heads):
        q = q_ref[0, hd]
        vrows = slice(hd * dv, (hd + 1) * dv)
        chunks = [(kc_ref, vtc_ref, 0, kc_ref.shape[2])]
        if with_latent:
            chunks += [(k_ref, vt_ref, j * tk, tk) for j in range(k_ref.shape[2] // tk)]

        def scores(idx):
            kr, _, start, size = chunks[idx]
            return _dot_nt(kr[0, hd, start:start + size, :], q)

        s_next = scores(0)
        m = denom = acc = None
        for idx, (_, vr, start, size) in enumerate(chunks):
            s = s_next
            if idx + 1 < len(chunks):
                s_next = scores(idx + 1)
            vt = vr[0, vrows, start:start + size]
            cmax = jnp.max(s, axis=0, keepdims=True)
            if m is None:
                m = cmax
                p = jnp.exp2(s - m)
                denom = jnp.sum(p, axis=0, keepdims=True)
                acc = _dot(vt, p.astype(BF16))
            else:
                m_new = jnp.maximum(m, cmax)
                alpha = jnp.exp2(m - m_new)
                p = jnp.exp2(s - m_new)
                denom = denom * alpha + jnp.sum(p, axis=0, keepdims=True)
                acc = acc * alpha + _dot(vt, p.astype(BF16))
                m = m_new
        o_ref[0, :, vrows] = (acc * (1.0 / denom)).T.astype(BF16)


def _attn_bounded_kernel(q_ref, kc_ref, vtc_ref, k_ref, vt_ref, bound_ref, o_ref, *, tq, tk):
    bound = bound_ref[0]
    chunks = [(kc_ref, vtc_ref, 0, kc_ref.shape[2])]
    chunks += [(k_ref, vt_ref, j * tk, tk) for j in range(k_ref.shape[2] // tk)]
    for qi in range(q_ref.shape[2] // tq):
        cols = slice(qi * tq, (qi + 1) * tq)
        q = q_ref[0, 0, cols, :]

        def scores(idx):
            kr, _, start, size = chunks[idx]
            return _dot_nt(kr[0, 0, start:start + size, :], q)

        s_next = scores(0)
        acc = denom8 = None
        for idx, (_, vr, start, size) in enumerate(chunks):
            s = s_next
            if idx + 1 < len(chunks):
                s_next = scores(idx + 1)
            p = jnp.exp2(s - bound)
            part = jnp.sum(p.reshape(size // SUBLANES, SUBLANES, p.shape[1]), axis=0)
            y = _dot(vr[0, :, start:start + size], p.astype(BF16))
            acc = y if acc is None else acc + y
            denom8 = part if denom8 is None else denom8 + part
        denom = jnp.sum(denom8, axis=0, keepdims=True)
        o_ref[0, cols, :] = (acc * (1.0 / denom)).T.astype(BF16)


def _attention(q, kc, vtc, k=None, vt=None, *, tq, tk=ATTN_EXACT_TILE, bounded_tq=None, bound=None, heads=1):
    b, nh, lq, dq = q.shape
    lc = kc.shape[2]
    dv = vtc.shape[1] // nh
    with_latent = k is not None
    in_specs = [
        pl.BlockSpec((1, heads, tq, dq), lambda i, h, j: (i, h, j, 0)),
        pl.BlockSpec((1, heads, lc, dq), lambda i, h, j: (i, h, 0, 0)),
        pl.BlockSpec((1, heads * dv, lc), lambda i, h, j: (i, h, 0)),
    ]
    args = [q, kc, vtc]
    if with_latent:
        lk = k.shape[2]
        in_specs += [pl.BlockSpec((1, heads, lk, dq), lambda i, h, j: (i, h, 0, 0)),
                     pl.BlockSpec((1, heads * dv, lk), lambda i, h, j: (i, h, 0))]
        args += [k, vt]
    if bounded_tq is not None:
        assert heads == 1
        in_specs += [pl.BlockSpec(memory_space=pltpu.SMEM)]
        args += [bound.reshape(1)]
        body = functools.partial(_attn_bounded_kernel, tq=bounded_tq, tk=tk)
        name = "mla_attn_bounded"
    else:
        body = functools.partial(_attn_kernel, with_latent=with_latent, tk=tk, heads=heads)
        name = "mla_attn" if with_latent else "mla_attn_ctx"
    return pl.pallas_call(
        body,
        grid=(b, nh // heads, lq // tq),
        in_specs=in_specs,
        out_specs=pl.BlockSpec((1, tq, heads * dv), lambda i, h, j: (i, j, h)),
        out_shape=jax.ShapeDtypeStruct((b, lq, nh * dv), BF16),
        compiler_params=_cparams("parallel", "parallel", "parallel"),
        name=name,
    )(*args)


def _latent_attention(bound, q, kc, vtc, k, vt):
    lq = q.shape[2]
    lk = k.shape[2]
    fast = functools.partial(_attention, tq=min(lq, ATTN_Q_STEP), tk=min(lk, ATTN_K_CHUNK),
                             bounded_tq=min(lq, ATTN_Q_TILE), bound=bound)
    exact = functools.partial(_attention, tq=min(lq, ATTN_EXACT_TILE), tk=min(lk, ATTN_EXACT_TILE))
    return lax.cond(bound <= ATTN_MAX_SCORE, fast, exact, q, kc, vtc, k, vt)


def _ffn_kernel(*refs, tm, fc, mixer_out):
    if mixer_out:
        (xm_ref, xp_ref, xn_ref, om_ref, op_ref, on_ref, wo_ref, mod_ref, g_ref, wa_ref, wg_ref, ca_ref,
         cg_ref, wd_ref, xo_ref, h_scr, ua_scr, ug_scr, act_scr, o_scr) = refs
    else:
        (xm_ref, xp_ref, xn_ref, mod_ref, g_ref, wa_ref, wg_ref, ca_ref, cg_ref, wd_ref,
         xo_ref, h_scr, ua_scr, ug_scr, act_scr) = refs
    i = pl.program_id(1)
    mod = mod_ref[0]
    g = g_ref[...]
    hal = CONV_HALO

    def hidden(x):
        return _modulate(x, g, mod[3:4], mod[4:5])

    x_main, x_nxt, x_prv = xm_ref[0], xn_ref[0], xp_ref[0]
    if mixer_out:
        o_scr[0:tm] = om_ref[0]
        o_nxt = on_ref[0].astype(F32)[:hal]
        o_prv = op_ref[0].astype(F32)[BF16_ROWS - hal:]
        o_scr[tm:] = jnp.concatenate([o_nxt, o_prv], axis=0).astype(BF16)
        y = _dot(o_scr[...], wo_ref[...])
        x_main = x_main + mod[2:3] * y[0:tm]
        x_nxt = x_nxt + mod[2:3] * y[tm:tm + hal]
        x_prv = x_prv + mod[2:3] * y[tm + hal:]

    h_scr[0:tm] = hidden(x_main).astype(BF16)
    nxt = jnp.where(i == pl.num_programs(1) - 1, 0.0, hidden(x_nxt))
    prv = jnp.where(i == 0, 0.0, hidden(x_prv))
    h_scr[tm:] = jnp.concatenate([nxt, prv], axis=0).astype(BF16)

    def put(u_scr, slab, u):
        u_scr[slab, hal:hal + tm, :] = u[0:tm]
        u_scr[slab, hal + tm:, :] = u[tm:tm + hal]
        u_scr[slab, 0:hal, :] = u[tm + hal:]

    def conv(u_scr, slab, cw):
        return (u_scr[slab, hal - 1:hal - 1 + tm, :] * cw[0:1] + u_scr[slab, hal:hal + tm, :] * cw[1:2]
                + u_scr[slab, hal + 1:hal + 1 + tm, :] * cw[2:3] + cw[3:4])

    per = fc // LANES
    for ci in range(wa_ref.shape[1] // fc):
        cols = slice(ci * fc, (ci + 1) * fc)
        ua = _dot(h_scr[...], wa_ref[:, cols])
        ug = _dot(h_scr[...], wg_ref[:, cols])
        for s in range(per):
            put(ua_scr, ci * per + s, ua[:, s * LANES:(s + 1) * LANES])
            put(ug_scr, ci * per + s, ug[:, s * LANES:(s + 1) * LANES])
        for s in range(per):
            slab = ci * per + s
            lanes = slice(slab * LANES, (slab + 1) * LANES)
            a = conv(ua_scr, slab, ca_ref[:, lanes])
            gt = conv(ug_scr, slab, cg_ref[:, lanes])
            act_scr[:, lanes] = (gt * _sigmoid(gt) * a).astype(BF16)
    xo_ref[0] = x_main + mod[5:6] * _dot(act_scr[...], wd_ref[...])


def _conv_ffn(x, mod, norm_g, wa, wg, ca, cg, wd, *, tm, fc, mixer_out=None):
    b, l, d = x.shape
    hal = CONV_HALO
    nt = l // tm
    d_ff = wa.shape[1]

    def halo_specs(rows, width):
        per, last = tm // rows, l // rows - 1
        return [pl.BlockSpec((1, rows, width), lambda i, j: (i, jnp.maximum(j * per - 1, 0), 0)),
                pl.BlockSpec((1, rows, width), lambda i, j: (i, jnp.minimum((j + 1) * per, last), 0))]

    in_specs = [pl.BlockSpec((1, tm, d), lambda i, j: (i, j, 0))] + halo_specs(hal, d)
    args = [x, x, x]
    scratch = [
        pltpu.VMEM((tm + 2 * hal, d), BF16),
        pltpu.VMEM((d_ff // LANES, tm + 2 * hal, LANES), F32),
        pltpu.VMEM((d_ff // LANES, tm + 2 * hal, LANES), F32),
        pltpu.VMEM((tm, d_ff), BF16),
    ]
    if mixer_out is not None:
        o, w_out = mixer_out
        do = o.shape[2]
        in_specs += [pl.BlockSpec((1, tm, do), lambda i, j: (i, j, 0))] + halo_specs(BF16_ROWS, do)
        in_specs += [_const_spec(w_out.shape)]
        args += [o, o, o, w_out]
        scratch += [pltpu.VMEM((tm + 2 * hal, do), BF16)]
    consts = [norm_g, wa, wg, ca, cg, wd]
    return pl.pallas_call(
        functools.partial(_ffn_kernel, tm=tm, fc=fc, mixer_out=mixer_out is not None),
        grid=(b, nt),
        in_specs=in_specs + [pl.BlockSpec((1, 6, d), lambda i, j: (i, 0, 0))] + [_const_spec(a.shape) for a in consts],
        out_specs=pl.BlockSpec((1, tm, d), lambda i, j: (i, j, 0)),
        out_shape=jax.ShapeDtypeStruct(x.shape, F32),
        scratch_shapes=scratch,
        compiler_params=_cparams("parallel", "parallel"),
        name="conv_ffn_mix" if mixer_out is not None else "conv_ffn",
    )(*args, mod, *consts)


def _rope_tables(length):
    nf = MLA_ROPE // 4
    t = jnp.arange(length)
    pos = jnp.stack([(t // GRID_W).astype(F32), (t % GRID_W).astype(F32)], axis=1)
    inv = ROPE_THETA ** (-jnp.arange(nf, dtype=F32) / nf)
    ang = pos[:, :, None] * inv
    cos = jnp.cos(ang)[:, :, None, :] * jnp.ones((1, 1, 2, 1), F32)
    sin = jnp.sin(ang)[:, :, None, :] * jnp.array([-1.0, 1.0], F32)[None, None, :, None]
    pad = lambda a: jnp.pad(a.reshape(length, MLA_ROPE), ((0, 0), (0, ROPE_PAD - MLA_ROPE)))
    return pad(cos), pad(sin)


def _rope_partner(a):
    nf = MLA_ROPE // 4
    r = a.reshape(a.shape[:-1] + (2, 2, nf))
    return jnp.flip(r, axis=-2).reshape(a.shape)


def _pad_last(a, width):
    return jnp.pad(a, [(0, 0)] * (a.ndim - 1) + [(0, width - a.shape[-1])])


def _mla_weights(w_down, q_lora_norm, kv_lora_norm, w_uq, w_ukv, q_norm, k_norm):
    q_rank = q_lora_norm.shape[0]
    kv_rank = kv_lora_norm.shape[0]
    qk = MLA_NOPE + MLA_ROPE
    w_pe = w_down[:, q_rank + kv_rank:]
    wuq = w_uq.reshape(q_rank, MLA_HEADS, qk)
    wuq_rope = wuq[..., MLA_NOPE:]
    wukv = w_ukv.reshape(kv_rank, MLA_HEADS, MLA_NOPE + MLA_V)

    def max_sq_norm(g):
        return MLA_NOPE * jnp.max(g[:MLA_NOPE] ** 2) + MLA_ROPE * jnp.max(g[MLA_NOPE:] ** 2)

    bound = jnp.sqrt(max_sq_norm(q_norm) * max_sq_norm(k_norm)) * (qk ** -0.5 * LOG2_E * 1.02)

    def gains(g):
        rope = g[MLA_NOPE:]
        return jnp.stack([g[:MLA_NOPE], _pad_last(rope, ROPE_PAD), _pad_last(_rope_partner(rope), ROPE_PAD)])

    return {
        "bound": bound,
        "wdq": w_down[:, :q_rank].astype(BF16),
        "wdkv": w_down[:, q_rank:q_rank + kv_rank].astype(BF16),
        "wdpe": jnp.concatenate([_pad_last(w_pe, ROPE_PAD), _pad_last(_rope_partner(w_pe), ROPE_PAD)],
                                axis=1).astype(BF16),
        "qln": q_lora_norm[None, :],
        "kvln": kv_lora_norm[None, :],
        "wuq": _pad_last(wuq, MLA_QPAD).reshape(q_rank, MLA_HEADS * MLA_QPAD).astype(BF16),
        "wuqs": _pad_last(_rope_partner(wuq_rope), ROPE_PAD).reshape(q_rank, MLA_HEADS * ROPE_PAD).astype(BF16),
        "wuk": wukv[..., :MLA_NOPE].reshape(kv_rank, MLA_HEADS * MLA_NOPE).astype(BF16),
        "wuvt": wukv[..., MLA_NOPE:].reshape(kv_rank, MLA_HEADS * MLA_V).T.astype(BF16),
        "qn": gains(q_norm),
        "kn": gains(k_norm),
    }


def kernel(x, c, ctx, c_ctx, w_ada, b_ada, norm_mix, norm_ffn, gla_w_in, gla_gate_w1, gla_gate_w2, gla_gate_b, gla_out_norm, gla_w_out, mla_w_down, mla_q_lora_norm, mla_kv_lora_norm, mla_w_uq, mla_w_ukv, mla_q_norm, mla_k_norm, mla_w_out, ffn_w_up, ffn_conv_w, ffn_conv_b, ffn_w_down):
    bsz, seq, d = x.shape
    lc = ctx.shape[1]
    depth = w_ada.shape[0]
    d_ff = ffn_w_down.shape[1]
    dqk = gla_gate_w2.shape[-1]
    rank = gla_gate_w1.shape[-1]
    dv = gla_out_norm.shape[-1]

    cond_rows = -(-(bsz + 1) // 8) * 8
    cond = jnp.zeros((cond_rows, d), F32).at[:bsz].set(c).at[bsz].set(c_ctx)
    mods = _ada_all(cond, w_ada, b_ada)

    cos_l, sin_l = _rope_tables(seq)
    cos_c = _pad_last(jnp.ones((lc, MLA_ROPE), F32), ROPE_PAD)
    sin_c = jnp.zeros((lc, ROPE_PAD), F32)

    tm_l = min(seq, ROW_TILE)
    tm_c = min(lc, ROW_TILE)
    xc = ctx
    for i in range(depth):
        last = i == depth - 1
        j = i // 2
        mod_l = mods[i, :bsz].reshape(bsz, 6, d)
        mod_c = jnp.broadcast_to(mods[i, bsz].reshape(1, 6, d), (bsz, 6, d))
        g_mix = norm_mix[i][None, :]
        mix_l = mix_c = None
        if i % 2 == 0:
            w_in = gla_w_in[j].astype(BF16)
            w1 = _pad_last(jnp.concatenate([gla_gate_w1[j, 0], gla_gate_w1[j, 1]], axis=1),
                           GLA_RANK_PAD).astype(BF16)
            w2 = jnp.zeros((GLA_RANK_PAD, 2 * dqk), F32)
            w2 = w2.at[:rank, :dqk].set(gla_gate_w2[j, 0]).at[rank:2 * rank, dqk:].set(gla_gate_w2[j, 1])
            w2 = w2.astype(BF16)
            gb = gla_gate_b[j].reshape(1, 2 * dqk)
            gn = gla_out_norm[j][None, :]
            w_out = gla_w_out[j].astype(BF16)
            s0 = jnp.zeros((bsz, GLA_HEADS, dv, dqk // GLA_HEADS), F32)

            vc, rc, *dirs_c = _gla_proj(xc, mod_c, g_mix, w_in, w1, w2, gb, tm=tm_c)
            v, r, *dirs_l = _gla_proj(x, mod_l, g_mix, w_in, w1, w2, gb, tm=tm_l)
            fwd_c, bwd_c, fwd_l, bwd_l = dirs_c[:2], dirs_c[2:], dirs_l[:2], dirs_l[2:]
            if last:
                _, s_fwd = _gla_scan(*fwd_c, vc, s0, direction=0, want_o=False)
                _, s_bwd = _gla_scan(*bwd_c, vc, s0, direction=1, want_o=False)
            else:
                oc_b, s_bwd = _gla_scan(*bwd_c, vc, s0, direction=1)
                xc, s_fwd = _gla_scan(*fwd_c, vc, s0, direction=0, fused=(oc_b, rc, xc, mod_c, gn, w_out))
            o_b, _ = _gla_scan(*bwd_l, v, s_bwd, direction=1)
            x, _ = _gla_scan(*fwd_l, v, s_fwd, direction=0, fused=(o_b, r, x, mod_l, gn, w_out))
        else:
            w = _mla_weights(mla_w_down[j], mla_q_lora_norm[j], mla_kv_lora_norm[j], mla_w_uq[j],
                             mla_w_ukv[j], mla_q_norm[j], mla_k_norm[j])
            w_out = mla_w_out[j].astype(BF16)
            if last:
                kc, vtc = _mla_proj(xc, mod_c, g_mix, cos_c, sin_c, w, want_q=False, tm=tm_c)
            else:
                qc, kc, vtc = _mla_proj(xc, mod_c, g_mix, cos_c, sin_c, w, want_q=True, tm=tm_c)
            q, k, vt = _mla_proj(x, mod_l, g_mix, cos_l, sin_l, w, want_q=True, tm=min(seq, MLA_PROJ_TILE))
            mix_l = (_latent_attention(w["bound"], q, kc, vtc, k, vt), w_out)
            if not last:
                mix_c = (_attention(qc, kc, vtc, tq=min(lc, ATTN_EXACT_TILE), heads=MLA_HEADS), w_out)

        g_ffn = norm_ffn[i][None, :]
        wa = ffn_w_up[i, :, :d_ff].astype(BF16)
        wg = ffn_w_up[i, :, d_ff:].astype(BF16)
        conv = jnp.concatenate([ffn_conv_w[i], ffn_conv_b[i][None, :]], axis=0)
        ca, cg = conv[:, :d_ff], conv[:, d_ff:]
        wd = ffn_w_down[i].astype(BF16)
        x = _conv_ffn(x, mod_l, g_ffn, wa, wg, ca, cg, wd, tm=tm_l, fc=FFN_COLS, mixer_out=mix_l)
        if not last:
            xc = _conv_ffn(xc, mod_c, g_ffn, wa, wg, ca, cg, wd, tm=tm_c, fc=FFN_COLS, mixer_out=mix_c)
    return x
```

```python
import functools

import jax
import jax.numpy as jnp
from jax import lax
from jax.experimental import pallas as pl
from jax.experimental.pallas import tpu as pltpu

F32 = jnp.float32
BF16 = jnp.bfloat16

LANES = 128
SUBLANES = 8
BF16_ROWS = 16
EPS = 1e-6
GRID_W = 64
ROPE_THETA = 10000.0
LOG2_E = 1.4426950408889634

GLA_HEADS = 4
GLA_GATE_NORMALIZER = 16.0
GLA_CHUNK = 128
GLA_RANK_PAD = 128

MLA_HEADS = 8
MLA_NOPE = 128
MLA_ROPE = 64
MLA_V = 128
MLA_QPAD = 256
ROPE_PAD = MLA_QPAD - MLA_NOPE
MLA_PROJ_ROWS = 256
ATTN_MAX_SCORE = 50.0

ROW_TILE = 512
MLA_PROJ_TILE = 1024
SCAN_BLOCK = 1024
ATTN_Q_STEP = 4096
ATTN_Q_TILE = 2048
ATTN_K_CHUNK = 2048
ATTN_EXACT_TILE = 512
FFN_COLS = 256
CONV_HALO = SUBLANES
VMEM_LIMIT = 56 * 1024 * 1024


def _cparams(*sem):
    return pltpu.CompilerParams(dimension_semantics=sem, vmem_limit_bytes=VMEM_LIMIT)


def _const_spec(shape):
    nd = len(shape)
    return pl.BlockSpec(shape, lambda *_: (0,) * nd, pipeline_mode=pl.Buffered(1))


def _dot(a, b):
    return jnp.dot(a, b, preferred_element_type=F32)


def _dot_nt(a, b):
    return lax.dot_general(a, b, (((1,), (1,)), ((), ())), preferred_element_type=F32)


def _dot_tn(a, b):
    return lax.dot_general(a, b, (((0,), (0,)), ((), ())), preferred_element_type=F32)


def _sigmoid(x):
    return 1.0 / (1.0 + jnp.exp(-x))


def _rms_scale(x, width):
    ss = jnp.sum(x * x, axis=-1, keepdims=True)
    return lax.rsqrt(ss * (1.0 / width) + EPS)


def _modulate(x, g, shift, scale):
    return x * _rms_scale(x, x.shape[-1]) * (g * (1.0 + scale)) + shift


def _ada_kernel(cond_ref, w_ref, b_ref, o_ref):
    cond = cond_ref[...]
    s = cond * _sigmoid(cond)
    o_ref[0] = jnp.dot(s, w_ref[0], preferred_element_type=F32,
                       precision=lax.Precision.HIGHEST) + b_ref[0]


def _ada_all(cond, w_ada, b_ada):
    depth, d, d6 = w_ada.shape
    rows = cond.shape[0]
    cols = 2 * d
    return pl.pallas_call(
        _ada_kernel,
        grid=(depth, d6 // cols),
        in_specs=[
            pl.BlockSpec((rows, d), lambda i, j: (0, 0)),
            pl.BlockSpec((1, d, cols), lambda i, j: (i, 0, j)),
            pl.BlockSpec((1, 1, cols), lambda i, j: (i, 0, j)),
        ],
        out_specs=pl.BlockSpec((1, rows, cols), lambda i, j: (i, 0, j)),
        out_shape=jax.ShapeDtypeStruct((depth, rows, d6), F32),
        compiler_params=_cparams("parallel", "parallel"),
        name="ada_mod",
    )(cond, w_ada, b_ada.reshape(depth, 1, d6))


def _gla_proj_kernel(x_ref, mod_ref, g_ref, win_ref, w1_ref, w2_ref, gb_ref,
                     v_ref, r_ref, qkf_ref, ff_ref, qkb_ref, fb_ref, *, dqk):
    mod = mod_ref[0]
    h = _modulate(x_ref[0], g_ref[...], mod[0:1], mod[1:2]).astype(BF16)
    c = GLA_CHUNK
    dk = dqk // GLA_HEADS
    dv = v_ref.shape[2]
    low = _dot(h, w1_ref[...]).astype(BF16)
    z = _dot(low, w2_ref[...]) + gb_ref[...]
    pqk = _dot(h, win_ref[:, :2 * dqk])
    unit = LOG2_E / GLA_GATE_NORMALIZER
    gates = jnp.minimum(z, 0.0) * unit - jnp.log(1.0 + jnp.exp(-jnp.abs(z))) * unit
    row = lax.broadcasted_iota(jnp.int32, (c, c), 0)
    col = lax.broadcasted_iota(jnp.int32, (c, c), 1)
    for direction, (qk_ref, f_ref) in enumerate(((qkf_ref, ff_ref), (qkb_ref, fb_ref))):
        reverse = direction == 1
        tri = ((col >= row) if reverse else (col <= row)).astype(BF16)
        tri2 = jnp.concatenate([tri, tri], axis=1)
        i_mid = c // 2 if reverse else c // 2 - 1
        i_end = 0 if reverse else c - 1
        g = gates[:, direction * dqk:(direction + 1) * dqk]
        g_hi = g.astype(BF16)
        g_lo = (g - g_hi.astype(F32)).astype(BF16)
        for ci in range(x_ref.shape[1] // c):
            rows = slice(ci * c, (ci + 1) * c)
            cum = _dot(tri2, jnp.concatenate([g_hi[rows], g_lo[rows]], axis=0))
            c_mid = cum[i_mid:i_mid + 1, :]
            c_end = cum[i_end:i_end + 1, :]
            rel = cum - c_mid
            qk_ref[0, rows, :dqk] = (pqk[rows, :dqk] * (jnp.exp2(rel) * (dk ** -0.5))).astype(BF16)
            qk_ref[0, rows, dqk:] = (pqk[rows, dqk:] * jnp.exp2(-rel)).astype(BF16)
            factors = jnp.concatenate([jnp.exp2(c_mid), jnp.exp2(c_end - c_mid), jnp.exp2(c_end)], axis=1)
            f_ref[0, ci * SUBLANES:(ci + 1) * SUBLANES, :] = jnp.broadcast_to(factors, (SUBLANES, 3 * dqk))
    pvr = _dot(h, win_ref[:, 2 * dqk:])
    v_ref[0] = pvr[:, :dv].astype(BF16)
    r_ref[0] = pvr[:, dv:].astype(BF16)


def _gla_proj(x, mod, norm_g, w_in, w1, w2, gb, *, tm):
    b, l, d = x.shape
    dqk = w2.shape[1] // 2
    dv = (w_in.shape[1] - 2 * dqk) // 2
    drows = tm // GLA_CHUNK * SUBLANES
    row = lambda i, j: (i, j, 0)
    tok = lambda width, dtype: (pl.BlockSpec((1, tm, width), row), jax.ShapeDtypeStruct((b, l, width), dtype))
    fac = (pl.BlockSpec((1, drows, 3 * dqk), row),
           jax.ShapeDtypeStruct((b, l // GLA_CHUNK * SUBLANES, 3 * dqk), F32))
    outs = [tok(dv, BF16), tok(dv, BF16)] + 2 * [tok(2 * dqk, BF16), fac]
    return pl.pallas_call(
        functools.partial(_gla_proj_kernel, dqk=dqk),
        grid=(b, l // tm),
        in_specs=[
            pl.BlockSpec((1, tm, d), row),
            pl.BlockSpec((1, 6, d), lambda i, j: (i, 0, 0)),
            _const_spec(norm_g.shape),
            _const_spec(w_in.shape),
            _const_spec(w1.shape),
            _const_spec(w2.shape),
            _const_spec(gb.shape),
        ],
        out_specs=[o[0] for o in outs],
        out_shape=[o[1] for o in outs],
        compiler_params=_cparams("parallel", "parallel"),
        name="gla_proj",
    )(x, mod, norm_g, w_in, w1, w2, gb)


def _gla_scan_kernel(*refs, reverse, want_o, fuse_out, tblk):
    if fuse_out:
        qk_ref, v_ref, f_ref, s0_ref, ob_ref, r_ref, gn_ref, y_ref, st_ref = refs
    elif want_o:
        qk_ref, v_ref, f_ref, s0_ref, o_ref, st_ref = refs
    else:
        qk_ref, v_ref, f_ref, s0_ref, st_ref = refs

    @pl.when(pl.program_id(1) == 0)
    def _():
        st_ref[...] = s0_ref[...]

    c = GLA_CHUNK
    dqk = qk_ref.shape[2] // 2
    dk = dqk // GLA_HEADS
    dv = v_ref.shape[2] // GLA_HEADS
    row = lax.broadcasted_iota(jnp.int32, (c, c), 0)
    col = lax.broadcasted_iota(jnp.int32, (c, c), 1)
    mask = (col >= row) if reverse else (col <= row)

    nch = tblk // c
    states = [st_ref[0, h] for h in range(GLA_HEADS)]
    for ci in (reversed(range(nch)) if reverse else range(nch)):
        rows = slice(ci * c, (ci + 1) * c)
        frow = slice(ci * SUBLANES, ci * SUBLANES + 1)
        for h in range(GLA_HEADS):
            qs = slice(h * dk, (h + 1) * dk)
            ks = slice(dqk + h * dk, dqk + (h + 1) * dk)
            vs = slice(h * dv, (h + 1) * dv)
            v = v_ref[0, rows, vs]
            st = states[h]
            k_in = qk_ref[0, rows, ks]
            k_st = (k_in.astype(F32) * f_ref[0, frow, ks]).astype(BF16)
            if want_o:
                q_in = qk_ref[0, rows, qs]
                q_st = (q_in.astype(F32) * f_ref[0, frow, qs]).astype(BF16)
                scores = jnp.where(mask, _dot_nt(q_in, k_in), 0.0)
                o = _dot(scores.astype(BF16), v) + _dot_nt(q_st, st.astype(BF16))
                if fuse_out:
                    o = o + ob_ref[0, rows, vs].astype(F32)
                    r = r_ref[0, rows, vs].astype(F32)
                    y = o * _rms_scale(o, dv) * gn_ref[...] * (r * _sigmoid(r))
                    y_ref[0, rows, vs] = y.astype(BF16)
                else:
                    o_ref[0, rows, vs] = o.astype(BF16)
            decay_end = f_ref[0, frow, 2 * dqk + h * dk:2 * dqk + (h + 1) * dk]
            states[h] = st * decay_end + _dot_tn(v, k_st)
    for h in range(GLA_HEADS):
        st_ref[0, h] = states[h]


def _gla_scan(qk, factors, v, s0, *, direction, want_o=True, fused=None):
    b, l, dq2 = qk.shape
    dvt = v.shape[2]
    tblk = min(l, SCAN_BLOCK)
    nblk = l // tblk
    reverse = direction == 1
    blk = (lambda i, j: (i, nblk - 1 - j, 0)) if reverse else (lambda i, j: (i, j, 0))
    st_spec = pl.BlockSpec((1,) + s0.shape[1:], lambda i, j: (i, 0, 0, 0))
    in_specs = [
        pl.BlockSpec((1, tblk, dq2), blk),
        pl.BlockSpec((1, tblk, dvt), blk),
        pl.BlockSpec((1, tblk // GLA_CHUNK * SUBLANES, factors.shape[2]), blk),
        st_spec,
    ]
    args = [qk, v, factors, s0]
    st_shape = jax.ShapeDtypeStruct(s0.shape, F32)
    if fused is not None:
        o_other, r, gn = fused
        in_specs += [
            pl.BlockSpec((1, tblk, dvt), blk),
            pl.BlockSpec((1, tblk, dvt), blk),
            _const_spec(gn.shape),
        ]
        args += [o_other, r, gn]
    if want_o:
        out_specs = [pl.BlockSpec((1, tblk, dvt), blk), st_spec]
        out_shape = [jax.ShapeDtypeStruct((b, l, dvt), BF16), st_shape]
    else:
        out_specs = [st_spec]
        out_shape = [st_shape]
    outs = pl.pallas_call(
        functools.partial(_gla_scan_kernel, reverse=reverse, want_o=want_o,
                          fuse_out=fused is not None, tblk=tblk),
        grid=(b, nblk),
        in_specs=in_specs,
        out_specs=out_specs,
        out_shape=out_shape,
        compiler_params=_cparams("parallel", "arbitrary"),
        name="gla_scan_%s%s" % ("bwd" if reverse else "fwd", "_out" if fused is not None else ""),
    )(*args)
    if want_o:
        return outs[0], outs[1]
    return None, outs[0]


def _mla_proj_kernel(*refs, want_q):
    (x_ref, mod_ref, g_ref, cos_ref, sin_ref, wdq_ref, wdkv_ref, wdpe_ref, qln_ref, kvln_ref,
     wuq_ref, wuqs_ref, wuk_ref, wuvt_ref, qn_ref, kn_ref) = refs[:16]
    if want_q:
        q_ref, k_ref, vt_ref = refs[16:]
    else:
        k_ref, vt_ref = refs[16:]
    mod = mod_ref[0]
    kn = kn_ref[...]
    qn = qn_ref[...] * ((MLA_NOPE + MLA_ROPE) ** -0.5 * LOG2_E)
    sub = min(x_ref.shape[1], MLA_PROJ_ROWS)
    parts = [slice(i * sub, (i + 1) * sub) for i in range(x_ref.shape[1] // sub)]

    def rope_tables(gains, rows):
        return cos_ref[rows, :] * gains[1:2], sin_ref[rows, :] * gains[2:3]

    def rope_part(raw, raw_partner, tables):
        return (raw * tables[0] + raw_partner * tables[1]) * _rms_scale(raw, MLA_ROPE)

    def normed(c, gain_ref):
        return (c * _rms_scale(c, c.shape[-1]) * gain_ref[...]).astype(BF16)

    hs = [_modulate(x_ref[0, r, :], g_ref[...], mod[0:1], mod[1:2]).astype(BF16) for r in parts]
    c_kv = [_dot(h, wdkv_ref[...]) for h in hs]
    kpe = [_dot(h, wdpe_ref[...]) for h in hs]
    if want_q:
        c_q = [_dot(h, wdq_ref[...]) for h in hs]
    c_kv = [normed(c, kvln_ref) for c in c_kv]
    for c, r in zip(c_kv, parts):
        vt_ref[0, :, r] = _dot_nt(wuvt_ref[...], c).astype(BF16)
    kn_all = [_dot(c, wuk_ref[...]) for c in c_kv]
    k_rope = [rope_part(p[:, :ROPE_PAD], p[:, ROPE_PAD:], rope_tables(kn, r)).astype(BF16) for p, r in zip(kpe, parts)]
    if want_q:
        c_q = [normed(c, qln_ref) for c in c_q]
        qm = [_dot(c, wuq_ref[...]) for c in c_q]
        qs = [_dot(c, wuqs_ref[...]) for c in c_q]
    for i, r in enumerate(parts):
        if want_q:
            q_tables = rope_tables(qn, r)
        for hd in range(MLA_HEADS):
            k_nope = kn_all[i][:, hd * MLA_NOPE:(hd + 1) * MLA_NOPE]
            k_ref[0, hd, r, :MLA_NOPE] = (k_nope * _rms_scale(k_nope, MLA_NOPE) * kn[0:1]).astype(BF16)
            k_ref[0, hd, r, MLA_NOPE:] = k_rope[i]
            if want_q:
                base = hd * MLA_QPAD
                q_nope = qm[i][:, base:base + MLA_NOPE]
                q_ref[0, hd, r, :MLA_NOPE] = (q_nope * _rms_scale(q_nope, MLA_NOPE) * qn[0:1]).astype(BF16)
                q_rope = rope_part(qm[i][:, base + MLA_NOPE:base + MLA_QPAD],
                                   qs[i][:, hd * ROPE_PAD:(hd + 1) * ROPE_PAD], q_tables)
                q_ref[0, hd, r, MLA_NOPE:] = q_rope.astype(BF16)


def _mla_proj(x, mod, norm_g, cos, sin, w, *, want_q, tm):
    b, l, d = x.shape
    consts = [w["wdq"], w["wdkv"], w["wdpe"], w["qln"], w["kvln"], w["wuq"], w["wuqs"], w["wuk"],
              w["wuvt"], w["qn"], w["kn"]]
    hspec = pl.BlockSpec((1, MLA_HEADS, tm, MLA_QPAD), lambda i, j: (i, 0, j, 0))
    hshape = jax.ShapeDtypeStruct((b, MLA_HEADS, l, MLA_QPAD), BF16)
    out_specs = [hspec, pl.BlockSpec((1, MLA_HEADS * MLA_V, tm), lambda i, j: (i, 0, j))]
    out_shape = [hshape, jax.ShapeDtypeStruct((b, MLA_HEADS * MLA_V, l), BF16)]
    if want_q:
        out_specs = [hspec] + out_specs
        out_shape = [hshape] + out_shape
    return pl.pallas_call(
        functools.partial(_mla_proj_kernel, want_q=want_q),
        grid=(b, l // tm),
        in_specs=[
            pl.BlockSpec((1, tm, d), lambda i, j: (i, j, 0)),
            pl.BlockSpec((1, 6, d), lambda i, j: (i, 0, 0)),
            _const_spec(norm_g.shape),
            pl.BlockSpec((tm, ROPE_PAD), lambda i, j: (j, 0)),
            pl.BlockSpec((tm, ROPE_PAD), lambda i, j: (j, 0)),
        ] + [_const_spec(a.shape) for a in consts],
        out_specs=out_specs,
        out_shape=out_shape,
        compiler_params=_cparams("parallel", "parallel"),
        name="mla_proj_q" if want_q else "mla_proj_kv",
    )(x, mod, norm_g, cos, sin, *consts)


def _attn_kernel(*refs, with_latent, tk, heads):
    if with_latent:
        q_ref, kc_ref, vtc_ref, k_ref, vt_ref, o_ref = refs
    else:
        q_ref, kc_ref, vtc_ref, o_ref = refs
    dv = vtc_ref.shape[1] // heads
    for hd in range(heads):
        q = q_ref[0, hd]
        vrows = slice(hd * dv, (hd + 1) * dv)
        chunks = [(kc_ref, vtc_ref, 0, kc_ref.shape[2])]
        if with_latent:
            chunks += [(k_ref, vt_ref, j * tk, tk) for j in range(k_ref.shape[2] // tk)]

        def scores(idx):
            kr, _, start, size = chunks[idx]
            return _dot_nt(kr[0, hd, start:start + size, :], q)

        s_next = scores(0)
        m = denom = acc = None
        for idx, (_, vr, start, size) in enumerate(chunks):
            s = s_next
            if idx + 1 < len(chunks):
                s_next = scores(idx + 1)
            vt = vr[0, vrows, start:start + size]
            cmax = jnp.max(s, axis=0, keepdims=True)
            if m is None:
                m = cmax
                p = jnp.exp2(s - m)
                denom = jnp.sum(p, axis=0, keepdims=True)
                acc = _dot(vt, p.astype(BF16))
            else:
                m_new = jnp.maximum(m, cmax)
                alpha = jnp.exp2(m - m_new)
                p = jnp.exp2(s - m_new)
                denom = denom * alpha + jnp.sum(p, axis=0, keepdims=True)
                acc = acc * alpha + _dot(vt, p.astype(BF16))
                m = m_new
        o_ref[0, :, vrows] = (acc * (1.0 / denom)).T.astype(BF16)


def _attn_bounded_kernel(q_ref, kc_ref, vtc_ref, k_ref, vt_ref, o_ref, *, tq, tk):
    chunks = [(kc_ref, vtc_ref, 0, kc_ref.shape[2])]
    chunks += [(k_ref, vt_ref, j * tk, tk) for j in range(k_ref.shape[2] // tk)]
    for qi in range(q_ref.shape[2] // tq):
        cols = slice(qi * tq, (qi + 1) * tq)
        q = q_ref[0, 0, cols, :]

        def scores(idx):
            kr, _, start, size = chunks[idx]
            return _dot_nt(kr[0, 0, start:start + size, :], q)

        s_next = scores(0)
        acc = denom8 = None
        for idx, (_, vr, start, size) in enumerate(chunks):
            s = s_next
            if idx + 1 < len(chunks):
                s_next = scores(idx + 1)
            p = jnp.exp2(s)
            part = jnp.sum(p.reshape(size // SUBLANES, SUBLANES, p.shape[1]), axis=0)
            y = _dot(vr[0, :, start:start + size], p.astype(BF16))
            acc = y if acc is None else acc + y
            denom8 = part if denom8 is None else denom8 + part
        denom = jnp.sum(denom8, axis=0, keepdims=True)
        o_ref[0, cols, :] = (acc * (1.0 / denom)).T.astype(BF16)


def _attention(q, kc, vtc, k=None, vt=None, *, tq, tk=ATTN_EXACT_TILE, bounded_tq=None, heads=1):
    b, nh, lq, dq = q.shape
    lc = kc.shape[2]
    dv = vtc.shape[1] // nh
    with_latent = k is not None
    in_specs = [
        pl.BlockSpec((1, heads, tq, dq), lambda i, h, j: (i, h, j, 0)),
        pl.BlockSpec((1, heads, lc, dq), lambda i, h, j: (i, h, 0, 0)),
        pl.BlockSpec((1, heads * dv, lc), lambda i, h, j: (i, h, 0)),
    ]
    args = [q, kc, vtc]
    if with_latent:
        lk = k.shape[2]
        in_specs += [pl.BlockSpec((1, heads, lk, dq), lambda i, h, j: (i, h, 0, 0)),
                     pl.BlockSpec((1, heads * dv, lk), lambda i, h, j: (i, h, 0))]
        args += [k, vt]
    if bounded_tq is not None:
        assert heads == 1
        body = functools.partial(_attn_bounded_kernel, tq=bounded_tq, tk=tk)
        name = "mla_attn_bounded"
    else:
        body = functools.partial(_attn_kernel, with_latent=with_latent, tk=tk, heads=heads)
        name = "mla_attn" if with_latent else "mla_attn_ctx"
    return pl.pallas_call(
        body,
        grid=(b, nh // heads, lq // tq),
        in_specs=in_specs,
        out_specs=pl.BlockSpec((1, tq, heads * dv), lambda i, h, j: (i, j, h)),
        out_shape=jax.ShapeDtypeStruct((b, lq, nh * dv), BF16),
        compiler_params=_cparams("parallel", "parallel", "parallel"),
        name=name,
    )(*args)


def _latent_attention(bounded, q, kc, vtc, k, vt):
    lq = q.shape[2]
    lk = k.shape[2]
    fast = functools.partial(_attention, tq=min(lq, ATTN_Q_STEP), tk=min(lk, ATTN_K_CHUNK),
                             bounded_tq=min(lq, ATTN_Q_TILE))
    exact = functools.partial(_attention, tq=min(lq, ATTN_EXACT_TILE), tk=min(lk, ATTN_EXACT_TILE))
    return lax.cond(bounded, fast, exact, q, kc, vtc, k, vt)


def _ffn_kernel(*refs, tm, fc, mixer_out):
    if mixer_out:
        (xm_ref, xp_ref, xn_ref, om_ref, op_ref, on_ref, wo_ref, mod_ref, g_ref, wa_ref, wg_ref, ca_ref,
         cg_ref, wd_ref, xo_ref, h_scr, ua_scr, ug_scr, act_scr, o_scr) = refs
    else:
        (xm_ref, xp_ref, xn_ref, mod_ref, g_ref, wa_ref, wg_ref, ca_ref, cg_ref, wd_ref,
         xo_ref, h_scr, ua_scr, ug_scr, act_scr) = refs
    i = pl.program_id(1)
    mod = mod_ref[0]
    g = g_ref[...]
    hal = CONV_HALO

    def hidden(x):
        return _modulate(x, g, mod[3:4], mod[4:5])

    x_main, x_nxt, x_prv = xm_ref[0], xn_ref[0], xp_ref[0]
    if mixer_out:
        o_scr[0:tm] = om_ref[0]
        o_nxt = on_ref[0].astype(F32)[:hal]
        o_prv = op_ref[0].astype(F32)[BF16_ROWS - hal:]
        o_scr[tm:] = jnp.concatenate([o_nxt, o_prv], axis=0).astype(BF16)
        y = _dot(o_scr[...], wo_ref[...])
        x_main = x_main + mod[2:3] * y[0:tm]
        x_nxt = x_nxt + mod[2:3] * y[tm:tm + hal]
        x_prv = x_prv + mod[2:3] * y[tm + hal:]

    h_scr[0:tm] = hidden(x_main).astype(BF16)
    nxt = jnp.where(i == pl.num_programs(1) - 1, 0.0, hidden(x_nxt))
    prv = jnp.where(i == 0, 0.0, hidden(x_prv))
    h_scr[tm:] = jnp.concatenate([nxt, prv], axis=0).astype(BF16)

    def put(u_scr, slab, u):
        u_scr[slab, hal:hal + tm, :] = u[0:tm]
        u_scr[slab, hal + tm:, :] = u[tm:tm + hal]
        u_scr[slab, 0:hal, :] = u[tm + hal:]

    def conv(u_scr, slab, cw):
        return (u_scr[slab, hal - 1:hal - 1 + tm, :] * cw[0:1] + u_scr[slab, hal:hal + tm, :] * cw[1:2]
                + u_scr[slab, hal + 1:hal + 1 + tm, :] * cw[2:3] + cw[3:4])

    per = fc // LANES
    for ci in range(wa_ref.shape[1] // fc):
        cols = slice(ci * fc, (ci + 1) * fc)
        ua = _dot(h_scr[...], wa_ref[:, cols])
        ug = _dot(h_scr[...], wg_ref[:, cols])
        for s in range(per):
            put(ua_scr, ci * per + s, ua[:, s * LANES:(s + 1) * LANES])
            put(ug_scr, ci * per + s, ug[:, s * LANES:(s + 1) * LANES])
        for s in range(per):
            slab = ci * per + s
            lanes = slice(slab * LANES, (slab + 1) * LANES)
            a = conv(ua_scr, slab, ca_ref[:, lanes])
            gt = conv(ug_scr, slab, cg_ref[:, lanes])
            act_scr[:, lanes] = (gt * _sigmoid(gt) * a).astype(BF16)
    xo_ref[0] = x_main + mod[5:6] * _dot(act_scr[...], wd_ref[...])


def _conv_ffn(x, mod, norm_g, wa, wg, ca, cg, wd, *, tm, fc, mixer_out=None):
    b, l, d = x.shape
    hal = CONV_HALO
    nt = l // tm
    d_ff = wa.shape[1]

    def halo_specs(rows, width):
        per, last = tm // rows, l // rows - 1
        return [pl.BlockSpec((1, rows, width), lambda i, j: (i, jnp.maximum(j * per - 1, 0), 0)),
                pl.BlockSpec((1, rows, width), lambda i, j: (i, jnp.minimum((j + 1) * per, last), 0))]

    in_specs = [pl.BlockSpec((1, tm, d), lambda i, j: (i, j, 0))] + halo_specs(hal, d)
    args = [x, x, x]
    scratch = [
        pltpu.VMEM((tm + 2 * hal, d), BF16),
        pltpu.VMEM((d_ff // LANES, tm + 2 * hal, LANES), F32),
        pltpu.VMEM((d_ff // LANES, tm + 2 * hal, LANES), F32),
        pltpu.VMEM((tm, d_ff), BF16),
    ]
    if mixer_out is not None:
        o, w_out = mixer_out
        do = o.shape[2]
        in_specs += [pl.BlockSpec((1, tm, do), lambda i, j: (i, j, 0))] + halo_specs(BF16_ROWS, do)
        in_specs += [_const_spec(w_out.shape)]
        args += [o, o, o, w_out]
        scratch += [pltpu.VMEM((tm + 2 * hal, do), BF16)]
    consts = [norm_g, wa, wg, ca, cg, wd]
    return pl.pallas_call(
        functools.partial(_ffn_kernel, tm=tm, fc=fc, mixer_out=mixer_out is not None),
        grid=(b, nt),
        in_specs=in_specs + [pl.BlockSpec((1, 6, d), lambda i, j: (i, 0, 0))] + [_const_spec(a.shape) for a in consts],
        out_specs=pl.BlockSpec((1, tm, d), lambda i, j: (i, j, 0)),
        out_shape=jax.ShapeDtypeStruct(x.shape, F32),
        scratch_shapes=scratch,
        compiler_params=_cparams("parallel", "parallel"),
        name="conv_ffn_mix" if mixer_out is not None else "conv_ffn",
    )(*args, mod, *consts)


def _rope_tables(length):
    nf = MLA_ROPE // 4
    t = jnp.arange(length)
    pos = jnp.stack([(t // GRID_W).astype(F32), (t % GRID_W).astype(F32)], axis=1)
    inv = ROPE_THETA ** (-jnp.arange(nf, dtype=F32) / nf)
    ang = pos[:, :, None] * inv
    cos = jnp.cos(ang)[:, :, None, :] * jnp.ones((1, 1, 2, 1), F32)
    sin = jnp.sin(ang)[:, :, None, :] * jnp.array([-1.0, 1.0], F32)[None, None, :, None]
    pad = lambda a: jnp.pad(a.reshape(length, MLA_ROPE), ((0, 0), (0, ROPE_PAD - MLA_ROPE)))
    return pad(cos), pad(sin)


def _rope_partner(a):
    nf = MLA_ROPE // 4
    r = a.reshape(a.shape[:-1] + (2, 2, nf))
    return jnp.flip(r, axis=-2).reshape(a.shape)


def _pad_last(a, width):
    return jnp.pad(a, [(0, 0)] * (a.ndim - 1) + [(0, width - a.shape[-1])])


def _mla_weights(w_down, q_lora_norm, kv_lora_norm, w_uq, w_ukv, q_norm, k_norm):
    q_rank = q_lora_norm.shape[0]
    kv_rank = kv_lora_norm.shape[0]
    qk = MLA_NOPE + MLA_ROPE
    w_pe = w_down[:, q_rank + kv_rank:]
    wuq = w_uq.reshape(q_rank, MLA_HEADS, qk)
    wuq_rope = wuq[..., MLA_NOPE:]
    wukv = w_ukv.reshape(kv_rank, MLA_HEADS, MLA_NOPE + MLA_V)

    def max_sq_norm(g):
        return MLA_NOPE * jnp.max(g[:MLA_NOPE] ** 2) + MLA_ROPE * jnp.max(g[MLA_NOPE:] ** 2)

    bound = jnp.sqrt(max_sq_norm(q_norm) * max_sq_norm(k_norm)) * (qk ** -0.5 * LOG2_E * 1.02)

    def gains(g):
        rope = g[MLA_NOPE:]
        return jnp.stack([g[:MLA_NOPE], _pad_last(rope, ROPE_PAD), _pad_last(_rope_partner(rope), ROPE_PAD)])

    return {
        "bounded": bound <= ATTN_MAX_SCORE,
        "wdq": w_down[:, :q_rank].astype(BF16),
        "wdkv": w_down[:, q_rank:q_rank + kv_rank].astype(BF16),
        "wdpe": jnp.concatenate([_pad_last(w_pe, ROPE_PAD), _pad_last(_rope_partner(w_pe), ROPE_PAD)],
                                axis=1).astype(BF16),
        "qln": q_lora_norm[None, :],
        "kvln": kv_lora_norm[None, :],
        "wuq": _pad_last(wuq, MLA_QPAD).reshape(q_rank, MLA_HEADS * MLA_QPAD).astype(BF16),
        "wuqs": _pad_last(_rope_partner(wuq_rope), ROPE_PAD).reshape(q_rank, MLA_HEADS * ROPE_PAD).astype(BF16),
        "wuk": wukv[..., :MLA_NOPE].reshape(kv_rank, MLA_HEADS * MLA_NOPE).astype(BF16),
        "wuvt": wukv[..., MLA_NOPE:].reshape(kv_rank, MLA_HEADS * MLA_V).T.astype(BF16),
        "qn": gains(q_norm),
        "kn": gains(k_norm),
    }


def kernel(x, c, ctx, c_ctx, w_ada, b_ada, norm_mix, norm_ffn, gla_w_in, gla_gate_w1, gla_gate_w2, gla_gate_b, gla_out_norm, gla_w_out, mla_w_down, mla_q_lora_norm, mla_kv_lora_norm, mla_w_uq, mla_w_ukv, mla_q_norm, mla_k_norm, mla_w_out, ffn_w_up, ffn_conv_w, ffn_conv_b, ffn_w_down):
    bsz, seq, d = x.shape
    lc = ctx.shape[1]
    depth = w_ada.shape[0]
    d_ff = ffn_w_down.shape[1]
    dqk = gla_gate_w2.shape[-1]
    rank = gla_gate_w1.shape[-1]
    dv = gla_out_norm.shape[-1]

    cond_rows = -(-(bsz + 1) // 8) * 8
    cond = jnp.zeros((cond_rows, d), F32).at[:bsz].set(c).at[bsz].set(c_ctx)
    mods = _ada_all(cond, w_ada, b_ada)

    cos_l, sin_l = _rope_tables(seq)
    cos_c = _pad_last(jnp.ones((lc, MLA_ROPE), F32), ROPE_PAD)
    sin_c = jnp.zeros((lc, ROPE_PAD), F32)

    tm_l = min(seq, ROW_TILE)
    tm_c = min(lc, ROW_TILE)
    xc = ctx
    for i in range(depth):
        last = i == depth - 1
        j = i // 2
        mod_l = mods[i, :bsz].reshape(bsz, 6, d)
        mod_c = jnp.broadcast_to(mods[i, bsz].reshape(1, 6, d), (bsz, 6, d))
        g_mix = norm_mix[i][None, :]
        mix_l = mix_c = None
        if i % 2 == 0:
            w_in = gla_w_in[j].astype(BF16)
            w1 = _pad_last(jnp.concatenate([gla_gate_w1[j, 0], gla_gate_w1[j, 1]], axis=1),
                           GLA_RANK_PAD).astype(BF16)
            w2 = jnp.zeros((GLA_RANK_PAD, 2 * dqk), F32)
            w2 = w2.at[:rank, :dqk].set(gla_gate_w2[j, 0]).at[rank:2 * rank, dqk:].set(gla_gate_w2[j, 1])
            w2 = w2.astype(BF16)
            gb = gla_gate_b[j].reshape(1, 2 * dqk)
            gn = gla_out_norm[j][None, :]
            w_out = gla_w_out[j].astype(BF16)
            s0 = jnp.zeros((bsz, GLA_HEADS, dv, dqk // GLA_HEADS), F32)

            vc, rc, *dirs_c = _gla_proj(xc, mod_c, g_mix, w_in, w1, w2, gb, tm=tm_c)
            v, r, *dirs_l = _gla_proj(x, mod_l, g_mix, w_in, w1, w2, gb, tm=tm_l)
            fwd_c, bwd_c, fwd_l, bwd_l = dirs_c[:2], dirs_c[2:], dirs_l[:2], dirs_l[2:]
            if last:
                _, s_fwd = _gla_scan(*fwd_c, vc, s0, direction=0, want_o=False)
                _, s_bwd = _gla_scan(*bwd_c, vc, s0, direction=1, want_o=False)
            else:
                oc_b, s_bwd = _gla_scan(*bwd_c, vc, s0, direction=1)
                yc, s_fwd = _gla_scan(*fwd_c, vc, s0, direction=0, fused=(oc_b, rc, gn))
                mix_c = (yc, w_out)
            o_b, _ = _gla_scan(*bwd_l, v, s_bwd, direction=1)
            y, _ = _gla_scan(*fwd_l, v, s_fwd, direction=0, fused=(o_b, r, gn))
            mix_l = (y, w_out)
        else:
            w = _mla_weights(mla_w_down[j], mla_q_lora_norm[j], mla_kv_lora_norm[j], mla_w_uq[j],
                             mla_w_ukv[j], mla_q_norm[j], mla_k_norm[j])
            w_out = mla_w_out[j].astype(BF16)
            if last:
                kc, vtc = _mla_proj(xc, mod_c, g_mix, cos_c, sin_c, w, want_q=False, tm=tm_c)
            else:
                qc, kc, vtc = _mla_proj(xc, mod_c, g_mix, cos_c, sin_c, w, want_q=True, tm=tm_c)
            q, k, vt = _mla_proj(x, mod_l, g_mix, cos_l, sin_l, w, want_q=True, tm=min(seq, MLA_PROJ_TILE))
            mix_l = (_latent_attention(w["bounded"], q, kc, vtc, k, vt), w_out)
            if not last:
                mix_c = (_attention(qc, kc, vtc, tq=min(lc, ATTN_EXACT_TILE), heads=MLA_HEADS), w_out)

        g_ffn = norm_ffn[i][None, :]
        wa = ffn_w_up[i, :, :d_ff].astype(BF16)
        wg = ffn_w_up[i, :, d_ff:].astype(BF16)
        conv = jnp.concatenate([ffn_conv_w[i], ffn_conv_b[i][None, :]], axis=0)
        ca, cg = conv[:, :d_ff], conv[:, d_ff:]
        wd = ffn_w_down[i].astype(BF16)
        x = _conv_ffn(x, mod_l, g_ffn, wa, wg, ca, cg, wd, tm=tm_l, fc=FFN_COLS, mixer_out=mix_l)
        if not last:
            xc = _conv_ffn(xc, mod_c, g_ffn, wa, wg, ca, cg, wd, tm=tm_c, fc=FFN_COLS, mixer_out=mix_c)
    return x
```

```python
import functools

import jax
import jax.numpy as jnp
from jax import lax
from jax.experimental import pallas as pl
from jax.experimental.pallas import tpu as pltpu

F32 = jnp.float32
BF16 = jnp.bfloat16

LANES = 128
SUBLANES = 8
BF16_ROWS = 16
EPS = 1e-6
GRID_W = 64
ROPE_THETA = 10000.0
LOG2_E = 1.4426950408889634

GLA_HEADS = 4
GLA_GATE_NORMALIZER = 16.0
GLA_CHUNK = 128
GLA_RANK_PAD = 128

MLA_HEADS = 8
MLA_NOPE = 128
MLA_ROPE = 64
MLA_V = 128
MLA_QPAD = 256
ROPE_PAD = MLA_QPAD - MLA_NOPE
MLA_PROJ_ROWS = 256
ATTN_MAX_SCORE = 50.0

ROW_TILE = 512
MLA_PROJ_TILE = 1024
SCAN_BLOCK = 1024
ATTN_Q_STEP = 4096
ATTN_Q_TILE = 2048
ATTN_K_CHUNK = 2048
ATTN_EXACT_TILE = 512
FFN_COLS = 256
CONV_HALO = SUBLANES
VMEM_LIMIT = 56 * 1024 * 1024


def _cparams(*sem, **extra):
    return pltpu.CompilerParams(dimension_semantics=sem, vmem_limit_bytes=VMEM_LIMIT, **extra)


def _const_spec(shape):
    nd = len(shape)
    return pl.BlockSpec(shape, lambda *_: (0,) * nd, pipeline_mode=pl.Buffered(1))


def _dot(a, b):
    return jnp.dot(a, b, preferred_element_type=F32)


def _dot_nt(a, b):
    return lax.dot_general(a, b, (((1,), (1,)), ((), ())), preferred_element_type=F32)


def _dot_tn(a, b):
    return lax.dot_general(a, b, (((0,), (0,)), ((), ())), preferred_element_type=F32)


def _sigmoid(x):
    return 1.0 / (1.0 + jnp.exp(-x))


def _rms_scale(x, width):
    ss = jnp.sum(x * x, axis=-1, keepdims=True)
    return lax.rsqrt(ss * (1.0 / width) + EPS)


def _modulate(x, g, shift, scale):
    return x * _rms_scale(x, x.shape[-1]) * (g * (1.0 + scale)) + shift


def _ada_kernel(cond_ref, w_ref, b_ref, o_ref):
    cond = cond_ref[...]
    s = cond * _sigmoid(cond)
    o_ref[0] = jnp.dot(s, w_ref[0], preferred_element_type=F32,
                       precision=lax.Precision.HIGHEST) + b_ref[0]


def _ada_all(cond, w_ada, b_ada):
    depth, d, d6 = w_ada.shape
    rows = cond.shape[0]
    cols = 2 * d
    return pl.pallas_call(
        _ada_kernel,
        grid=(depth, d6 // cols),
        in_specs=[
            pl.BlockSpec((rows, d), lambda i, j: (0, 0)),
            pl.BlockSpec((1, d, cols), lambda i, j: (i, 0, j)),
            pl.BlockSpec((1, 1, cols), lambda i, j: (i, 0, j)),
        ],
        out_specs=pl.BlockSpec((1, rows, cols), lambda i, j: (i, 0, j)),
        out_shape=jax.ShapeDtypeStruct((depth, rows, d6), F32),
        compiler_params=_cparams("parallel", "parallel"),
        name="ada_mod",
    )(cond, w_ada, b_ada.reshape(depth, 1, d6))


def _gla_proj_kernel(x_ref, mod_ref, g_ref, win_ref, w1_ref, w2_ref, gb_ref,
                     v_ref, r_ref, qkf_ref, ff_ref, qkb_ref, fb_ref, *, dqk):
    mod = mod_ref[0]
    h = _modulate(x_ref[0], g_ref[...], mod[0:1], mod[1:2]).astype(BF16)
    c = GLA_CHUNK
    dk = dqk // GLA_HEADS
    dv = v_ref.shape[2]
    low = _dot(h, w1_ref[...]).astype(BF16)
    z = _dot(low, w2_ref[...]) + gb_ref[...]
    pqk = _dot(h, win_ref[:, :2 * dqk])
    unit = LOG2_E / GLA_GATE_NORMALIZER
    gates = jnp.minimum(z, 0.0) * unit - jnp.log(1.0 + jnp.exp(-jnp.abs(z))) * unit
    row = lax.broadcasted_iota(jnp.int32, (c, c), 0)
    col = lax.broadcasted_iota(jnp.int32, (c, c), 1)
    for direction, (qk_ref, f_ref) in enumerate(((qkf_ref, ff_ref), (qkb_ref, fb_ref))):
        reverse = direction == 1
        tri = ((col >= row) if reverse else (col <= row)).astype(BF16)
        tri2 = jnp.concatenate([tri, tri], axis=1)
        i_mid = c // 2 if reverse else c // 2 - 1
        i_end = 0 if reverse else c - 1
        g = gates[:, direction * dqk:(direction + 1) * dqk]
        g_hi = g.astype(BF16)
        g_lo = (g - g_hi.astype(F32)).astype(BF16)
        for ci in range(x_ref.shape[1] // c):
            rows = slice(ci * c, (ci + 1) * c)
            cum = _dot(tri2, jnp.concatenate([g_hi[rows], g_lo[rows]], axis=0))
            c_mid = cum[i_mid:i_mid + 1, :]
            c_end = cum[i_end:i_end + 1, :]
            rel = cum - c_mid
            qk_ref[0, rows, :dqk] = (pqk[rows, :dqk] * (jnp.exp2(rel) * (dk ** -0.5))).astype(BF16)
            qk_ref[0, rows, dqk:] = (pqk[rows, dqk:] * jnp.exp2(-rel)).astype(BF16)
            factors = jnp.concatenate([jnp.exp2(c_mid), jnp.exp2(c_end - c_mid), jnp.exp2(c_end)], axis=1)
            f_ref[0, ci * SUBLANES:(ci + 1) * SUBLANES, :] = jnp.broadcast_to(factors, (SUBLANES, 3 * dqk))
    pvr = _dot(h, win_ref[:, 2 * dqk:])
    v_ref[0] = pvr[:, :dv].astype(BF16)
    r_ref[0] = pvr[:, dv:].astype(BF16)


def _gla_proj(x, mod, norm_g, w_in, w1, w2, gb, *, tm):
    b, l, d = x.shape
    dqk = w2.shape[1] // 2
    dv = (w_in.shape[1] - 2 * dqk) // 2
    drows = tm // GLA_CHUNK * SUBLANES
    row = lambda i, j: (i, j, 0)
    tok = lambda width, dtype: (pl.BlockSpec((1, tm, width), row), jax.ShapeDtypeStruct((b, l, width), dtype))
    fac = (pl.BlockSpec((1, drows, 3 * dqk), row),
           jax.ShapeDtypeStruct((b, l // GLA_CHUNK * SUBLANES, 3 * dqk), F32))
    outs = [tok(dv, BF16), tok(dv, BF16)] + 2 * [tok(2 * dqk, BF16), fac]
    return pl.pallas_call(
        functools.partial(_gla_proj_kernel, dqk=dqk),
        grid=(b, l // tm),
        in_specs=[
            pl.BlockSpec((1, tm, d), row),
            pl.BlockSpec((1, 6, d), lambda i, j: (i, 0, 0)),
            _const_spec(norm_g.shape),
            _const_spec(w_in.shape),
            _const_spec(w1.shape),
            _const_spec(w2.shape),
            _const_spec(gb.shape),
        ],
        out_specs=[o[0] for o in outs],
        out_shape=[o[1] for o in outs],
        compiler_params=_cparams("parallel", "parallel"),
        name="gla_proj",
    )(x, mod, norm_g, w_in, w1, w2, gb)


def _gla_scan_kernel(*refs, reverse, want_o, fuse_out, tblk):
    if fuse_out:
        (qk_ref, v_ref, f_ref, s0_ref, ob_ref, r_ref, x_ref, mod_ref, gn_ref, wout_ref,
         xo_ref, st_ref, y_scr) = refs
    elif want_o:
        qk_ref, v_ref, f_ref, s0_ref, o_ref, st_ref = refs
    else:
        qk_ref, v_ref, f_ref, s0_ref, st_ref = refs

    @pl.when(pl.program_id(1) == 0)
    def _():
        st_ref[...] = s0_ref[...]

    c = GLA_CHUNK
    dqk = qk_ref.shape[2] // 2
    dk = dqk // GLA_HEADS
    dv = v_ref.shape[2] // GLA_HEADS
    row = lax.broadcasted_iota(jnp.int32, (c, c), 0)
    col = lax.broadcasted_iota(jnp.int32, (c, c), 1)
    mask = (col >= row) if reverse else (col <= row)

    nch = tblk // c
    states = [st_ref[0, h] for h in range(GLA_HEADS)]
    for ci in (reversed(range(nch)) if reverse else range(nch)):
        rows = slice(ci * c, (ci + 1) * c)
        frow = slice(ci * SUBLANES, ci * SUBLANES + 1)
        for h in range(GLA_HEADS):
            qs = slice(h * dk, (h + 1) * dk)
            ks = slice(dqk + h * dk, dqk + (h + 1) * dk)
            vs = slice(h * dv, (h + 1) * dv)
            v = v_ref[0, rows, vs]
            st = states[h]
            k_in = qk_ref[0, rows, ks]
            k_st = (k_in.astype(F32) * f_ref[0, frow, ks]).astype(BF16)
            if want_o:
                q_in = qk_ref[0, rows, qs]
                q_st = (q_in.astype(F32) * f_ref[0, frow, qs]).astype(BF16)
                scores = jnp.where(mask, _dot_nt(q_in, k_in), 0.0)
                o = _dot(scores.astype(BF16), v) + _dot_nt(q_st, st.astype(BF16))
                if fuse_out:
                    o = o + ob_ref[0, rows, vs].astype(F32)
                    r = r_ref[0, rows, vs].astype(F32)
                    y = o * _rms_scale(o, dv) * gn_ref[...] * (r * _sigmoid(r))
                    y_scr[rows, vs] = y.astype(BF16)
                else:
                    o_ref[0, rows, vs] = o.astype(BF16)
            decay_end = f_ref[0, frow, 2 * dqk + h * dk:2 * dqk + (h + 1) * dk]
            states[h] = st * decay_end + _dot_tn(v, k_st)
    for h in range(GLA_HEADS):
        st_ref[0, h] = states[h]

    if fuse_out:
        gate = mod_ref[0][2:3]
        xo_ref[0] = x_ref[0] + gate * _dot(y_scr[...], wout_ref[...])


def _gla_scan(qk, factors, v, s0, *, direction, want_o=True, fused=None):
    b, l, dq2 = qk.shape
    dvt = v.shape[2]
    tblk = min(l, SCAN_BLOCK)
    nblk = l // tblk
    reverse = direction == 1
    blk = (lambda i, j: (i, nblk - 1 - j, 0)) if reverse else (lambda i, j: (i, j, 0))
    st_spec = pl.BlockSpec((1,) + s0.shape[1:], lambda i, j: (i, 0, 0, 0))
    in_specs = [
        pl.BlockSpec((1, tblk, dq2), blk),
        pl.BlockSpec((1, tblk, dvt), blk),
        pl.BlockSpec((1, tblk // GLA_CHUNK * SUBLANES, factors.shape[2]), blk),
        st_spec,
    ]
    args = [qk, v, factors, s0]
    st_shape = jax.ShapeDtypeStruct(s0.shape, F32)
    scratch = []
    if fused is not None:
        o_other, r, x, mod, gn, w_out = fused
        d = x.shape[2]
        in_specs += [
            pl.BlockSpec((1, tblk, dvt), blk),
            pl.BlockSpec((1, tblk, dvt), blk),
            pl.BlockSpec((1, tblk, d), blk),
            pl.BlockSpec((1, 6, d), lambda i, j: (i, 0, 0)),
            _const_spec(gn.shape),
            _const_spec(w_out.shape),
        ]
        args += [o_other, r, x, mod, gn, w_out]
        out_specs = [pl.BlockSpec((1, tblk, d), blk), st_spec]
        out_shape = [jax.ShapeDtypeStruct(x.shape, F32), st_shape]
        scratch = [pltpu.VMEM((tblk, dvt), BF16)]
    elif want_o:
        out_specs = [pl.BlockSpec((1, tblk, dvt), blk), st_spec]
        out_shape = [jax.ShapeDtypeStruct((b, l, dvt), BF16), st_shape]
    else:
        out_specs = [st_spec]
        out_shape = [st_shape]
    outs = pl.pallas_call(
        functools.partial(_gla_scan_kernel, reverse=reverse, want_o=want_o,
                          fuse_out=fused is not None, tblk=tblk),
        grid=(b, nblk),
        in_specs=in_specs,
        out_specs=out_specs,
        out_shape=out_shape,
        scratch_shapes=scratch,
        compiler_params=_cparams("parallel", "arbitrary"),
        name="gla_scan_%s%s" % ("bwd" if reverse else "fwd", "_out" if fused is not None else ""),
    )(*args)
    if want_o:
        return outs[0], outs[1]
    return None, outs[0]


def _mla_proj_kernel(*refs, want_q):
    (x_ref, mod_ref, g_ref, cos_ref, sin_ref, wdq_ref, wdkv_ref, wdpe_ref, qln_ref, kvln_ref,
     wuq_ref, wuqs_ref, wuk_ref, wuvt_ref, qn_ref, kn_ref) = refs[:16]
    if want_q:
        q_ref, k_ref, vt_ref = refs[16:]
    else:
        k_ref, vt_ref = refs[16:]
    mod = mod_ref[0]
    kn = kn_ref[...]
    qn = qn_ref[...] * ((MLA_NOPE + MLA_ROPE) ** -0.5 * LOG2_E)
    sub = min(x_ref.shape[1], MLA_PROJ_ROWS)
    parts = [slice(i * sub, (i + 1) * sub) for i in range(x_ref.shape[1] // sub)]

    def rope_tables(gains, rows):
        return cos_ref[rows, :] * gains[1:2], sin_ref[rows, :] * gains[2:3]

    def rope_part(raw, raw_partner, tables):
        return (raw * tables[0] + raw_partner * tables[1]) * _rms_scale(raw, MLA_ROPE)

    def normed(c, gain_ref):
        return (c * _rms_scale(c, c.shape[-1]) * gain_ref[...]).astype(BF16)

    hs = [_modulate(x_ref[0, r, :], g_ref[...], mod[0:1], mod[1:2]).astype(BF16) for r in parts]
    c_kv = [_dot(h, wdkv_ref[...]) for h in hs]
    kpe = [_dot(h, wdpe_ref[...]) for h in hs]
    if want_q:
        c_q = [_dot(h, wdq_ref[...]) for h in hs]
    c_kv = [normed(c, kvln_ref) for c in c_kv]
    for c, r in zip(c_kv, parts):
        vt_ref[0, :, r] = _dot_nt(wuvt_ref[...], c).astype(BF16)
    kn_all = [_dot(c, wuk_ref[...]) for c in c_kv]
    k_rope = [rope_part(p[:, :ROPE_PAD], p[:, ROPE_PAD:], rope_tables(kn, r)).astype(BF16) for p, r in zip(kpe, parts)]
    if want_q:
        c_q = [normed(c, qln_ref) for c in c_q]
        qm = [_dot(c, wuq_ref[...]) for c in c_q]
        qs = [_dot(c, wuqs_ref[...]) for c in c_q]
    for i, r in enumerate(parts):
        if want_q:
            q_tables = rope_tables(qn, r)
        for hd in range(MLA_HEADS):
            k_nope = kn_all[i][:, hd * MLA_NOPE:(hd + 1) * MLA_NOPE]
            k_ref[0, hd, r, :MLA_NOPE] = (k_nope * _rms_scale(k_nope, MLA_NOPE) * kn[0:1]).astype(BF16)
            k_ref[0, hd, r, MLA_NOPE:] = k_rope[i]
            if want_q:
                base = hd * MLA_QPAD
                q_nope = qm[i][:, base:base + MLA_NOPE]
                q_ref[0, hd, r, :MLA_NOPE] = (q_nope * _rms_scale(q_nope, MLA_NOPE) * qn[0:1]).astype(BF16)
                q_rope = rope_part(qm[i][:, base + MLA_NOPE:base + MLA_QPAD],
                                   qs[i][:, hd * ROPE_PAD:(hd + 1) * ROPE_PAD], q_tables)
                q_ref[0, hd, r, MLA_NOPE:] = q_rope.astype(BF16)


def _mla_proj(x, mod, norm_g, cos, sin, w, *, want_q, tm):
    b, l, d = x.shape
    consts = [w["wdq"], w["wdkv"], w["wdpe"], w["qln"], w["kvln"], w["wuq"], w["wuqs"], w["wuk"],
              w["wuvt"], w["qn"], w["kn"]]
    hspec = pl.BlockSpec((1, MLA_HEADS, tm, MLA_QPAD), lambda i, j: (i, 0, j, 0))
    hshape = jax.ShapeDtypeStruct((b, MLA_HEADS, l, MLA_QPAD), BF16)
    out_specs = [hspec, pl.BlockSpec((1, MLA_HEADS * MLA_V, tm), lambda i, j: (i, 0, j))]
    out_shape = [hshape, jax.ShapeDtypeStruct((b, MLA_HEADS * MLA_V, l), BF16)]
    if want_q:
        out_specs = [hspec] + out_specs
        out_shape = [hshape] + out_shape
    return pl.pallas_call(
        functools.partial(_mla_proj_kernel, want_q=want_q),
        grid=(b, l // tm),
        in_specs=[
            pl.BlockSpec((1, tm, d), lambda i, j: (i, j, 0)),
            pl.BlockSpec((1, 6, d), lambda i, j: (i, 0, 0)),
            _const_spec(norm_g.shape),
            pl.BlockSpec((tm, ROPE_PAD), lambda i, j: (j, 0)),
            pl.BlockSpec((tm, ROPE_PAD), lambda i, j: (j, 0)),
        ] + [_const_spec(a.shape) for a in consts],
        out_specs=out_specs,
        out_shape=out_shape,
        compiler_params=_cparams("parallel", "parallel"),
        name="mla_proj_q" if want_q else "mla_proj_kv",
    )(x, mod, norm_g, cos, sin, *consts)


def _attn_kernel(*refs, with_latent, tk, heads):
    if with_latent:
        q_ref, kc_ref, vtc_ref, k_ref, vt_ref, o_ref = refs
    else:
        q_ref, kc_ref, vtc_ref, o_ref = refs
    dv = vtc_ref.shape[1] // heads
    for hd in range(heads):
        q = q_ref[0, hd]
        vrows = slice(hd * dv, (hd + 1) * dv)
        chunks = [(kc_ref, vtc_ref, 0, kc_ref.shape[2])]
        if with_latent:
            chunks += [(k_ref, vt_ref, j * tk, tk) for j in range(k_ref.shape[2] // tk)]

        def scores(idx):
            kr, _, start, size = chunks[idx]
            return _dot_nt(kr[0, hd, start:start + size, :], q)

        s_next = scores(0)
        m = denom = acc = None
        for idx, (_, vr, start, size) in enumerate(chunks):
            s = s_next
            if idx + 1 < len(chunks):
                s_next = scores(idx + 1)
            vt = vr[0, vrows, start:start + size]
            cmax = jnp.max(s, axis=0, keepdims=True)
            if m is None:
                m = cmax
                p = jnp.exp2(s - m)
                denom = jnp.sum(p, axis=0, keepdims=True)
                acc = _dot(vt, p.astype(BF16))
            else:
                m_new = jnp.maximum(m, cmax)
                alpha = jnp.exp2(m - m_new)
                p = jnp.exp2(s - m_new)
                denom = denom * alpha + jnp.sum(p, axis=0, keepdims=True)
                acc = acc * alpha + _dot(vt, p.astype(BF16))
                m = m_new
        o_ref[0, :, vrows] = (acc * (1.0 / denom)).T.astype(BF16)


def _attn_bounded_kernel(q_ref, kc_ref, vtc_ref, k_ref, vt_ref, o_ref, *, tq, tk):
    chunks = [(kc_ref, vtc_ref, 0, kc_ref.shape[2])]
    chunks += [(k_ref, vt_ref, j * tk, tk) for j in range(k_ref.shape[2] // tk)]
    for qi in range(q_ref.shape[2] // tq):
        cols = slice(qi * tq, (qi + 1) * tq)
        q = q_ref[0, 0, cols, :]

        def scores(idx):
            kr, _, start, size = chunks[idx]
            return _dot_nt(kr[0, 0, start:start + size, :], q)

        s_next = scores(0)
        acc = denom8 = None
        for idx, (_, vr, start, size) in enumerate(chunks):
            s = s_next
            if idx + 1 < len(chunks):
                s_next = scores(idx + 1)
            p = jnp.exp2(s)
            part = jnp.sum(p.reshape(size // SUBLANES, SUBLANES, p.shape[1]), axis=0)
            y = _dot(vr[0, :, start:start + size], p.astype(BF16))
            acc = y if acc is None else acc + y
            denom8 = part if denom8 is None else denom8 + part
        denom = jnp.sum(denom8, axis=0, keepdims=True)
        o_ref[0, cols, :] = (acc * (1.0 / denom)).T.astype(BF16)


def _attention(q, kc, vtc, k=None, vt=None, *, tq, tk=ATTN_EXACT_TILE, bounded_tq=None, heads=1):
    b, nh, lq, dq = q.shape
    lc = kc.shape[2]
    dv = vtc.shape[1] // nh
    with_latent = k is not None
    in_specs = [
        pl.BlockSpec((1, heads, tq, dq), lambda i, h, j: (i, h, j, 0)),
        pl.BlockSpec((1, heads, lc, dq), lambda i, h, j: (i, h, 0, 0)),
        pl.BlockSpec((1, heads * dv, lc), lambda i, h, j: (i, h, 0)),
    ]
    args = [q, kc, vtc]
    if with_latent:
        lk = k.shape[2]
        in_specs += [pl.BlockSpec((1, heads, lk, dq), lambda i, h, j: (i, h, 0, 0)),
                     pl.BlockSpec((1, heads * dv, lk), lambda i, h, j: (i, h, 0))]
        args += [k, vt]
    if bounded_tq is not None:
        assert heads == 1
        body = functools.partial(_attn_bounded_kernel, tq=bounded_tq, tk=tk)
        name = "mla_attn_bounded"
    else:
        body = functools.partial(_attn_kernel, with_latent=with_latent, tk=tk, heads=heads)
        name = "mla_attn" if with_latent else "mla_attn_ctx"
    return pl.pallas_call(
        body,
        grid=(b, nh // heads, lq // tq),
        in_specs=in_specs,
        out_specs=pl.BlockSpec((1, tq, heads * dv), lambda i, h, j: (i, j, h)),
        out_shape=jax.ShapeDtypeStruct((b, lq, nh * dv), BF16),
        compiler_params=_cparams("parallel", "parallel", "parallel"),
        name=name,
    )(*args)


def _latent_attention(bounded, q, kc, vtc, k, vt):
    lq = q.shape[2]
    lk = k.shape[2]
    fast = functools.partial(_attention, tq=min(lq, ATTN_Q_STEP), tk=min(lk, ATTN_K_CHUNK),
                             bounded_tq=min(lq, ATTN_Q_TILE))
    exact = functools.partial(_attention, tq=min(lq, ATTN_EXACT_TILE), tk=min(lk, ATTN_EXACT_TILE))
    return lax.cond(bounded, fast, exact, q, kc, vtc, k, vt)


def _ffn_kernel(*refs, tm, fc, mixer_out):
    if mixer_out:
        (xm_ref, xp_ref, xn_ref, om_ref, op_ref, on_ref, wo_ref, mod_ref, g_ref, wa_ref, wg_ref, ca_ref,
         cg_ref, wd_ref, xo_ref, h_scr, ua_scr, ug_scr, act_scr, o_scr) = refs
    else:
        (xm_ref, xp_ref, xn_ref, mod_ref, g_ref, wa_ref, wg_ref, ca_ref, cg_ref, wd_ref,
         xo_ref, h_scr, ua_scr, ug_scr, act_scr) = refs
    i = pl.program_id(1)
    mod = mod_ref[0]
    g = g_ref[...]
    hal = CONV_HALO

    def hidden(x):
        return _modulate(x, g, mod[3:4], mod[4:5])

    x_main, x_nxt, x_prv = xm_ref[0], xn_ref[0], xp_ref[0]
    if mixer_out:
        o_scr[0:tm] = om_ref[0]
        o_nxt = on_ref[0].astype(F32)[:hal]
        o_prv = op_ref[0].astype(F32)[BF16_ROWS - hal:]
        o_scr[tm:] = jnp.concatenate([o_nxt, o_prv], axis=0).astype(BF16)
        y = _dot(o_scr[...], wo_ref[...])
        x_main = x_main + mod[2:3] * y[0:tm]
        x_nxt = x_nxt + mod[2:3] * y[tm:tm + hal]
        x_prv = x_prv + mod[2:3] * y[tm + hal:]

    h_scr[0:tm] = hidden(x_main).astype(BF16)
    nxt = jnp.where(i == pl.num_programs(1) - 1, 0.0, hidden(x_nxt))
    prv = jnp.where(i == 0, 0.0, hidden(x_prv))
    h_scr[tm:] = jnp.concatenate([nxt, prv], axis=0).astype(BF16)

    def put(u_scr, slab, u):
        u_scr[slab, hal:hal + tm, :] = u[0:tm]
        u_scr[slab, hal + tm:, :] = u[tm:tm + hal]
        u_scr[slab, 0:hal, :] = u[tm + hal:]

    def conv(u_scr, slab, cw):
        return (u_scr[slab, hal - 1:hal - 1 + tm, :] * cw[0:1] + u_scr[slab, hal:hal + tm, :] * cw[1:2]
                + u_scr[slab, hal + 1:hal + 1 + tm, :] * cw[2:3] + cw[3:4])

    per = fc // LANES
    for ci in range(wa_ref.shape[1] // fc):
        cols = slice(ci * fc, (ci + 1) * fc)
        ua = _dot(h_scr[...], wa_ref[:, cols])
        ug = _dot(h_scr[...], wg_ref[:, cols])
        for s in range(per):
            put(ua_scr, ci * per + s, ua[:, s * LANES:(s + 1) * LANES])
            put(ug_scr, ci * per + s, ug[:, s * LANES:(s + 1) * LANES])
        for s in range(per):
            slab = ci * per + s
            lanes = slice(slab * LANES, (slab + 1) * LANES)
            a = conv(ua_scr, slab, ca_ref[:, lanes])
            gt = conv(ug_scr, slab, cg_ref[:, lanes])
            act_scr[:, lanes] = (gt * _sigmoid(gt) * a).astype(BF16)
    xo_ref[0] = x_main + mod[5:6] * _dot(act_scr[...], wd_ref[...])


def _conv_ffn(x, mod, norm_g, wa, wg, ca, cg, wd, *, tm, fc, mixer_out=None):
    b, l, d = x.shape
    hal = CONV_HALO
    nt = l // tm
    d_ff = wa.shape[1]

    def halo_specs(rows, width):
        per, last = tm // rows, l // rows - 1
        return [pl.BlockSpec((1, rows, width), lambda i, j: (i, jnp.maximum(j * per - 1, 0), 0)),
                pl.BlockSpec((1, rows, width), lambda i, j: (i, jnp.minimum((j + 1) * per, last), 0))]

    in_specs = [pl.BlockSpec((1, tm, d), lambda i, j: (i, j, 0))] + halo_specs(hal, d)
    args = [x, x, x]
    scratch = [
        pltpu.VMEM((tm + 2 * hal, d), BF16),
        pltpu.VMEM((d_ff // LANES, tm + 2 * hal, LANES), F32),
        pltpu.VMEM((d_ff // LANES, tm + 2 * hal, LANES), F32),
        pltpu.VMEM((tm, d_ff), BF16),
    ]
    if mixer_out is not None:
        o, w_out = mixer_out
        do = o.shape[2]
        in_specs += [pl.BlockSpec((1, tm, do), lambda i, j: (i, j, 0))] + halo_specs(BF16_ROWS, do)
        in_specs += [_const_spec(w_out.shape)]
        args += [o, o, o, w_out]
        scratch += [pltpu.VMEM((tm + 2 * hal, do), BF16)]
    consts = [norm_g, wa, wg, ca, cg, wd]
    fuse = [False] * (len(args) + 1) + [a is wa or a is wg or a is wd for a in consts]
    return pl.pallas_call(
        functools.partial(_ffn_kernel, tm=tm, fc=fc, mixer_out=mixer_out is not None),
        grid=(b, nt),
        in_specs=in_specs + [pl.BlockSpec((1, 6, d), lambda i, j: (i, 0, 0))] + [_const_spec(a.shape) for a in consts],
        out_specs=pl.BlockSpec((1, tm, d), lambda i, j: (i, j, 0)),
        out_shape=jax.ShapeDtypeStruct(x.shape, F32),
        scratch_shapes=scratch,
        compiler_params=_cparams("parallel", "parallel", allow_input_fusion=fuse),
        name="conv_ffn_mix" if mixer_out is not None else "conv_ffn",
    )(*args, mod, *consts)


def _rope_tables(length):
    nf = MLA_ROPE // 4
    t = jnp.arange(length)
    pos = jnp.stack([(t // GRID_W).astype(F32), (t % GRID_W).astype(F32)], axis=1)
    inv = ROPE_THETA ** (-jnp.arange(nf, dtype=F32) / nf)
    ang = pos[:, :, None] * inv
    cos = jnp.cos(ang)[:, :, None, :] * jnp.ones((1, 1, 2, 1), F32)
    sin = jnp.sin(ang)[:, :, None, :] * jnp.array([-1.0, 1.0], F32)[None, None, :, None]
    pad = lambda a: jnp.pad(a.reshape(length, MLA_ROPE), ((0, 0), (0, ROPE_PAD - MLA_ROPE)))
    return pad(cos), pad(sin)


def _rope_partner(a):
    nf = MLA_ROPE // 4
    r = a.reshape(a.shape[:-1] + (2, 2, nf))
    return jnp.flip(r, axis=-2).reshape(a.shape)


def _pad_last(a, width):
    return jnp.pad(a, [(0, 0)] * (a.ndim - 1) + [(0, width - a.shape[-1])])


def _mla_weights(w_down, q_lora_norm, kv_lora_norm, w_uq, w_ukv, q_norm, k_norm):
    q_rank = q_lora_norm.shape[0]
    kv_rank = kv_lora_norm.shape[0]
    qk = MLA_NOPE + MLA_ROPE
    w_pe = w_down[:, q_rank + kv_rank:]
    wuq = w_uq.reshape(q_rank, MLA_HEADS, qk)
    wuq_rope = wuq[..., MLA_NOPE:]
    wukv = w_ukv.reshape(kv_rank, MLA_HEADS, MLA_NOPE + MLA_V)

    def max_sq_norm(g):
        return MLA_NOPE * jnp.max(g[:MLA_NOPE] ** 2) + MLA_ROPE * jnp.max(g[MLA_NOPE:] ** 2)

    bound = jnp.sqrt(max_sq_norm(q_norm) * max_sq_norm(k_norm)) * (qk ** -0.5 * LOG2_E * 1.02)

    def gains(g):
        rope = g[MLA_NOPE:]
        return jnp.stack([g[:MLA_NOPE], _pad_last(rope, ROPE_PAD), _pad_last(_rope_partner(rope), ROPE_PAD)])

    return {
        "bounded": bound <= ATTN_MAX_SCORE,
        "wdq": w_down[:, :q_rank].astype(BF16),
        "wdkv": w_down[:, q_rank:q_rank + kv_rank].astype(BF16),
        "wdpe": jnp.concatenate([_pad_last(w_pe, ROPE_PAD), _pad_last(_rope_partner(w_pe), ROPE_PAD)],
                                axis=1).astype(BF16),
        "qln": q_lora_norm[None, :],
        "kvln": kv_lora_norm[None, :],
        "wuq": _pad_last(wuq, MLA_QPAD).reshape(q_rank, MLA_HEADS * MLA_QPAD).astype(BF16),
        "wuqs": _pad_last(_rope_partner(wuq_rope), ROPE_PAD).reshape(q_rank, MLA_HEADS * ROPE_PAD).astype(BF16),
        "wuk": wukv[..., :MLA_NOPE].reshape(kv_rank, MLA_HEADS * MLA_NOPE).astype(BF16),
        "wuvt": wukv[..., MLA_NOPE:].reshape(kv_rank, MLA_HEADS * MLA_V).T.astype(BF16),
        "qn": gains(q_norm),
        "kn": gains(k_norm),
    }


def kernel(x, c, ctx, c_ctx, w_ada, b_ada, norm_mix, norm_ffn, gla_w_in, gla_gate_w1, gla_gate_w2, gla_gate_b, gla_out_norm, gla_w_out, mla_w_down, mla_q_lora_norm, mla_kv_lora_norm, mla_w_uq, mla_w_ukv, mla_q_norm, mla_k_norm, mla_w_out, ffn_w_up, ffn_conv_w, ffn_conv_b, ffn_w_down):
    bsz, seq, d = x.shape
    lc = ctx.shape[1]
    depth = w_ada.shape[0]
    d_ff = ffn_w_down.shape[1]
    dqk = gla_gate_w2.shape[-1]
    rank = gla_gate_w1.shape[-1]
    dv = gla_out_norm.shape[-1]

    cond_rows = -(-(bsz + 1) // 8) * 8
    cond = jnp.zeros((cond_rows, d), F32).at[:bsz].set(c).at[bsz].set(c_ctx)
    mods = _ada_all(cond, w_ada, b_ada)

    cos_l, sin_l = _rope_tables(seq)
    cos_c = _pad_last(jnp.ones((lc, MLA_ROPE), F32), ROPE_PAD)
    sin_c = jnp.zeros((lc, ROPE_PAD), F32)

    tm_l = min(seq, ROW_TILE)
    tm_c = min(lc, ROW_TILE)
    xc = ctx
    for i in range(depth):
        last = i == depth - 1
        j = i // 2
        mod_l = mods[i, :bsz].reshape(bsz, 6, d)
        mod_c = jnp.broadcast_to(mods[i, bsz].reshape(1, 6, d), (bsz, 6, d))
        g_mix = norm_mix[i][None, :]
        mix_l = mix_c = None
        if i % 2 == 0:
            w_in = gla_w_in[j].astype(BF16)
            w1 = _pad_last(jnp.concatenate([gla_gate_w1[j, 0], gla_gate_w1[j, 1]], axis=1),
                           GLA_RANK_PAD).astype(BF16)
            w2 = jnp.zeros((GLA_RANK_PAD, 2 * dqk), F32)
            w2 = w2.at[:rank, :dqk].set(gla_gate_w2[j, 0]).at[rank:2 * rank, dqk:].set(gla_gate_w2[j, 1])
            w2 = w2.astype(BF16)
            gb = gla_gate_b[j].reshape(1, 2 * dqk)
            gn = gla_out_norm[j][None, :]
            w_out = gla_w_out[j].astype(BF16)
            s0 = jnp.zeros((bsz, GLA_HEADS, dv, dqk // GLA_HEADS), F32)

            vc, rc, *dirs_c = _gla_proj(xc, mod_c, g_mix, w_in, w1, w2, gb, tm=tm_c)
            v, r, *dirs_l = _gla_proj(x, mod_l, g_mix, w_in, w1, w2, gb, tm=tm_l)
            fwd_c, bwd_c, fwd_l, bwd_l = dirs_c[:2], dirs_c[2:], dirs_l[:2], dirs_l[2:]
            if last:
                _, s_fwd = _gla_scan(*fwd_c, vc, s0, direction=0, want_o=False)
                _, s_bwd = _gla_scan(*bwd_c, vc, s0, direction=1, want_o=False)
            else:
                oc_b, s_bwd = _gla_scan(*bwd_c, vc, s0, direction=1)
                xc, s_fwd = _gla_scan(*fwd_c, vc, s0, direction=0, fused=(oc_b, rc, xc, mod_c, gn, w_out))
            o_b, _ = _gla_scan(*bwd_l, v, s_bwd, direction=1)
            x, _ = _gla_scan(*fwd_l, v, s_fwd, direction=0, fused=(o_b, r, x, mod_l, gn, w_out))
        else:
            w = _mla_weights(mla_w_down[j], mla_q_lora_norm[j], mla_kv_lora_norm[j], mla_w_uq[j],
                             mla_w_ukv[j], mla_q_norm[j], mla_k_norm[j])
            w_out = mla_w_out[j].astype(BF16)
            if last:
                kc, vtc = _mla_proj(xc, mod_c, g_mix, cos_c, sin_c, w, want_q=False, tm=tm_c)
            else:
                qc, kc, vtc = _mla_proj(xc, mod_c, g_mix, cos_c, sin_c, w, want_q=True, tm=tm_c)
            q, k, vt = _mla_proj(x, mod_l, g_mix, cos_l, sin_l, w, want_q=True, tm=min(seq, MLA_PROJ_TILE))
            mix_l = (_latent_attention(w["bounded"], q, kc, vtc, k, vt), w_out)
            if not last:
                mix_c = (_attention(qc, kc, vtc, tq=min(lc, ATTN_EXACT_TILE), heads=MLA_HEADS), w_out)

        g_ffn = norm_ffn[i][None, :]
        wa = ffn_w_up[i, :, :d_ff].astype(BF16)
        wg = ffn_w_up[i, :, d_ff:].astype(BF16)
        conv = jnp.concatenate([ffn_conv_w[i], ffn_conv_b[i][None, :]], axis=0)
        ca, cg = conv[:, :d_ff], conv[:, d_ff:]
        wd = ffn_w_down[i].astype(BF16)
        x = _conv_ffn(x, mod_l, g_ffn, wa, wg, ca, cg, wd, tm=tm_l, fc=FFN_COLS, mixer_out=mix_l)
        if not last:
            xc = _conv_ffn(xc, mod_c, g_ffn, wa, wg, ca, cg, wd, tm=tm_c, fc=FFN_COLS, mixer_out=mix_c)
    return x
```

```python
import functools

import jax
import jax.numpy as jnp
from jax import lax
from jax.experimental import pallas as pl
from jax.experimental.pallas import tpu as pltpu

F32 = jnp.float32
BF16 = jnp.bfloat16

LANES = 128
SUBLANES = 8
BF16_ROWS = 16
EPS = 1e-6
GRID_W = 64
ROPE_THETA = 10000.0
LOG2_E = 1.4426950408889634

GLA_HEADS = 4
GLA_GATE_NORMALIZER = 16.0
GLA_CHUNK = 128
GLA_RANK_PAD = 128

MLA_HEADS = 8
MLA_NOPE = 128
MLA_ROPE = 64
MLA_V = 128
MLA_QPAD = 256
ROPE_PAD = MLA_QPAD - MLA_NOPE
MLA_PROJ_ROWS = 256
ATTN_MAX_SCORE = 50.0

ROW_TILE = 512
MLA_PROJ_TILE = 1024
SCAN_BLOCK = 1024
ATTN_Q_STEP = 4096
ATTN_Q_TILE = 2048
ATTN_K_CHUNK = 2048
ATTN_EXACT_TILE = 512
FFN_COLS = 256
CONV_HALO = SUBLANES
VMEM_LIMIT = 56 * 1024 * 1024


def _cparams(*sem):
    return pltpu.CompilerParams(dimension_semantics=sem, vmem_limit_bytes=VMEM_LIMIT)


def _const_spec(shape):
    nd = len(shape)
    return pl.BlockSpec(shape, lambda *_: (0,) * nd, pipeline_mode=pl.Buffered(1))


def _dot(a, b):
    return jnp.dot(a, b, preferred_element_type=F32)


def _dot_nt(a, b):
    return lax.dot_general(a, b, (((1,), (1,)), ((), ())), preferred_element_type=F32)


def _dot_tn(a, b):
    return lax.dot_general(a, b, (((0,), (0,)), ((), ())), preferred_element_type=F32)


def _sigmoid(x):
    return 1.0 / (1.0 + jnp.exp(-x))


def _rms_scale(x, width):
    ss = jnp.sum(x * x, axis=-1, keepdims=True)
    return lax.rsqrt(ss * (1.0 / width) + EPS)


def _modulate(x, g, shift, scale):
    return x * _rms_scale(x, x.shape[-1]) * (g * (1.0 + scale)) + shift


def _ada_kernel(cond_ref, w_ref, b_ref, o_ref):
    cond = cond_ref[...]
    s = cond * _sigmoid(cond)
    o_ref[0] = jnp.dot(s, w_ref[0], preferred_element_type=F32,
                       precision=lax.Precision.HIGHEST) + b_ref[0]


def _ada_all(cond, w_ada, b_ada):
    depth, d, d6 = w_ada.shape
    rows = cond.shape[0]
    cols = 2 * d
    return pl.pallas_call(
        _ada_kernel,
        grid=(depth, d6 // cols),
        in_specs=[
            pl.BlockSpec((rows, d), lambda i, j: (0, 0)),
            pl.BlockSpec((1, d, cols), lambda i, j: (i, 0, j)),
            pl.BlockSpec((1, 1, cols), lambda i, j: (i, 0, j)),
        ],
        out_specs=pl.BlockSpec((1, rows, cols), lambda i, j: (i, 0, j)),
        out_shape=jax.ShapeDtypeStruct((depth, rows, d6), F32),
        compiler_params=_cparams("parallel", "parallel"),
        name="ada_mod",
    )(cond, w_ada, b_ada.reshape(depth, 1, d6))


def _gla_proj_kernel(x_ref, mod_ref, g_ref, win_ref, w1_ref, w2_ref, gb_ref,
                     *out_refs, dqk, part):
    mod = mod_ref[0]
    h = _modulate(x_ref[0], g_ref[...], mod[0:1], mod[1:2]).astype(BF16)
    if part == "vr":
        v_ref, r_ref = out_refs
        dv = v_ref.shape[2]
        pvr = _dot(h, win_ref[:, 2 * dqk:])
        v_ref[0] = pvr[:, :dv].astype(BF16)
        r_ref[0] = pvr[:, dv:].astype(BF16)
        return
    qkf_ref, ff_ref, qkb_ref, fb_ref = out_refs
    c = GLA_CHUNK
    dk = dqk // GLA_HEADS
    low = _dot(h, w1_ref[...]).astype(BF16)
    z = _dot(low, w2_ref[...]) + gb_ref[...]
    pqk = _dot(h, win_ref[:, :2 * dqk])
    unit = LOG2_E / GLA_GATE_NORMALIZER
    gates = jnp.minimum(z, 0.0) * unit - jnp.log(1.0 + jnp.exp(-jnp.abs(z))) * unit
    row = lax.broadcasted_iota(jnp.int32, (c, c), 0)
    col = lax.broadcasted_iota(jnp.int32, (c, c), 1)
    for direction, (qk_ref, f_ref) in enumerate(((qkf_ref, ff_ref), (qkb_ref, fb_ref))):
        reverse = direction == 1
        tri = ((col >= row) if reverse else (col <= row)).astype(BF16)
        tri2 = jnp.concatenate([tri, tri], axis=1)
        i_mid = c // 2 if reverse else c // 2 - 1
        i_end = 0 if reverse else c - 1
        g = gates[:, direction * dqk:(direction + 1) * dqk]
        g_hi = g.astype(BF16)
        g_lo = (g - g_hi.astype(F32)).astype(BF16)
        for ci in range(x_ref.shape[1] // c):
            rows = slice(ci * c, (ci + 1) * c)
            cum = _dot(tri2, jnp.concatenate([g_hi[rows], g_lo[rows]], axis=0))
            c_mid = cum[i_mid:i_mid + 1, :]
            c_end = cum[i_end:i_end + 1, :]
            rel = cum - c_mid
            qk_ref[0, rows, :dqk] = (pqk[rows, :dqk] * (jnp.exp2(rel) * (dk ** -0.5))).astype(BF16)
            qk_ref[0, rows, dqk:] = (pqk[rows, dqk:] * jnp.exp2(-rel)).astype(BF16)
            factors = jnp.concatenate([jnp.exp2(c_mid), jnp.exp2(c_end - c_mid), jnp.exp2(c_end)], axis=1)
            f_ref[0, ci * SUBLANES:(ci + 1) * SUBLANES, :] = jnp.broadcast_to(factors, (SUBLANES, 3 * dqk))


def _gla_proj(x, mod, norm_g, w_in, w1, w2, gb, *, tm):
    b, l, d = x.shape
    dqk = w2.shape[1] // 2
    dv = (w_in.shape[1] - 2 * dqk) // 2
    drows = tm // GLA_CHUNK * SUBLANES
    row = lambda i, j: (i, j, 0)
    tok = lambda width, dtype: (pl.BlockSpec((1, tm, width), row), jax.ShapeDtypeStruct((b, l, width), dtype))
    fac = (pl.BlockSpec((1, drows, 3 * dqk), row),
           jax.ShapeDtypeStruct((b, l // GLA_CHUNK * SUBLANES, 3 * dqk), F32))
    results = []
    for part, outs in (("vr", [tok(dv, BF16), tok(dv, BF16)]), ("qk", 2 * [tok(2 * dqk, BF16), fac])):
        results += _gla_proj_part(x, mod, norm_g, w_in, w1, w2, gb, tm=tm, dqk=dqk, part=part, outs=outs)
    return results


def _gla_proj_part(x, mod, norm_g, w_in, w1, w2, gb, *, tm, dqk, part, outs):
    b, l, d = x.shape
    row = lambda i, j: (i, j, 0)
    return pl.pallas_call(
        functools.partial(_gla_proj_kernel, dqk=dqk, part=part),
        grid=(b, l // tm),
        in_specs=[
            pl.BlockSpec((1, tm, d), row),
            pl.BlockSpec((1, 6, d), lambda i, j: (i, 0, 0)),
            _const_spec(norm_g.shape),
            _const_spec(w_in.shape),
            _const_spec(w1.shape),
            _const_spec(w2.shape),
            _const_spec(gb.shape),
        ],
        out_specs=[o[0] for o in outs],
        out_shape=[o[1] for o in outs],
        compiler_params=_cparams("parallel", "parallel"),
        name="gla_proj_" + part,
    )(x, mod, norm_g, w_in, w1, w2, gb)


def _gla_scan_kernel(*refs, reverse, want_o, fuse_out, tblk):
    if fuse_out:
        (qk_ref, v_ref, f_ref, s0_ref, ob_ref, r_ref, x_ref, mod_ref, gn_ref, wout_ref,
         xo_ref, st_ref, y_scr) = refs
    elif want_o:
        qk_ref, v_ref, f_ref, s0_ref, o_ref, st_ref = refs
    else:
        qk_ref, v_ref, f_ref, s0_ref, st_ref = refs

    @pl.when(pl.program_id(1) == 0)
    def _():
        st_ref[...] = s0_ref[...]

    c = GLA_CHUNK
    dqk = qk_ref.shape[2] // 2
    dk = dqk // GLA_HEADS
    dv = v_ref.shape[2] // GLA_HEADS
    row = lax.broadcasted_iota(jnp.int32, (c, c), 0)
    col = lax.broadcasted_iota(jnp.int32, (c, c), 1)
    mask = (col >= row) if reverse else (col <= row)

    nch = tblk // c
    states = [st_ref[0, h] for h in range(GLA_HEADS)]
    for ci in (reversed(range(nch)) if reverse else range(nch)):
        rows = slice(ci * c, (ci + 1) * c)
        frow = slice(ci * SUBLANES, ci * SUBLANES + 1)
        for h in range(GLA_HEADS):
            qs = slice(h * dk, (h + 1) * dk)
            ks = slice(dqk + h * dk, dqk + (h + 1) * dk)
            vs = slice(h * dv, (h + 1) * dv)
            v = v_ref[0, rows, vs]
            st = states[h]
            k_in = qk_ref[0, rows, ks]
            k_st = (k_in.astype(F32) * f_ref[0, frow, ks]).astype(BF16)
            if want_o:
                q_in = qk_ref[0, rows, qs]
                q_st = (q_in.astype(F32) * f_ref[0, frow, qs]).astype(BF16)
                scores = jnp.where(mask, _dot_nt(q_in, k_in), 0.0)
                o = _dot(scores.astype(BF16), v) + _dot_nt(q_st, st.astype(BF16))
                if fuse_out:
                    o = o + ob_ref[0, rows, vs].astype(F32)
                    r = r_ref[0, rows, vs].astype(F32)
                    y = o * _rms_scale(o, dv) * gn_ref[...] * (r * _sigmoid(r))
                    y_scr[rows, vs] = y.astype(BF16)
                else:
                    o_ref[0, rows, vs] = o.astype(BF16)
            decay_end = f_ref[0, frow, 2 * dqk + h * dk:2 * dqk + (h + 1) * dk]
            states[h] = st * decay_end + _dot_tn(v, k_st)
    for h in range(GLA_HEADS):
        st_ref[0, h] = states[h]

    if fuse_out:
        gate = mod_ref[0][2:3]
        xo_ref[0] = x_ref[0] + gate * _dot(y_scr[...], wout_ref[...])


def _gla_scan(qk, factors, v, s0, *, direction, want_o=True, fused=None):
    b, l, dq2 = qk.shape
    dvt = v.shape[2]
    tblk = min(l, SCAN_BLOCK)
    nblk = l // tblk
    reverse = direction == 1
    blk = (lambda i, j: (i, nblk - 1 - j, 0)) if reverse else (lambda i, j: (i, j, 0))
    st_spec = pl.BlockSpec((1,) + s0.shape[1:], lambda i, j: (i, 0, 0, 0))
    in_specs = [
        pl.BlockSpec((1, tblk, dq2), blk),
        pl.BlockSpec((1, tblk, dvt), blk),
        pl.BlockSpec((1, tblk // GLA_CHUNK * SUBLANES, factors.shape[2]), blk),
        st_spec,
    ]
    args = [qk, v, factors, s0]
    st_shape = jax.ShapeDtypeStruct(s0.shape, F32)
    scratch = []
    if fused is not None:
        o_other, r, x, mod, gn, w_out = fused
        d = x.shape[2]
        in_specs += [
            pl.BlockSpec((1, tblk, dvt), blk),
            pl.BlockSpec((1, tblk, dvt), blk),
            pl.BlockSpec((1, tblk, d), blk),
            pl.BlockSpec((1, 6, d), lambda i, j: (i, 0, 0)),
            _const_spec(gn.shape),
            _const_spec(w_out.shape),
        ]
        args += [o_other, r, x, mod, gn, w_out]
        out_specs = [pl.BlockSpec((1, tblk, d), blk), st_spec]
        out_shape = [jax.ShapeDtypeStruct(x.shape, F32), st_shape]
        scratch = [pltpu.VMEM((tblk, dvt), BF16)]
    elif want_o:
        out_specs = [pl.BlockSpec((1, tblk, dvt), blk), st_spec]
        out_shape = [jax.ShapeDtypeStruct((b, l, dvt), BF16), st_shape]
    else:
        out_specs = [st_spec]
        out_shape = [st_shape]
    outs = pl.pallas_call(
        functools.partial(_gla_scan_kernel, reverse=reverse, want_o=want_o,
                          fuse_out=fused is not None, tblk=tblk),
        grid=(b, nblk),
        in_specs=in_specs,
        out_specs=out_specs,
        out_shape=out_shape,
        scratch_shapes=scratch,
        compiler_params=_cparams("parallel", "arbitrary"),
        name="gla_scan_%s%s" % ("bwd" if reverse else "fwd", "_out" if fused is not None else ""),
    )(*args)
    if want_o:
        return outs[0], outs[1]
    return None, outs[0]


def _mla_proj_kernel(*refs, want_q):
    (x_ref, mod_ref, g_ref, cos_ref, sin_ref, wdq_ref, wdkv_ref, wdpe_ref, qln_ref, kvln_ref,
     wuq_ref, wuqs_ref, wuk_ref, wuvt_ref, qn_ref, kn_ref) = refs[:16]
    if want_q:
        q_ref, k_ref, vt_ref = refs[16:]
    else:
        k_ref, vt_ref = refs[16:]
    mod = mod_ref[0]
    kn = kn_ref[...]
    qn = qn_ref[...] * ((MLA_NOPE + MLA_ROPE) ** -0.5 * LOG2_E)
    sub = min(x_ref.shape[1], MLA_PROJ_ROWS)
    parts = [slice(i * sub, (i + 1) * sub) for i in range(x_ref.shape[1] // sub)]

    def rope_tables(gains, rows):
        return cos_ref[rows, :] * gains[1:2], sin_ref[rows, :] * gains[2:3]

    def rope_part(raw, raw_partner, tables):
        return (raw * tables[0] + raw_partner * tables[1]) * _rms_scale(raw, MLA_ROPE)

    def normed(c, gain_ref):
        return (c * _rms_scale(c, c.shape[-1]) * gain_ref[...]).astype(BF16)

    hs = [_modulate(x_ref[0, r, :], g_ref[...], mod[0:1], mod[1:2]).astype(BF16) for r in parts]
    c_kv = [_dot(h, wdkv_ref[...]) for h in hs]
    kpe = [_dot(h, wdpe_ref[...]) for h in hs]
    if want_q:
        c_q = [_dot(h, wdq_ref[...]) for h in hs]
    c_kv = [normed(c, kvln_ref) for c in c_kv]
    for c, r in zip(c_kv, parts):
        vt_ref[0, :, r] = _dot_nt(wuvt_ref[...], c).astype(BF16)
    kn_all = [_dot(c, wuk_ref[...]) for c in c_kv]
    k_rope = [rope_part(p[:, :ROPE_PAD], p[:, ROPE_PAD:], rope_tables(kn, r)).astype(BF16) for p, r in zip(kpe, parts)]
    if want_q:
        c_q = [normed(c, qln_ref) for c in c_q]
        qm = [_dot(c, wuq_ref[...]) for c in c_q]
        qs = [_dot(c, wuqs_ref[...]) for c in c_q]
    for i, r in enumerate(parts):
        if want_q:
            q_tables = rope_tables(qn, r)
        for hd in range(MLA_HEADS):
            k_nope = kn_all[i][:, hd * MLA_NOPE:(hd + 1) * MLA_NOPE]
            k_ref[0, hd, r, :MLA_NOPE] = (k_nope * _rms_scale(k_nope, MLA_NOPE) * kn[0:1]).astype(BF16)
            k_ref[0, hd, r, MLA_NOPE:] = k_rope[i]
            if want_q:
                base = hd * MLA_QPAD
                q_nope = qm[i][:, base:base + MLA_NOPE]
                q_ref[0, hd, r, :MLA_NOPE] = (q_nope * _rms_scale(q_nope, MLA_NOPE) * qn[0:1]).astype(BF16)
                q_rope = rope_part(qm[i][:, base + MLA_NOPE:base + MLA_QPAD],
                                   qs[i][:, hd * ROPE_PAD:(hd + 1) * ROPE_PAD], q_tables)
                q_ref[0, hd, r, MLA_NOPE:] = q_rope.astype(BF16)


def _mla_proj(x, mod, norm_g, cos, sin, w, *, want_q, tm):
    b, l, d = x.shape
    consts = [w["wdq"], w["wdkv"], w["wdpe"], w["qln"], w["kvln"], w["wuq"], w["wuqs"], w["wuk"],
              w["wuvt"], w["qn"], w["kn"]]
    hspec = pl.BlockSpec((1, MLA_HEADS, tm, MLA_QPAD), lambda i, j: (i, 0, j, 0))
    hshape = jax.ShapeDtypeStruct((b, MLA_HEADS, l, MLA_QPAD), BF16)
    out_specs = [hspec, pl.BlockSpec((1, MLA_HEADS * MLA_V, tm), lambda i, j: (i, 0, j))]
    out_shape = [hshape, jax.ShapeDtypeStruct((b, MLA_HEADS * MLA_V, l), BF16)]
    if want_q:
        out_specs = [hspec] + out_specs
        out_shape = [hshape] + out_shape
    return pl.pallas_call(
        functools.partial(_mla_proj_kernel, want_q=want_q),
        grid=(b, l // tm),
        in_specs=[
            pl.BlockSpec((1, tm, d), lambda i, j: (i, j, 0)),
            pl.BlockSpec((1, 6, d), lambda i, j: (i, 0, 0)),
            _const_spec(norm_g.shape),
            pl.BlockSpec((tm, ROPE_PAD), lambda i, j: (j, 0)),
            pl.BlockSpec((tm, ROPE_PAD), lambda i, j: (j, 0)),
        ] + [_const_spec(a.shape) for a in consts],
        out_specs=out_specs,
        out_shape=out_shape,
        compiler_params=_cparams("parallel", "parallel"),
        name="mla_proj_q" if want_q else "mla_proj_kv",
    )(x, mod, norm_g, cos, sin, *consts)


def _attn_kernel(*refs, with_latent, tk, heads):
    if with_latent:
        q_ref, kc_ref, vtc_ref, k_ref, vt_ref, o_ref = refs
    else:
        q_ref, kc_ref, vtc_ref, o_ref = refs
    dv = vtc_ref.shape[1] // heads
    for hd in range(heads):
        q = q_ref[0, hd]
        vrows = slice(hd * dv, (hd + 1) * dv)
        chunks = [(kc_ref, vtc_ref, 0, kc_ref.shape[2])]
        if with_latent:
            chunks += [(k_ref, vt_ref, j * tk, tk) for j in range(k_ref.shape[2] // tk)]

        def scores(idx):
            kr, _, start, size = chunks[idx]
            return _dot_nt(kr[0, hd, start:start + size, :], q)

        s_next = scores(0)
        m = denom = acc = None
        for idx, (_, vr, start, size) in enumerate(chunks):
            s = s_next
            if idx + 1 < len(chunks):
                s_next = scores(idx + 1)
            vt = vr[0, vrows, start:start + size]
            cmax = jnp.max(s, axis=0, keepdims=True)
            if m is None:
                m = cmax
                p = jnp.exp2(s - m)
                denom = jnp.sum(p, axis=0, keepdims=True)
                acc = _dot(vt, p.astype(BF16))
            else:
                m_new = jnp.maximum(m, cmax)
                alpha = jnp.exp2(m - m_new)
                p = jnp.exp2(s - m_new)
                denom = denom * alpha + jnp.sum(p, axis=0, keepdims=True)
                acc = acc * alpha + _dot(vt, p.astype(BF16))
                m = m_new
        o_ref[0, :, vrows] = (acc * (1.0 / denom)).T.astype(BF16)


def _attn_bounded_kernel(q_ref, kc_ref, vtc_ref, k_ref, vt_ref, o_ref, *, tq, tk):
    chunks = [(kc_ref, vtc_ref, 0, kc_ref.shape[2])]
    chunks += [(k_ref, vt_ref, j * tk, tk) for j in range(k_ref.shape[2] // tk)]
    for qi in range(q_ref.shape[2] // tq):
        cols = slice(qi * tq, (qi + 1) * tq)
        q = q_ref[0, 0, cols, :]

        def scores(idx):
            kr, _, start, size = chunks[idx]
            return _dot_nt(kr[0, 0, start:start + size, :], q)

        s_next = scores(0)
        acc = denom8 = None
        for idx, (_, vr, start, size) in enumerate(chunks):
            s = s_next
            if idx + 1 < len(chunks):
                s_next = scores(idx + 1)
            p = jnp.exp2(s)
            part = jnp.sum(p.reshape(size // SUBLANES, SUBLANES, p.shape[1]), axis=0)
            y = _dot(vr[0, :, start:start + size], p.astype(BF16))
            acc = y if acc is None else acc + y
            denom8 = part if denom8 is None else denom8 + part
        denom = jnp.sum(denom8, axis=0, keepdims=True)
        o_ref[0, cols, :] = (acc * (1.0 / denom)).T.astype(BF16)


def _attention(q, kc, vtc, k=None, vt=None, *, tq, tk=ATTN_EXACT_TILE, bounded_tq=None, heads=1):
    b, nh, lq, dq = q.shape
    lc = kc.shape[2]
    dv = vtc.shape[1] // nh
    with_latent = k is not None
    in_specs = [
        pl.BlockSpec((1, heads, tq, dq), lambda i, h, j: (i, h, j, 0)),
        pl.BlockSpec((1, heads, lc, dq), lambda i, h, j: (i, h, 0, 0)),
        pl.BlockSpec((1, heads * dv, lc), lambda i, h, j: (i, h, 0)),
    ]
    args = [q, kc, vtc]
    if with_latent:
        lk = k.shape[2]
        in_specs += [pl.BlockSpec((1, heads, lk, dq), lambda i, h, j: (i, h, 0, 0)),
                     pl.BlockSpec((1, heads * dv, lk), lambda i, h, j: (i, h, 0))]
        args += [k, vt]
    if bounded_tq is not None:
        assert heads == 1
        body = functools.partial(_attn_bounded_kernel, tq=bounded_tq, tk=tk)
        name = "mla_attn_bounded"
    else:
        body = functools.partial(_attn_kernel, with_latent=with_latent, tk=tk, heads=heads)
        name = "mla_attn" if with_latent else "mla_attn_ctx"
    return pl.pallas_call(
        body,
        grid=(b, nh // heads, lq // tq),
        in_specs=in_specs,
        out_specs=pl.BlockSpec((1, tq, heads * dv), lambda i, h, j: (i, j, h)),
        out_shape=jax.ShapeDtypeStruct((b, lq, nh * dv), BF16),
        compiler_params=_cparams("parallel", "parallel", "parallel"),
        name=name,
    )(*args)


def _latent_attention(bounded, q, kc, vtc, k, vt):
    lq = q.shape[2]
    lk = k.shape[2]
    fast = functools.partial(_attention, tq=min(lq, ATTN_Q_STEP), tk=min(lk, ATTN_K_CHUNK),
                             bounded_tq=min(lq, ATTN_Q_TILE))
    exact = functools.partial(_attention, tq=min(lq, ATTN_EXACT_TILE), tk=min(lk, ATTN_EXACT_TILE))
    return lax.cond(bounded, fast, exact, q, kc, vtc, k, vt)


def _ffn_kernel(*refs, tm, fc, mixer_out):
    if mixer_out:
        (xm_ref, xp_ref, xn_ref, om_ref, op_ref, on_ref, wo_ref, mod_ref, g_ref, wa_ref, wg_ref, ca_ref,
         cg_ref, wd_ref, xo_ref, h_scr, ua_scr, ug_scr, act_scr, o_scr) = refs
    else:
        (xm_ref, xp_ref, xn_ref, mod_ref, g_ref, wa_ref, wg_ref, ca_ref, cg_ref, wd_ref,
         xo_ref, h_scr, ua_scr, ug_scr, act_scr) = refs
    i = pl.program_id(1)
    mod = mod_ref[0]
    g = g_ref[...]
    hal = CONV_HALO

    def hidden(x):
        return _modulate(x, g, mod[3:4], mod[4:5])

    x_main, x_nxt, x_prv = xm_ref[0], xn_ref[0], xp_ref[0]
    if mixer_out:
        o_scr[0:tm] = om_ref[0]
        o_nxt = on_ref[0].astype(F32)[:hal]
        o_prv = op_ref[0].astype(F32)[BF16_ROWS - hal:]
        o_scr[tm:] = jnp.concatenate([o_nxt, o_prv], axis=0).astype(BF16)
        y = _dot(o_scr[...], wo_ref[...])
        x_main = x_main + mod[2:3] * y[0:tm]
        x_nxt = x_nxt + mod[2:3] * y[tm:tm + hal]
        x_prv = x_prv + mod[2:3] * y[tm + hal:]

    h_scr[0:tm] = hidden(x_main).astype(BF16)
    nxt = jnp.where(i == pl.num_programs(1) - 1, 0.0, hidden(x_nxt))
    prv = jnp.where(i == 0, 0.0, hidden(x_prv))
    h_scr[tm:] = jnp.concatenate([nxt, prv], axis=0).astype(BF16)

    def put(u_scr, slab, u):
        u_scr[slab, hal:hal + tm, :] = u[0:tm]
        u_scr[slab, hal + tm:, :] = u[tm:tm + hal]
        u_scr[slab, 0:hal, :] = u[tm + hal:]

    def conv(u_scr, slab, cw):
        return (u_scr[slab, hal - 1:hal - 1 + tm, :] * cw[0:1] + u_scr[slab, hal:hal + tm, :] * cw[1:2]
                + u_scr[slab, hal + 1:hal + 1 + tm, :] * cw[2:3] + cw[3:4])

    per = fc // LANES
    for ci in range(wa_ref.shape[1] // fc):
        cols = slice(ci * fc, (ci + 1) * fc)
        ua = _dot(h_scr[...], wa_ref[:, cols])
        ug = _dot(h_scr[...], wg_ref[:, cols])
        for s in range(per):
            put(ua_scr, ci * per + s, ua[:, s * LANES:(s + 1) * LANES])
            put(ug_scr, ci * per + s, ug[:, s * LANES:(s + 1) * LANES])
        for s in range(per):
            slab = ci * per + s
            lanes = slice(slab * LANES, (slab + 1) * LANES)
            a = conv(ua_scr, slab, ca_ref[:, lanes])
            gt = conv(ug_scr, slab, cg_ref[:, lanes])
            act_scr[:, lanes] = (gt * _sigmoid(gt) * a).astype(BF16)
    xo_ref[0] = x_main + mod[5:6] * _dot(act_scr[...], wd_ref[...])


def _conv_ffn(x, mod, norm_g, wa, wg, ca, cg, wd, *, tm, fc, mixer_out=None):
    b, l, d = x.shape
    hal = CONV_HALO
    nt = l // tm
    d_ff = wa.shape[1]

    def halo_specs(rows, width):
        per, last = tm // rows, l // rows - 1
        return [pl.BlockSpec((1, rows, width), lambda i, j: (i, jnp.maximum(j * per - 1, 0), 0)),
                pl.BlockSpec((1, rows, width), lambda i, j: (i, jnp.minimum((j + 1) * per, last), 0))]

    in_specs = [pl.BlockSpec((1, tm, d), lambda i, j: (i, j, 0))] + halo_specs(hal, d)
    args = [x, x, x]
    scratch = [
        pltpu.VMEM((tm + 2 * hal, d), BF16),
        pltpu.VMEM((d_ff // LANES, tm + 2 * hal, LANES), F32),
        pltpu.VMEM((d_ff // LANES, tm + 2 * hal, LANES), F32),
        pltpu.VMEM((tm, d_ff), BF16),
    ]
    if mixer_out is not None:
        o, w_out = mixer_out
        do = o.shape[2]
        in_specs += [pl.BlockSpec((1, tm, do), lambda i, j: (i, j, 0))] + halo_specs(BF16_ROWS, do)
        in_specs += [_const_spec(w_out.shape)]
        args += [o, o, o, w_out]
        scratch += [pltpu.VMEM((tm + 2 * hal, do), BF16)]
    consts = [norm_g, wa, wg, ca, cg, wd]
    return pl.pallas_call(
        functools.partial(_ffn_kernel, tm=tm, fc=fc, mixer_out=mixer_out is not None),
        grid=(b, nt),
        in_specs=in_specs + [pl.BlockSpec((1, 6, d), lambda i, j: (i, 0, 0))] + [_const_spec(a.shape) for a in consts],
        out_specs=pl.BlockSpec((1, tm, d), lambda i, j: (i, j, 0)),
        out_shape=jax.ShapeDtypeStruct(x.shape, F32),
        scratch_shapes=scratch,
        compiler_params=_cparams("parallel", "parallel"),
        name="conv_ffn_mix" if mixer_out is not None else "conv_ffn",
    )(*args, mod, *consts)


def _rope_tables(length):
    nf = MLA_ROPE // 4
    t = jnp.arange(length)
    pos = jnp.stack([(t // GRID_W).astype(F32), (t % GRID_W).astype(F32)], axis=1)
    inv = ROPE_THETA ** (-jnp.arange(nf, dtype=F32) / nf)
    ang = pos[:, :, None] * inv
    cos = jnp.cos(ang)[:, :, None, :] * jnp.ones((1, 1, 2, 1), F32)
    sin = jnp.sin(ang)[:, :, None, :] * jnp.array([-1.0, 1.0], F32)[None, None, :, None]
    pad = lambda a: jnp.pad(a.reshape(length, MLA_ROPE), ((0, 0), (0, ROPE_PAD - MLA_ROPE)))
    return pad(cos), pad(sin)


def _rope_partner(a):
    nf = MLA_ROPE // 4
    r = a.reshape(a.shape[:-1] + (2, 2, nf))
    return jnp.flip(r, axis=-2).reshape(a.shape)


def _pad_last(a, width):
    return jnp.pad(a, [(0, 0)] * (a.ndim - 1) + [(0, width - a.shape[-1])])


def _mla_weights(w_down, q_lora_norm, kv_lora_norm, w_uq, w_ukv, q_norm, k_norm):
    q_rank = q_lora_norm.shape[0]
    kv_rank = kv_lora_norm.shape[0]
    qk = MLA_NOPE + MLA_ROPE
    w_pe = w_down[:, q_rank + kv_rank:]
    wuq = w_uq.reshape(q_rank, MLA_HEADS, qk)
    wuq_rope = wuq[..., MLA_NOPE:]
    wukv = w_ukv.reshape(kv_rank, MLA_HEADS, MLA_NOPE + MLA_V)

    def max_sq_norm(g):
        return MLA_NOPE * jnp.max(g[:MLA_NOPE] ** 2) + MLA_ROPE * jnp.max(g[MLA_NOPE:] ** 2)

    bound = jnp.sqrt(max_sq_norm(q_norm) * max_sq_norm(k_norm)) * (qk ** -0.5 * LOG2_E * 1.02)

    def gains(g):
        rope = g[MLA_NOPE:]
        return jnp.stack([g[:MLA_NOPE], _pad_last(rope, ROPE_PAD), _pad_last(_rope_partner(rope), ROPE_PAD)])

    return {
        "bounded": bound <= ATTN_MAX_SCORE,
        "wdq": w_down[:, :q_rank].astype(BF16),
        "wdkv": w_down[:, q_rank:q_rank + kv_rank].astype(BF16),
        "wdpe": jnp.concatenate([_pad_last(w_pe, ROPE_PAD), _pad_last(_rope_partner(w_pe), ROPE_PAD)],
                                axis=1).astype(BF16),
        "qln": q_lora_norm[None, :],
        "kvln": kv_lora_norm[None, :],
        "wuq": _pad_last(wuq, MLA_QPAD).reshape(q_rank, MLA_HEADS * MLA_QPAD).astype(BF16),
        "wuqs": _pad_last(_rope_partner(wuq_rope), ROPE_PAD).reshape(q_rank, MLA_HEADS * ROPE_PAD).astype(BF16),
        "wuk": wukv[..., :MLA_NOPE].reshape(kv_rank, MLA_HEADS * MLA_NOPE).astype(BF16),
        "wuvt": wukv[..., MLA_NOPE:].reshape(kv_rank, MLA_HEADS * MLA_V).T.astype(BF16),
        "qn": gains(q_norm),
        "kn": gains(k_norm),
    }


def kernel(x, c, ctx, c_ctx, w_ada, b_ada, norm_mix, norm_ffn, gla_w_in, gla_gate_w1, gla_gate_w2, gla_gate_b, gla_out_norm, gla_w_out, mla_w_down, mla_q_lora_norm, mla_kv_lora_norm, mla_w_uq, mla_w_ukv, mla_q_norm, mla_k_norm, mla_w_out, ffn_w_up, ffn_conv_w, ffn_conv_b, ffn_w_down):
    bsz, seq, d = x.shape
    lc = ctx.shape[1]
    depth = w_ada.shape[0]
    d_ff = ffn_w_down.shape[1]
    dqk = gla_gate_w2.shape[-1]
    rank = gla_gate_w1.shape[-1]
    dv = gla_out_norm.shape[-1]

    cond_rows = -(-(bsz + 1) // 8) * 8
    cond = jnp.zeros((cond_rows, d), F32).at[:bsz].set(c).at[bsz].set(c_ctx)
    mods = _ada_all(cond, w_ada, b_ada)

    cos_l, sin_l = _rope_tables(seq)
    cos_c = _pad_last(jnp.ones((lc, MLA_ROPE), F32), ROPE_PAD)
    sin_c = jnp.zeros((lc, ROPE_PAD), F32)

    tm_l = min(seq, ROW_TILE)
    tm_c = min(lc, ROW_TILE)
    xc = ctx
    for i in range(depth):
        last = i == depth - 1
        j = i // 2
        mod_l = mods[i, :bsz].reshape(bsz, 6, d)
        mod_c = jnp.broadcast_to(mods[i, bsz].reshape(1, 6, d), (bsz, 6, d))
        g_mix = norm_mix[i][None, :]
        mix_l = mix_c = None
        if i % 2 == 0:
            w_in = gla_w_in[j].astype(BF16)
            w1 = _pad_last(jnp.concatenate([gla_gate_w1[j, 0], gla_gate_w1[j, 1]], axis=1),
                           GLA_RANK_PAD).astype(BF16)
            w2 = jnp.zeros((GLA_RANK_PAD, 2 * dqk), F32)
            w2 = w2.at[:rank, :dqk].set(gla_gate_w2[j, 0]).at[rank:2 * rank, dqk:].set(gla_gate_w2[j, 1])
            w2 = w2.astype(BF16)
            gb = gla_gate_b[j].reshape(1, 2 * dqk)
            gn = gla_out_norm[j][None, :]
            w_out = gla_w_out[j].astype(BF16)
            s0 = jnp.zeros((bsz, GLA_HEADS, dv, dqk // GLA_HEADS), F32)

            vc, rc, *dirs_c = _gla_proj(xc, mod_c, g_mix, w_in, w1, w2, gb, tm=tm_c)
            v, r, *dirs_l = _gla_proj(x, mod_l, g_mix, w_in, w1, w2, gb, tm=tm_l)
            fwd_c, bwd_c, fwd_l, bwd_l = dirs_c[:2], dirs_c[2:], dirs_l[:2], dirs_l[2:]
            if last:
                _, s_fwd = _gla_scan(*fwd_c, vc, s0, direction=0, want_o=False)
                _, s_bwd = _gla_scan(*bwd_c, vc, s0, direction=1, want_o=False)
            else:
                oc_b, s_bwd = _gla_scan(*bwd_c, vc, s0, direction=1)
                xc, s_fwd = _gla_scan(*fwd_c, vc, s0, direction=0, fused=(oc_b, rc, xc, mod_c, gn, w_out))
            o_b, _ = _gla_scan(*bwd_l, v, s_bwd, direction=1)
            x, _ = _gla_scan(*fwd_l, v, s_fwd, direction=0, fused=(o_b, r, x, mod_l, gn, w_out))
        else:
            w = _mla_weights(mla_w_down[j], mla_q_lora_norm[j], mla_kv_lora_norm[j], mla_w_uq[j],
                             mla_w_ukv[j], mla_q_norm[j], mla_k_norm[j])
            w_out = mla_w_out[j].astype(BF16)
            if last:
                kc, vtc = _mla_proj(xc, mod_c, g_mix, cos_c, sin_c, w, want_q=False, tm=tm_c)
            else:
                qc, kc, vtc = _mla_proj(xc, mod_c, g_mix, cos_c, sin_c, w, want_q=True, tm=tm_c)
            q, k, vt = _mla_proj(x, mod_l, g_mix, cos_l, sin_l, w, want_q=True, tm=min(seq, MLA_PROJ_TILE))
            mix_l = (_latent_attention(w["bounded"], q, kc, vtc, k, vt), w_out)
            if not last:
                mix_c = (_attention(qc, kc, vtc, tq=min(lc, ATTN_EXACT_TILE), heads=MLA_HEADS), w_out)

        g_ffn = norm_ffn[i][None, :]
        wa = ffn_w_up[i, :, :d_ff].astype(BF16)
        wg = ffn_w_up[i, :, d_ff:].astype(BF16)
        conv = jnp.concatenate([ffn_conv_w[i], ffn_conv_b[i][None, :]], axis=0)
        ca, cg = conv[:, :d_ff], conv[:, d_ff:]
        wd = ffn_w_down[i].astype(BF16)
        x = _conv_ffn(x, mod_l, g_ffn, wa, wg, ca, cg, wd, tm=tm_l, fc=FFN_COLS, mixer_out=mix_l)
        if not last:
            xc = _conv_ffn(xc, mod_c, g_ffn, wa, wg, ca, cg, wd, tm=tm_c, fc=FFN_COLS, mixer_out=mix_c)
    return x
```
